```python
import jax, jax.numpy as jnp
from jax import lax
import numpy as np

D_MODEL = 1024
BATCH = 8
SEQ = 2048
DEPTH = 1
DEC_BATCH = 128
DEC_SEQ = 4
PAST_LEN = 16384
PAGE_SIZE = 128

N_META = 16
RET_HEADS = 4
RET_DK = 64
RET_DV = 128
RET_CHUNK = 128
RWKV_HEADS = 8
RWKV_HD = 64
RWKV_W = RWKV_HEADS * RWKV_HD
DECAY_LORA = 64
AAA_LORA = 64
GATE_LORA = 128
D_FF = 2816
ROPE_BASE = 10000.0
NORM_EPS = 1e-6
RET_GN_EPS = 1e-6
RWKV_GN_EPS = 64e-5

RET_QK = RET_HEADS * RET_DK
RET_V = RET_HEADS * RET_DV
SHIFT_W = 3 * RWKV_W + DECAY_LORA + AAA_LORA + GATE_LORA
GATE_W = 2 * D_MODEL
PROJ_W = 2 * RET_QK + 2 * RET_V + SHIFT_W + GATE_W

kernel_name = 'hybrid_retention_rwkv7_decoder'


def _rmsnorm(x, g):
    x32 = x.astype(jnp.float32)
    y = x32 * lax.rsqrt(jnp.mean(x32 * x32, axis=-1, keepdims=True) + NORM_EPS)
    return (y * g.astype(jnp.float32)).astype(x.dtype)


def _swiglu(x, w_gate, w_up, w_down):
    return (jax.nn.silu(x @ w_gate) * (x @ w_up)) @ w_down


def _rotary(x, pos):
    half = x.shape[-1] // 2
    inv_freq = ROPE_BASE ** (-jnp.arange(half, dtype=jnp.float32) / half)
    ang = pos.astype(jnp.float32)[:, None] * inv_freq[None, :]
    cos = jnp.cos(ang)[None, :, None, :]
    sin = jnp.sin(ang)[None, :, None, :]
    x1, x2 = x[..., :half], x[..., half:]
    return jnp.concatenate([x1 * cos - x2 * sin, x1 * sin + x2 * cos], axis=-1)


def _head_norm(y, eps):
    mu = jnp.mean(y, axis=-1, keepdims=True)
    yc = y - mu
    var = jnp.mean(yc * yc, axis=-1, keepdims=True)
    out = yc * lax.rsqrt(var + eps)
    return out.reshape(y.shape[0], y.shape[1], -1)


def _retention_chunk(S, q, k, v, log_gamma):
    L = q.shape[1]
    idx = jnp.arange(L, dtype=jnp.float32)
    diff = idx[:, None] - idx[None, :]
    mask = jnp.where(diff >= 0, jnp.exp(log_gamma[:, None, None] * jnp.maximum(diff, 0.0)), 0.0)
    scores = jnp.einsum('bihd,bjhd->bhij', q, k) * mask[None]
    o = jnp.einsum('bhij,bjhv->bihv', scores, v)
    q_decay = jnp.exp(log_gamma[:, None] * (idx + 1.0)[None, :])
    o = o + jnp.einsum('bihd,bhdv,hi->bihv', q, S, q_decay)
    k_decay = jnp.exp(log_gamma[:, None] * (L - 1.0 - idx)[None, :])
    S_new = jnp.exp(log_gamma * L)[None, :, None, None] * S + jnp.einsum('bjhd,hj,bjhv->bhdv', k, k_decay, v)
    return o, S_new


def _retention_mix(q, k, v, S, log_gamma, lead):
    B, T = q.shape[0], q.shape[1]
    outs = []
    if lead > 0:
        o, S = _retention_chunk(S, q[:, :lead], k[:, :lead], v[:, :lead], log_gamma)
        outs.append(o)
    n_full, rem = divmod(T - lead, RET_CHUNK)
    if n_full > 0:
        def to_blocks(a):
            s = a[:, lead:lead + n_full * RET_CHUNK]
            return s.reshape((B, n_full, RET_CHUNK) + a.shape[2:]).swapaxes(0, 1)

        def step(S_c, blk):
            qc, kc, vc = blk
            o_c, S_c = _retention_chunk(S_c, qc, kc, vc, log_gamma)
            return S_c, o_c

        S, o_blocks = lax.scan(step, S, (to_blocks(q), to_blocks(k), to_blocks(v)))
        outs.append(o_blocks.swapaxes(0, 1).reshape((B, n_full * RET_CHUNK) + v.shape[2:]))
    if rem > 0:
        s0 = lead + n_full * RET_CHUNK
        o, S = _retention_chunk(S, q[:, s0:], k[:, s0:], v[:, s0:], log_gamma)
        outs.append(o)
    o_all = jnp.concatenate(outs, axis=1) if len(outs) > 1 else outs[0]
    return o_all, S


def _rwkv7_mix(p, prev, S, mu_shift, w0, w2, a0, a2, g2, k_k, k_a, r_k, lnx_g, lnx_b):
    B, T, _ = p.shape
    p_prev = jnp.concatenate([prev[:, None, :], p[:, :-1]], axis=1)
    pm = p + (p_prev - p) * mu_shift
    splits = [RWKV_W, 2 * RWKV_W, 3 * RWKV_W, 3 * RWKV_W + DECAY_LORA, 3 * RWKV_W + DECAY_LORA + AAA_LORA]
    r, k, v, xw, xa, xg = jnp.split(pm, splits, axis=-1)
    w = -jax.nn.softplus(-(w0 + jnp.tanh(xw) @ w2)) - 0.5
    decay = jnp.exp(-jnp.exp(w))
    a = jax.nn.sigmoid(a0 + xa @ a2)
    g = jax.nn.sigmoid(xg) @ g2

    def hs(t):
        return t.reshape(B, T, RWKV_HEADS, RWKV_HD)

    kk = hs(k * k_k)
    kk = kk * lax.rsqrt(jnp.maximum(jnp.sum(kk * kk, axis=-1, keepdims=True), 1e-24))
    k = k * (1.0 + (a - 1.0) * k_a)
    r_h, k_h, v_h, a_h = hs(r), hs(k), hs(v), hs(a)
    b_h = kk * a_h

    def step(S_t, inp):
        r_t, w_t, k_t, v_t, kk_t, b_t = inp
        sa = jnp.einsum('bhvk,bhk->bhv', S_t, -kk_t)
        S_t = S_t * w_t[:, :, None, :] + sa[..., :, None] * b_t[..., None, :] + v_t[..., :, None] * k_t[..., None, :]
        y_t = jnp.einsum('bhvk,bhk->bhv', S_t, r_t)
        return S_t, y_t

    seq = tuple(jnp.moveaxis(t, 1, 0) for t in (r_h, hs(decay), k_h, v_h, kk, b_h))
    S, y = lax.scan(step, S, seq)
    y = jnp.moveaxis(y, 0, 1)
    out = _head_norm(y, RWKV_GN_EPS) * lnx_g + lnx_b
    bonus = jnp.sum(r_h * k_h * r_k, axis=-1, keepdims=True) * v_h
    out = (out + bonus.reshape(B, T, RWKV_W)) * g
    return out, S, p[:, -1]


def _layer(x, pos, lead, S_ret, S_wkv, shift_prev,
           ffn1_norm, ffn1_w_gate, ffn1_w_up, ffn1_w_down, mix_norm, w_in, ret_gn_g,
           mu_shift, w0, w2, a0, a2, g2, k_k, k_a, r_k, lnx_g, lnx_b,
           w_out_ret, w_out_rwkv, w_out, ffn2_norm, ffn2_w_gate, ffn2_w_up, ffn2_w_down):
    B, T, _ = x.shape
    h = x + 0.5 * _swiglu(_rmsnorm(x, ffn1_norm), ffn1_w_gate, ffn1_w_up, ffn1_w_down)
    u = _rmsnorm(h, mix_norm)
    p = (u @ w_in).astype(jnp.float32)
    splits = [RET_QK, 2 * RET_QK, 2 * RET_QK + RET_V, 2 * RET_QK + 2 * RET_V, 2 * RET_QK + 2 * RET_V + SHIFT_W]
    q, k, v, g_ret, p_rwkv, gates = jnp.split(p, splits, axis=-1)
    log_gamma = jnp.log1p(-jnp.exp2(-5.0 - jnp.arange(RET_HEADS, dtype=jnp.float32)))
    q = _rotary(q.reshape(B, T, RET_HEADS, RET_DK), pos)
    k = _rotary(k.reshape(B, T, RET_HEADS, RET_DK), pos) * (RET_DK ** -0.5)
    v = v.reshape(B, T, RET_HEADS, RET_DV)
    o_ret, S_ret_new = _retention_mix(q, k, v, S_ret.astype(jnp.float32), log_gamma, lead)
    o_ret = _head_norm(o_ret, RET_GN_EPS) * ret_gn_g * jax.nn.silu(g_ret)
    o_rwkv, S_wkv_new, shift_new = _rwkv7_mix(p_rwkv, shift_prev.astype(jnp.float32), S_wkv.astype(jnp.float32),
                                             mu_shift, w0, w2, a0, a2, g2, k_k, k_a, r_k, lnx_g, lnx_b)
    gate_a, gate_b = jnp.split(jax.nn.sigmoid(gates), 2, axis=-1)
    merged = gate_a * (o_ret @ w_out_ret) + gate_b * (o_rwkv @ w_out_rwkv)
    h = h + (merged @ w_out).astype(h.dtype)
    h = h + 0.5 * _swiglu(_rmsnorm(h, ffn2_norm), ffn2_w_gate, ffn2_w_up, ffn2_w_down)
    return h, S_ret_new, S_wkv_new, shift_new


def setup_inputs(seed: int = 0) -> dict:
    key = jax.random.key(seed)
    ks = jax.random.split(key, 32)
    f32 = jnp.float32

    def nrm(k, shape, scale):
        return scale * jax.random.normal(k, shape, f32)

    def gain(k, n):
        return 1.0 + 0.05 * jax.random.normal(k, (n,), f32)

    return {
        'x_prompt': nrm(ks[0], (BATCH, SEQ, D_MODEL), 1.0),
        'x_sample': nrm(ks[1], (DEC_BATCH, DEC_SEQ, D_MODEL), 1.0),
        'state_ret': nrm(ks[2], (DEC_BATCH, RET_HEADS, RET_DK, RET_DV), 0.5),
        'state_wkv': nrm(ks[3], (DEC_BATCH, RWKV_HEADS, RWKV_HD, RWKV_HD), 0.3),
        'state_shift': nrm(ks[4], (DEC_BATCH, SHIFT_W), 1.0),
        'meta_tokens': nrm(ks[5], (N_META, D_MODEL), 1.0),
        'ffn1_norm': gain(ks[6], D_MODEL),
        'ffn1_w_gate': nrm(ks[7], (D_MODEL, D_FF), D_MODEL ** -0.5),
        'ffn1_w_up': nrm(ks[8], (D_MODEL, D_FF), D_MODEL ** -0.5),
        'ffn1_w_down': nrm(ks[9], (D_FF, D_MODEL), D_FF ** -0.5),
        'mix_norm': gain(ks[10], D_MODEL),
        'w_in': nrm(ks[11], (D_MODEL, PROJ_W), D_MODEL ** -0.5),
        'ret_gn_g': gain(ks[12], RET_V),
        'mu_shift': jax.random.uniform(ks[13], (SHIFT_W,), f32, 0.0, 1.0),
        'w0': jax.random.uniform(ks[14], (RWKV_W,), f32, -6.5, -1.5),
        'w2': nrm(ks[15], (DECAY_LORA, RWKV_W), 0.1 * DECAY_LORA ** -0.5),
        'a0': nrm(ks[16], (RWKV_W,), 0.1),
        'a2': nrm(ks[17], (AAA_LORA, RWKV_W), 0.1 * AAA_LORA ** -0.5),
        'g2': nrm(ks[18], (GATE_LORA, RWKV_W), GATE_LORA ** -0.5),
        'k_k': 0.85 + 0.05 * jax.random.normal(ks[19], (RWKV_W,), f32),
        'k_a': gain(ks[20], RWKV_W),
        'r_k': nrm(ks[21], (RWKV_HEADS, RWKV_HD), 0.1),
        'lnx_g': gain(ks[22], RWKV_W),
        'lnx_b': nrm(ks[23], (RWKV_W,), 0.02),
        'w_out_ret': nrm(ks[24], (RET_V, D_MODEL), RET_V ** -0.5),
        'w_out_rwkv': nrm(ks[25], (RWKV_W, D_MODEL), RWKV_W ** -0.5),
        'w_out': nrm(ks[26], (D_MODEL, D_MODEL), D_MODEL ** -0.5),
        'ffn2_norm': gain(ks[27], D_MODEL),
        'ffn2_w_gate': nrm(ks[28], (D_MODEL, D_FF), D_MODEL ** -0.5),
        'ffn2_w_up': nrm(ks[29], (D_MODEL, D_FF), D_MODEL ** -0.5),
        'ffn2_w_down': nrm(ks[30], (D_FF, D_MODEL), D_FF ** -0.5),
        'final_norm': gain(ks[31], D_MODEL),
    }


def reference(x_prompt, x_sample, state_ret, state_wkv, state_shift, meta_tokens,
              ffn1_norm, ffn1_w_gate, ffn1_w_up, ffn1_w_down, mix_norm, w_in, ret_gn_g,
              mu_shift, w0, w2, a0, a2, g2, k_k, k_a, r_k, lnx_g, lnx_b,
              w_out_ret, w_out_rwkv, w_out, ffn2_norm, ffn2_w_gate, ffn2_w_up, ffn2_w_down, final_norm):
    weights = (ffn1_norm, ffn1_w_gate, ffn1_w_up, ffn1_w_down, mix_norm, w_in, ret_gn_g,
               mu_shift, w0, w2, a0, a2, g2, k_k, k_a, r_k, lnx_g, lnx_b,
               w_out_ret, w_out_rwkv, w_out, ffn2_norm, ffn2_w_gate, ffn2_w_up, ffn2_w_down)
    st_dtype = state_ret.dtype
    Bp = x_prompt.shape[0]
    meta = jnp.broadcast_to(meta_tokens.astype(x_prompt.dtype)[None], (Bp, N_META, D_MODEL))
    h_p = jnp.concatenate([meta, x_prompt], axis=1)
    pos_p = jnp.arange(h_p.shape[1], dtype=jnp.int32)
    S_ret_p = jnp.zeros((Bp, RET_HEADS, RET_DK, RET_DV), jnp.float32)
    S_wkv_p = jnp.zeros((Bp, RWKV_HEADS, RWKV_HD, RWKV_HD), jnp.float32)
    sh_p = jnp.zeros((Bp, SHIFT_W), jnp.float32)
    h_s = x_sample
    pos_s = PAST_LEN + jnp.arange(x_sample.shape[1], dtype=jnp.int32)
    S_ret_s, S_wkv_s, sh_s = state_ret, state_wkv, state_shift
    for _ in range(DEPTH):
        h_p, S_ret_p, S_wkv_p, sh_p = _layer(h_p, pos_p, N_META, S_ret_p, S_wkv_p, sh_p, *weights)
        h_s, S_ret_s, S_wkv_s, sh_s = _layer(h_s, pos_s, 0, S_ret_s, S_wkv_s, sh_s, *weights)
    y_prompt = _rmsnorm(h_p, final_norm)[:, N_META:].astype(x_prompt.dtype)
    y_sample = _rmsnorm(h_s, final_norm).astype(x_sample.dtype)
    return (y_prompt, y_sample,
            S_ret_p.astype(st_dtype), S_wkv_p.astype(st_dtype), sh_p.astype(st_dtype),
            S_ret_s.astype(st_dtype), S_wkv_s.astype(st_dtype), sh_s.astype(st_dtype))
```

```python
import functools

import numpy as np
import jax
import jax.numpy as jnp
from jax import lax
from jax.experimental import pallas as pl
from jax.experimental.pallas import tpu as pltpu

D_MODEL = 1024
N_META = 16
PAST_LEN = 16384
RET_HEADS = 4
RET_DK = 64
RET_DV = 128
RET_CHUNK = 128
RWKV_HEADS = 8
RWKV_HD = 64
RWKV_W = RWKV_HEADS * RWKV_HD
DECAY_LORA = 64
AAA_LORA = 64
GATE_LORA = 128
D_FF = 2816
ROPE_BASE = 10000.0
NORM_EPS = 1e-6
RET_GN_EPS = 1e-6
RWKV_GN_EPS = 64e-5
RET_QK = RET_HEADS * RET_DK
RET_V = RET_HEADS * RET_DV
SHIFT_W = 3 * RWKV_W + DECAY_LORA + AAA_LORA + GATE_LORA
GATE_W = 2 * D_MODEL
PROJ_W = 2 * RET_QK + 2 * RET_V + SHIFT_W + GATE_W

_C_Q, _C_K, _C_V, _C_G = 0, RET_QK, 2 * RET_QK, 2 * RET_QK + RET_V
_C_P = 2 * RET_QK + 2 * RET_V
_C_GATE = _C_P + SHIFT_W

V7X_VMEM_LIMIT_BYTES = 56 * 1024 * 1024
FF_CHUNK = 256
RWKV_CHUNK = 64
LOG_GAMMA = tuple(float(np.log1p(-2.0 ** (-5.0 - h))) for h in range(RET_HEADS))

f32 = jnp.float32
bf16 = jnp.bfloat16


def _resident(shape):
    zeros = (0,) * len(shape)
    return pl.BlockSpec(shape, lambda *_: zeros, pipeline_mode=pl.Buffered(1))


def _rms(x, g):
    return x * lax.rsqrt(jnp.mean(x * x, axis=-1, keepdims=True) + NORM_EPS) * g


def _dot(a, b):
    return jnp.dot(a, b, preferred_element_type=f32)


def _dot_nt(a, b):
    return lax.dot_general(a, b, (((1,), (1,)), ((), ())), preferred_element_type=f32)


def _dot_tn(a, b):
    return lax.dot_general(a, b, (((0,), (0,)), ((), ())), preferred_element_type=f32)


def _ffn_kernel(x_ref, g_ref, wg_ref, wu_ref, wd_ref, fin_ref, o_ref, *, final_norm):
    x = x_ref[...]
    xn = _rms(x, g_ref[...]).astype(bf16)
    acc = jnp.zeros(x.shape, f32)
    for c in range(D_FF // FF_CHUNK):
        sl = slice(c * FF_CHUNK, (c + 1) * FF_CHUNK)
        gt = _dot(xn, wg_ref[:, sl])
        up = _dot(xn, wu_ref[:, sl])
        act = (gt * jax.nn.sigmoid(gt) * up).astype(bf16)
        acc = acc + _dot(act, wd_ref[sl, :])
    h = x + 0.5 * acc
    if final_norm:
        h = _rms(h, fin_ref[...])
    o_ref[...] = h


def _ffn(x, norm_g, wg, wu, wd, fin_g, *, tm, final_norm):
    rows = x.shape[0]
    row = pl.BlockSpec((tm, D_MODEL), lambda i: (i, 0))
    return pl.pallas_call(
        functools.partial(_ffn_kernel, final_norm=final_norm),
        grid=(rows // tm,),
        in_specs=[row, _resident((1, D_MODEL)), _resident((D_MODEL, D_FF)), _resident((D_MODEL, D_FF)),
                  _resident((D_FF, D_MODEL)), _resident((1, D_MODEL))],
        out_specs=row,
        out_shape=jax.ShapeDtypeStruct((rows, D_MODEL), f32),
        compiler_params=pltpu.CompilerParams(dimension_semantics=("parallel",),
                                             vmem_limit_bytes=V7X_VMEM_LIMIT_BYTES),
        name="ffn_final" if final_norm else "ffn",
    )(x, norm_g, wg, wu, wd, fin_g)


def _swap_halves(x):
    parts = []
    for j in range(x.shape[1] // 128):
        xs = x[:, 128 * j:128 * (j + 1)]
        fwd = pltpu.roll(xs, 32, 1)
        bwd = pltpu.roll(xs, 96, 1)
        lane = lax.broadcasted_iota(jnp.int32, xs.shape, 1)
        parts.append(jnp.where((lane % RET_DK) < RET_DK // 2, bwd, fwd))
    return jnp.concatenate(parts, axis=1)


def _inproj_kernel(h_ref, g_ref, w_ref, cos_ref, sin_ref, q_ref, k_ref, v_ref, sg_ref, p_ref, gate_ref):
    un = _rms(h_ref[...], g_ref[...]).astype(bf16)
    cos = cos_ref[...]
    sin = sin_ref[...]
    q = _dot(un, w_ref[:, _C_Q:_C_K])
    q_ref[...] = q * cos + _swap_halves(q) * sin
    k = _dot(un, w_ref[:, _C_K:_C_V])
    k_ref[...] = (k * cos + _swap_halves(k) * sin) * (RET_DK ** -0.5)
    v_ref[...] = _dot(un, w_ref[:, _C_V:_C_G])
    gr = _dot(un, w_ref[:, _C_G:_C_P])
    sg_ref[...] = gr * jax.nn.sigmoid(gr)
    p_ref[...] = _dot(un, w_ref[:, _C_P:_C_GATE])
    gate_ref[...] = jax.nn.sigmoid(_dot(un, w_ref[:, _C_GATE:PROJ_W]))


def _inproj(h, norm_g, w_in, cos, sin, *, tm):
    rows = h.shape[0]
    tab_blocks = cos.shape[0] // tm

    def rowspec(width):
        return pl.BlockSpec((tm, width), lambda i: (i, 0))

    tab = pl.BlockSpec((tm, RET_QK), lambda i: (i % tab_blocks, 0))
    widths = (RET_QK, RET_QK, RET_V, RET_V, SHIFT_W, GATE_W)
    return pl.pallas_call(
        _inproj_kernel,
        grid=(rows // tm,),
        in_specs=[rowspec(D_MODEL), _resident((1, D_MODEL)), _resident((D_MODEL, PROJ_W)), tab, tab],
        out_specs=[rowspec(w) for w in widths],
        out_shape=[jax.ShapeDtypeStruct((rows, w), f32) for w in widths],
        compiler_params=pltpu.CompilerParams(dimension_semantics=("parallel",),
                                             vmem_limit_bytes=V7X_VMEM_LIMIT_BYTES),
        name="inproj",
    )(h, norm_g, w_in, cos, sin)


def _ret_kernel(q_ref, k_ref, v_ref, sg_ref, s0_ref, gn_ref, o_ref, s_out_ref, s_scr, *, rows, valid):
    c = pl.program_id(1)

    @pl.when(c == 0)
    def _():
        s_scr[...] = s0_ref[0]

    q = q_ref[0]
    k = k_ref[0]
    v = v_ref[0]
    sg = sg_ref[0]
    gn = gn_ref[...]
    ii = lax.broadcasted_iota(jnp.int32, (rows, rows), 0)
    jj = lax.broadcasted_iota(jnp.int32, (rows, rows), 1)
    diff = (ii - jj).astype(f32)
    row = lax.broadcasted_iota(jnp.int32, (rows, 1), 0).astype(f32)
    for h in range(RET_HEADS):
        lg = LOG_GAMMA[h]
        mask = jnp.where(diff >= 0, jnp.exp(lg * jnp.maximum(diff, 0.0)), 0.0)
        qh = q[:, RET_DK * h:RET_DK * (h + 1)]
        kh = k[:, RET_DK * h:RET_DK * (h + 1)]
        vh = v[:, RET_DV * h:RET_DV * (h + 1)].astype(bf16)
        scores = _dot_nt(qh.astype(bf16), kh.astype(bf16)) * mask
        q_decay = jnp.exp(lg * (row + 1.0))
        k_decay = jnp.exp(lg * (valid - 1.0 - row))
        s_old = s_scr[h]
        o = _dot(scores.astype(bf16), vh) + _dot((qh * q_decay).astype(bf16), s_old.astype(bf16))
        s_scr[h] = float(np.exp(lg * valid)) * s_old + _dot_tn((kh * k_decay).astype(bf16), vh)
        mu = jnp.mean(o, axis=-1, keepdims=True)
        oc = o - mu
        var = jnp.mean(oc * oc, axis=-1, keepdims=True)
        sl = slice(RET_DV * h, RET_DV * (h + 1))
        o_ref[0, :, sl] = oc * lax.rsqrt(var + RET_GN_EPS) * gn[:, sl] * sg[:, sl]

    @pl.when(c == pl.num_programs(1) - 1)
    def _():
        s_out_ref[0] = s_scr[...]


def _retention(q, k, v, sg, s0, gn, *, rows, valid):
    B, T, _ = q.shape
    bcast = s0.shape[0] == 1

    def seq(width):
        return pl.BlockSpec((1, rows, width), lambda b, c: (b, c, 0))

    state = pl.BlockSpec((1, RET_HEADS, RET_DK, RET_DV), lambda b, c: (b, 0, 0, 0))
    state_in = pl.BlockSpec((1, RET_HEADS, RET_DK, RET_DV), lambda b, c: (0, 0, 0, 0)) if bcast else state
    return pl.pallas_call(
        functools.partial(_ret_kernel, rows=rows, valid=valid),
        grid=(B, T // rows),
        in_specs=[seq(RET_QK), seq(RET_QK), seq(RET_V), seq(RET_V), state_in, _resident((1, RET_V))],
        out_specs=[seq(RET_V), state],
        out_shape=[jax.ShapeDtypeStruct((B, T, RET_V), f32),
                   jax.ShapeDtypeStruct((B, RET_HEADS, RET_DK, RET_DV), f32)],
        scratch_shapes=[pltpu.VMEM((RET_HEADS, RET_DK, RET_DV), f32)],
        compiler_params=pltpu.CompilerParams(dimension_semantics=("parallel", "arbitrary"),
                                             vmem_limit_bytes=V7X_VMEM_LIMIT_BYTES),
        name="retention",
    )(q, k, v, sg, s0, gn)


def _split3(x):
    hi = x.astype(bf16)
    r1 = x - hi.astype(f32)
    mid = r1.astype(bf16)
    lo = (r1 - mid.astype(f32)).astype(bf16)
    return hi, mid, lo


def _rwkv_kernel(p_ref, prev0_ref, h0_ref, mu_ref, w0_ref, w2_ref, a0_ref, a2_ref, g2_ref, kk_ref, ka_ref,
                 rk_ref, lng_ref, lnb_ref, ones_ref, o_ref, h_out_ref, shift_ref, h_scr, prev_scr, y_scr,
                 *, rows, valid):
    c = pl.program_id(1)
    C = rows
    N = RWKV_HD

    @pl.when(c == 0)
    def _():
        h_scr[...] = h0_ref[0]
        prev_scr[...] = prev0_ref[0]

    p = p_ref[0]
    row1 = lax.broadcasted_iota(jnp.int32, (C, 1), 0)
    p_prev = jnp.where(row1 == 0, prev_scr[...], pltpu.roll(p, 1, 0))
    prev_scr[...] = p[valid - 1:valid, :]
    pm = p + (p_prev - p) * mu_ref[...]
    r = pm[:, 0:RWKV_W]
    k = pm[:, RWKV_W:2 * RWKV_W]
    v = pm[:, 2 * RWKV_W:3 * RWKV_W]
    o_w = 3 * RWKV_W
    xw = pm[:, o_w:o_w + DECAY_LORA]
    xa = pm[:, o_w + DECAY_LORA:o_w + DECAY_LORA + AAA_LORA]
    xg = pm[:, o_w + DECAY_LORA + AAA_LORA:SHIFT_W]

    z = w0_ref[...] + _dot(jnp.tanh(xw).astype(bf16), w2_ref[...])
    nz = -z
    softplus = jnp.maximum(nz, 0.0) + jnp.log(1.0 + jnp.exp(-jnp.abs(nz)))
    ld = -jnp.exp(-softplus - 0.5)
    a = jax.nn.sigmoid(a0_ref[...] + _dot(xa.astype(bf16), a2_ref[...]))
    g = _dot(jax.nn.sigmoid(xg).astype(bf16), g2_ref[...])
    ones_bd = ones_ref[...]
    kk = k * kk_ref[...]
    kk = kk * lax.rsqrt(jnp.maximum(_dot((kk * kk).astype(bf16), ones_bd), 1e-24))
    kp = k * (1.0 + (a - 1.0) * ka_ref[...])
    if valid < C:
        live = (row1 < valid).astype(f32)
        ld = ld * live
        kk = kk * live
        kp = kp * live
        v = v * live
    b = kk * a

    ti = lax.broadcasted_iota(jnp.int32, (C, C), 0)
    tj = lax.broadcasted_iota(jnp.int32, (C, C), 1)
    incl = ti >= tj
    strict = ti > tj
    tri = incl.astype(bf16)
    ld_hi, ld_mid, ld_lo = _split3(ld)
    cum = _dot(tri, ld_hi) + _dot(tri, ld_mid) + _dot(tri, ld_lo)
    cum_last = cum[C - 1:C, :]
    e_in = jnp.exp(cum)
    e_ex = jnp.exp(cum - ld)
    e_neg = jnp.exp(-cum)
    e_end = jnp.exp(cum_last - cum)
    g_end = jnp.exp(cum_last)
    at = -kk * e_ex
    rt = r * e_in
    bt = (b * e_neg).astype(bf16)
    kt = (kp * e_neg).astype(bf16)
    bh = (b * e_end).astype(bf16)
    kh = (kp * e_end).astype(bf16)
    vb = v.astype(bf16)
    eye = ti == tj
    ki = lax.broadcasted_iota(jnp.int32, (N, N), 0)
    kj = lax.broadcasted_iota(jnp.int32, (N, N), 1)
    eye_n = ki == kj

    for h in range(RWKV_HEADS):
        sl = slice(N * h, N * (h + 1))
        at_h = at[:, sl]
        rt_h = rt[:, sl]
        ar = jnp.concatenate([at_h, rt_h], axis=0).astype(bf16)
        gb = _dot_nt(ar, bt[:, sl])
        gk = _dot_nt(ar, kt[:, sl])
        lab = jnp.where(strict, gb[:C], 0.0)
        lak = jnp.where(strict, gk[:C], 0.0)
        mrb = jnp.where(incl, gb[C:], 0.0).astype(bf16)
        mrk = jnp.where(incl, gk[C:], 0.0).astype(bf16)
        pinv = jnp.where(eye, 1.0, lab)
        m = lab
        n = 1
        while 2 * n < C:
            mb = m.astype(bf16)
            m = _dot(mb, mb)
            pinv = pinv + _dot(pinv.astype(bf16), m.astype(bf16))
            n *= 2
        tinv = pinv.astype(bf16)
        v_h = vb[:, sl]
        lakv = _dot(lak.astype(bf16), v_h)
        wa = _dot(tinv, at_h.astype(bf16)).astype(bf16)
        uv = _dot(tinv, lakv.astype(bf16)).astype(bf16)
        ry = rt_h + _dot(mrb, wa)
        y0 = _dot(mrb, uv) + _dot(mrk, v_h)
        bh_h = bh[:, sl]
        gmat = jnp.where(eye_n, g_end[:, sl], 0.0) + _dot_tn(bh_h, wa)
        dmat = _dot_tn(bh_h, uv) + _dot_tn(kh[:, sl], v_h)
        h_old = h_scr[h].astype(bf16)
        y_scr[:, sl] = _dot(ry.astype(bf16), h_old) + y0
        h_scr[h] = _dot(gmat.astype(bf16), h_old) + dmat

    y = y_scr[...]
    inv_n = 1.0 / N
    mean = _dot(y.astype(bf16), ones_bd) * inv_n
    yc = y - mean
    var = _dot((yc * yc).astype(bf16), ones_bd) * inv_n
    out = yc * lax.rsqrt(var + RWKV_GN_EPS) * lng_ref[...] + lnb_ref[...]
    bonus = _dot((r * kp * rk_ref[...]).astype(bf16), ones_bd) * v
    o_ref[0] = (out + bonus) * g

    @pl.when(c == pl.num_programs(1) - 1)
    def _():
        h_out_ref[0] = h_scr[...]
        shift_ref[0] = prev_scr[...]


def _rwkv(p, prev0, h0, params, ones_bd, *, rows, valid):
    B, T, _ = p.shape

    def maybe_bcast(arr, block):
        nd = len(block)
        if arr.shape[0] == 1:
            return pl.BlockSpec(block, lambda b, c: (0,) * nd)
        return pl.BlockSpec(block, lambda b, c: (b,) + (0,) * (nd - 1))

    seq_in = pl.BlockSpec((1, rows, SHIFT_W), lambda b, c: (b, c, 0))
    seq_out = pl.BlockSpec((1, rows, RWKV_W), lambda b, c: (b, c, 0))
    st_block = (1, RWKV_HEADS, RWKV_HD, RWKV_HD)
    sh_block = (1, 1, SHIFT_W)
    param_specs = [_resident(x.shape) for x in params]
    return pl.pallas_call(
        functools.partial(_rwkv_kernel, rows=rows, valid=valid),
        grid=(B, T // rows),
        in_specs=[seq_in, maybe_bcast(prev0, sh_block), maybe_bcast(h0, st_block)] + param_specs
                 + [_resident(ones_bd.shape)],
        out_specs=[seq_out, pl.BlockSpec(st_block, lambda b, c: (b, 0, 0, 0)),
                   pl.BlockSpec(sh_block, lambda b, c: (b, 0, 0))],
        out_shape=[jax.ShapeDtypeStruct((B, T, RWKV_W), f32),
                   jax.ShapeDtypeStruct((B,) + st_block[1:], f32),
                   jax.ShapeDtypeStruct((B,) + sh_block[1:], f32)],
        scratch_shapes=[pltpu.VMEM(st_block[1:], f32), pltpu.VMEM((1, SHIFT_W), f32),
                        pltpu.VMEM((rows, RWKV_W), f32)],
        compiler_params=pltpu.CompilerParams(dimension_semantics=("parallel", "arbitrary"),
                                             vmem_limit_bytes=V7X_VMEM_LIMIT_BYTES),
        name="rwkv7",
    )(p, prev0, h0, *params, ones_bd)


def _merge_kernel(h_ref, oret_ref, orwkv_ref, gate_ref, wr_ref, ww_ref, wo_ref, o_ref):
    a = _dot(oret_ref[...].astype(bf16), wr_ref[...])
    b = _dot(orwkv_ref[...].astype(bf16), ww_ref[...])
    gate = gate_ref[...]
    merged = gate[:, :D_MODEL] * a + gate[:, D_MODEL:] * b
    o_ref[...] = h_ref[...] + _dot(merged.astype(bf16), wo_ref[...])


def _merge(h, o_ret, o_rwkv, gates, w_out_ret, w_out_rwkv, w_out, *, tm):
    rows = h.shape[0]

    def rowspec(width):
        return pl.BlockSpec((tm, width), lambda i: (i, 0))

    return pl.pallas_call(
        _merge_kernel,
        grid=(rows // tm,),
        in_specs=[rowspec(D_MODEL), rowspec(RET_V), rowspec(RWKV_W), rowspec(GATE_W),
                  _resident((RET_V, D_MODEL)), _resident((RWKV_W, D_MODEL)), _resident((D_MODEL, D_MODEL))],
        out_specs=rowspec(D_MODEL),
        out_shape=jax.ShapeDtypeStruct((rows, D_MODEL), f32),
        compiler_params=pltpu.CompilerParams(dimension_semantics=("parallel",),
                                             vmem_limit_bytes=V7X_VMEM_LIMIT_BYTES),
        name="merge",
    )(h, o_ret, o_rwkv, gates, w_out_ret, w_out_rwkv, w_out)


def _rotary_tables(pos):
    half = RET_DK // 2
    inv_freq = ROPE_BASE ** (-jnp.arange(half, dtype=f32) / half)
    ang = pos.astype(f32)[:, None] * inv_freq[None, :]
    cos = jnp.cos(ang)
    sin = jnp.sin(ang)
    cos_t = jnp.tile(jnp.concatenate([cos, cos], axis=1), (1, RET_HEADS))
    sin_t = jnp.tile(jnp.concatenate([-sin, sin], axis=1), (1, RET_HEADS))
    return cos_t, sin_t


def _row_tile(rows, target):
    tm = min(rows, target)
    while rows % tm:
        tm -= 8
    return tm


def kernel(x_prompt, x_sample, state_ret, state_wkv, state_shift, meta_tokens, ffn1_norm, ffn1_w_gate, ffn1_w_up, ffn1_w_down, mix_norm, w_in, ret_gn_g, mu_shift, w0, w2, a0, a2, g2, k_k, k_a, r_k, lnx_g, lnx_b, w_out_ret, w_out_rwkv, w_out, ffn2_norm, ffn2_w_gate, ffn2_w_up, ffn2_w_down, final_norm):
    Bp, Tp, _ = x_prompt.shape
    Bs, Ts, _ = x_sample.shape
    st_dtype = state_ret.dtype

    def row(x):
        return x.reshape(1, -1).astype(f32)

    ffn1 = (row(ffn1_norm), ffn1_w_gate.astype(bf16), ffn1_w_up.astype(bf16), ffn1_w_down.astype(bf16))
    ffn2 = (row(ffn2_norm), ffn2_w_gate.astype(bf16), ffn2_w_up.astype(bf16), ffn2_w_down.astype(bf16))
    fin = row(final_norm)
    w_in_b = w_in.astype(bf16)
    wr_b, ww_b, wo_b = w_out_ret.astype(bf16), w_out_rwkv.astype(bf16), w_out.astype(bf16)
    rwkv_params = (row(mu_shift), row(w0), w2.astype(bf16), row(a0), a2.astype(bf16), g2.astype(bf16),
                   row(k_k), row(k_a), row(r_k), row(lnx_g), row(lnx_b))
    head_id = jnp.arange(RWKV_W, dtype=jnp.int32) // RWKV_HD
    ones_bd = (head_id[:, None] == head_id[None, :]).astype(bf16)
    gn = row(ret_gn_g)

    def pre(x, cos, sin, tm_ffn, tm_proj):
        h = _ffn(x, *ffn1, fin, tm=tm_ffn, final_norm=False)
        return (h,) + tuple(_inproj(h, row(mix_norm), w_in_b, cos, sin, tm=tm_proj))

    def post(h, o_ret, o_rwkv, gates, tm):
        h2 = _merge(h, o_ret, o_rwkv, gates, wr_b, ww_b, wo_b, tm=tm)
        return _ffn(h2, *ffn2, fin, tm=tm, final_norm=True)

    n_s = Bs * Ts
    x_small = jnp.concatenate([x_sample.reshape(n_s, D_MODEL), meta_tokens.astype(x_sample.dtype)], axis=0)
    cos_s, sin_s = _rotary_tables(PAST_LEN + jnp.arange(Ts, dtype=jnp.int32))
    cos_m, sin_m = _rotary_tables(jnp.arange(N_META, dtype=jnp.int32))
    cos_small = jnp.concatenate([jnp.tile(cos_s, (Bs, 1)), cos_m], axis=0)
    sin_small = jnp.concatenate([jnp.tile(sin_s, (Bs, 1)), sin_m], axis=0)
    n_small = n_s + N_META
    h_small, q_s, k_s, v_s, sg_s, p_s, gate_s = pre(x_small, cos_small, sin_small, n_small, n_small)

    def meta(x):
        return x[n_s:].reshape(1, N_META, -1)

    zeros_ret = jnp.zeros((1, RET_HEADS, RET_DK, RET_DV), f32)
    _, s_ret_m = _retention(meta(q_s), meta(k_s), meta(v_s), meta(sg_s), zeros_ret, gn, rows=N_META, valid=N_META)
    _, h_wkv_m, shift_m = _rwkv(meta(p_s), jnp.zeros((1, 1, SHIFT_W), f32),
                                jnp.zeros((1, RWKV_HEADS, RWKV_HD, RWKV_HD), f32), rwkv_params, ones_bd,
                                rows=N_META, valid=N_META)

    pad_t = -(-Ts // 8) * 8

    def samp(x):
        x = x[:n_s].reshape(Bs, Ts, -1)
        return jnp.pad(x, ((0, 0), (0, pad_t - Ts), (0, 0)))

    o_ret_s, s_ret_s = _retention(samp(q_s), samp(k_s), samp(v_s), samp(sg_s), state_ret.astype(f32), gn,
                                  rows=pad_t, valid=Ts)
    o_wkv_s, h_wkv_s, shift_s = _rwkv(samp(p_s), state_shift.astype(f32).reshape(Bs, 1, SHIFT_W),
                                      jnp.swapaxes(state_wkv.astype(f32), -1, -2), rwkv_params, ones_bd,
                                      rows=pad_t, valid=Ts)
    y_sample = post(h_small[:n_s], o_ret_s[:, :Ts].reshape(n_s, RET_V), o_wkv_s[:, :Ts].reshape(n_s, RWKV_W),
                    gate_s[:n_s], _row_tile(n_s, 512))

    n_p = Bp * Tp
    cos_p, sin_p = _rotary_tables(N_META + jnp.arange(Tp, dtype=jnp.int32))
    h_p, q_p, k_p, v_p, sg_p, p_p, gate_p = pre(x_prompt.reshape(n_p, D_MODEL), cos_p, sin_p,
                                                _row_tile(n_p, 512), _row_tile(Tp, 256))

    def seqs(x):
        return x.reshape(Bp, Tp, -1)

    o_ret_p, s_ret_p = _retention(seqs(q_p), seqs(k_p), seqs(v_p), seqs(sg_p), s_ret_m, gn,
                                  rows=RET_CHUNK, valid=RET_CHUNK)
    o_wkv_p, h_wkv_p, shift_p = _rwkv(seqs(p_p), shift_m, h_wkv_m, rwkv_params, ones_bd,
                                      rows=RWKV_CHUNK, valid=RWKV_CHUNK)
    y_prompt = post(h_p, o_ret_p.reshape(n_p, RET_V), o_wkv_p.reshape(n_p, RWKV_W), gate_p, _row_tile(n_p, 512))

    return (y_prompt.reshape(Bp, Tp, D_MODEL).astype(x_prompt.dtype),
            y_sample.reshape(Bs, Ts, D_MODEL).astype(x_sample.dtype),
            s_ret_p.astype(st_dtype), jnp.swapaxes(h_wkv_p, -1, -2).astype(st_dtype),
            shift_p.reshape(Bp, SHIFT_W).astype(st_dtype),
            s_ret_s.astype(st_dtype), jnp.swapaxes(h_wkv_s, -1, -2).astype(st_dtype),
            shift_s.reshape(Bs, SHIFT_W).astype(st_dtype))
```

```python
import functools

import numpy as np
import jax
import jax.numpy as jnp
from jax import lax
from jax.experimental import pallas as pl
from jax.experimental.pallas import tpu as pltpu

D_MODEL = 1024
N_META = 16
PAST_LEN = 16384
RET_HEADS = 4
RET_DK = 64
RET_DV = 128
RET_CHUNK = 128
RWKV_HEADS = 8
RWKV_HD = 64
RWKV_W = RWKV_HEADS * RWKV_HD
DECAY_LORA = 64
AAA_LORA = 64
GATE_LORA = 128
D_FF = 2816
ROPE_BASE = 10000.0
NORM_EPS = 1e-6
RET_GN_EPS = 1e-6
RWKV_GN_EPS = 64e-5
RET_QK = RET_HEADS * RET_DK
RET_V = RET_HEADS * RET_DV
SHIFT_W = 3 * RWKV_W + DECAY_LORA + AAA_LORA + GATE_LORA
GATE_W = 2 * D_MODEL
PROJ_W = 2 * RET_QK + 2 * RET_V + SHIFT_W + GATE_W

_C_Q, _C_K, _C_V, _C_G = 0, RET_QK, 2 * RET_QK, 2 * RET_QK + RET_V
_C_P = 2 * RET_QK + 2 * RET_V
_C_GATE = _C_P + SHIFT_W

V7X_VMEM_LIMIT_BYTES = 56 * 1024 * 1024
FF_CHUNK = 256
RWKV_CHUNK = 64
RWKV_PROMPT_SEQS = 4
RWKV_SAMPLE_SEQS = 8
LOG_GAMMA = tuple(float(np.log1p(-2.0 ** (-5.0 - h))) for h in range(RET_HEADS))

f32 = jnp.float32
bf16 = jnp.bfloat16


def _resident(shape):
    zeros = (0,) * len(shape)
    return pl.BlockSpec(shape, lambda *_: zeros, pipeline_mode=pl.Buffered(1))


def _rms(x, g):
    return x * lax.rsqrt(jnp.mean(x * x, axis=-1, keepdims=True) + NORM_EPS) * g


def _dot(a, b):
    return jnp.dot(a, b, preferred_element_type=f32)


def _dot_nt(a, b):
    return lax.dot_general(a, b, (((1,), (1,)), ((), ())), preferred_element_type=f32)


def _dot_tn(a, b):
    return lax.dot_general(a, b, (((0,), (0,)), ((), ())), preferred_element_type=f32)


def _ffn_kernel(x_ref, g_ref, wg_ref, wu_ref, wd_ref, fin_ref, o_ref, *, final_norm):
    x = x_ref[...]
    xn = _rms(x, g_ref[...]).astype(bf16)
    acc = jnp.zeros(x.shape, f32)
    for c in range(D_FF // FF_CHUNK):
        sl = slice(c * FF_CHUNK, (c + 1) * FF_CHUNK)
        gt = _dot(xn, wg_ref[:, sl])
        up = _dot(xn, wu_ref[:, sl])
        act = (gt * jax.nn.sigmoid(gt) * up).astype(bf16)
        acc = acc + _dot(act, wd_ref[sl, :])
    h = x + 0.5 * acc
    if final_norm:
        h = _rms(h, fin_ref[...])
    o_ref[...] = h


def _ffn(x, norm_g, wg, wu, wd, fin_g, *, tm, final_norm):
    rows = x.shape[0]
    row = pl.BlockSpec((tm, D_MODEL), lambda i: (i, 0))
    return pl.pallas_call(
        functools.partial(_ffn_kernel, final_norm=final_norm),
        grid=(rows // tm,),
        in_specs=[row, _resident((1, D_MODEL)), _resident((D_MODEL, D_FF)), _resident((D_MODEL, D_FF)),
                  _resident((D_FF, D_MODEL)), _resident((1, D_MODEL))],
        out_specs=row,
        out_shape=jax.ShapeDtypeStruct((rows, D_MODEL), f32),
        compiler_params=pltpu.CompilerParams(dimension_semantics=("parallel",),
                                             vmem_limit_bytes=V7X_VMEM_LIMIT_BYTES),
        name="ffn_final" if final_norm else "ffn",
    )(x, norm_g, wg, wu, wd, fin_g)


def _swap_halves(x):
    parts = []
    for j in range(x.shape[1] // 128):
        xs = x[:, 128 * j:128 * (j + 1)]
        fwd = pltpu.roll(xs, 32, 1)
        bwd = pltpu.roll(xs, 96, 1)
        lane = lax.broadcasted_iota(jnp.int32, xs.shape, 1)
        parts.append(jnp.where((lane % RET_DK) < RET_DK // 2, bwd, fwd))
    return jnp.concatenate(parts, axis=1)


def _inproj_kernel(h_ref, g_ref, w_ref, cos_ref, sin_ref, q_ref, k_ref, v_ref, sg_ref, p_ref, gate_ref):
    un = _rms(h_ref[...], g_ref[...]).astype(bf16)
    cos = cos_ref[...]
    sin = sin_ref[...]
    q = _dot(un, w_ref[:, _C_Q:_C_K])
    q_ref[...] = q * cos + _swap_halves(q) * sin
    k = _dot(un, w_ref[:, _C_K:_C_V])
    k_ref[...] = (k * cos + _swap_halves(k) * sin) * (RET_DK ** -0.5)
    v_ref[...] = _dot(un, w_ref[:, _C_V:_C_G])
    gr = _dot(un, w_ref[:, _C_G:_C_P])
    sg_ref[...] = gr * jax.nn.sigmoid(gr)
    p_ref[...] = _dot(un, w_ref[:, _C_P:_C_GATE])
    gate_ref[...] = jax.nn.sigmoid(_dot(un, w_ref[:, _C_GATE:PROJ_W]))


def _inproj(h, norm_g, w_in, cos, sin, *, tm):
    rows = h.shape[0]
    tab_blocks = cos.shape[0] // tm

    def rowspec(width):
        return pl.BlockSpec((tm, width), lambda i: (i, 0))

    tab = pl.BlockSpec((tm, RET_QK), lambda i: (i % tab_blocks, 0))
    widths = (RET_QK, RET_QK, RET_V, RET_V, SHIFT_W, GATE_W)
    return pl.pallas_call(
        _inproj_kernel,
        grid=(rows // tm,),
        in_specs=[rowspec(D_MODEL), _resident((1, D_MODEL)), _resident((D_MODEL, PROJ_W)), tab, tab],
        out_specs=[rowspec(w) for w in widths],
        out_shape=[jax.ShapeDtypeStruct((rows, w), f32) for w in widths],
        compiler_params=pltpu.CompilerParams(dimension_semantics=("parallel",),
                                             vmem_limit_bytes=V7X_VMEM_LIMIT_BYTES),
        name="inproj",
    )(h, norm_g, w_in, cos, sin)


def _ret_kernel(q_ref, k_ref, v_ref, sg_ref, s0_ref, gn_ref, o_ref, s_out_ref, s_scr, *, rows, valid):
    c = pl.program_id(1)

    @pl.when(c == 0)
    def _():
        s_scr[...] = s0_ref[0]

    q = q_ref[0]
    k = k_ref[0]
    v = v_ref[0]
    sg = sg_ref[0]
    gn = gn_ref[...]
    ii = lax.broadcasted_iota(jnp.int32, (rows, rows), 0)
    jj = lax.broadcasted_iota(jnp.int32, (rows, rows), 1)
    diff = (ii - jj).astype(f32)
    row = lax.broadcasted_iota(jnp.int32, (rows, 1), 0).astype(f32)
    for h in range(RET_HEADS):
        lg = LOG_GAMMA[h]
        mask = jnp.where(diff >= 0, jnp.exp(lg * jnp.maximum(diff, 0.0)), 0.0)
        qh = q[:, RET_DK * h:RET_DK * (h + 1)]
        kh = k[:, RET_DK * h:RET_DK * (h + 1)]
        vh = v[:, RET_DV * h:RET_DV * (h + 1)].astype(bf16)
        scores = _dot_nt(qh.astype(bf16), kh.astype(bf16)) * mask
        q_decay = jnp.exp(lg * (row + 1.0))
        k_decay = jnp.exp(lg * (valid - 1.0 - row))
        s_old = s_scr[h]
        o = _dot(scores.astype(bf16), vh) + _dot((qh * q_decay).astype(bf16), s_old.astype(bf16))
        s_scr[h] = float(np.exp(lg * valid)) * s_old + _dot_tn((kh * k_decay).astype(bf16), vh)
        mu = jnp.mean(o, axis=-1, keepdims=True)
        oc = o - mu
        var = jnp.mean(oc * oc, axis=-1, keepdims=True)
        sl = slice(RET_DV * h, RET_DV * (h + 1))
        o_ref[0, :, sl] = oc * lax.rsqrt(var + RET_GN_EPS) * gn[:, sl] * sg[:, sl]

    @pl.when(c == pl.num_programs(1) - 1)
    def _():
        s_out_ref[0] = s_scr[...]


def _retention(q, k, v, sg, s0, gn, *, rows, valid):
    B, T, _ = q.shape
    bcast = s0.shape[0] == 1

    def seq(width):
        return pl.BlockSpec((1, rows, width), lambda b, c: (b, c, 0))

    state = pl.BlockSpec((1, RET_HEADS, RET_DK, RET_DV), lambda b, c: (b, 0, 0, 0))
    state_in = pl.BlockSpec((1, RET_HEADS, RET_DK, RET_DV), lambda b, c: (0, 0, 0, 0)) if bcast else state
    return pl.pallas_call(
        functools.partial(_ret_kernel, rows=rows, valid=valid),
        grid=(B, T // rows),
        in_specs=[seq(RET_QK), seq(RET_QK), seq(RET_V), seq(RET_V), state_in, _resident((1, RET_V))],
        out_specs=[seq(RET_V), state],
        out_shape=[jax.ShapeDtypeStruct((B, T, RET_V), f32),
                   jax.ShapeDtypeStruct((B, RET_HEADS, RET_DK, RET_DV), f32)],
        scratch_shapes=[pltpu.VMEM((RET_HEADS, RET_DK, RET_DV), f32)],
        compiler_params=pltpu.CompilerParams(dimension_semantics=("parallel", "arbitrary"),
                                             vmem_limit_bytes=V7X_VMEM_LIMIT_BYTES),
        name="retention",
    )(q, k, v, sg, s0, gn)


def _split3(x):
    hi = x.astype(bf16)
    r1 = x - hi.astype(f32)
    mid = r1.astype(bf16)
    lo = (r1 - mid.astype(f32)).astype(bf16)
    return hi, mid, lo


def _rwkv_kernel(p_ref, prev0_ref, h0_ref, mu_ref, w0_ref, w2_ref, a0_ref, a2_ref, g2_ref, kk_ref, ka_ref,
                 rk_ref, lng_ref, lnb_ref, ones_ref, o_ref, h_out_ref, shift_ref, h_scr, prev_scr, y_scr,
                 *, nb, rows, valid):
    c = pl.program_id(1)
    C = rows
    N = RWKV_HD
    R = nb * C
    assert C & (C - 1) == 0
    log2c = C.bit_length() - 1

    @pl.when(c == 0)
    def _():
        h_scr[...] = jnp.broadcast_to(h0_ref[...], h_scr.shape)
        prev_scr[...] = jnp.broadcast_to(prev0_ref[...], prev_scr.shape)

    p = p_ref[...].reshape(R, SHIFT_W)
    rowid = lax.broadcasted_iota(jnp.int32, (R, 1), 0)
    step = rowid & (C - 1)
    p_prev = pltpu.roll(p, 1, 0)
    for j in range(nb):
        p_prev = jnp.where(rowid == j * C, prev_scr[j], p_prev)
    for j in range(nb):
        prev_scr[j] = p[j * C + valid - 1:j * C + valid, :]
    pm = p + (p_prev - p) * mu_ref[...]
    r = pm[:, 0:RWKV_W]
    k = pm[:, RWKV_W:2 * RWKV_W]
    v = pm[:, 2 * RWKV_W:3 * RWKV_W]
    o_w = 3 * RWKV_W
    xw = pm[:, o_w:o_w + DECAY_LORA]
    xa = pm[:, o_w + DECAY_LORA:o_w + DECAY_LORA + AAA_LORA]
    xg = pm[:, o_w + DECAY_LORA + AAA_LORA:SHIFT_W]

    z = w0_ref[...] + _dot(jnp.tanh(xw).astype(bf16), w2_ref[...])
    nz = -z
    softplus = jnp.maximum(nz, 0.0) + jnp.log(1.0 + jnp.exp(-jnp.abs(nz)))
    ld = -jnp.exp(-softplus - 0.5)
    a = jax.nn.sigmoid(a0_ref[...] + _dot(xa.astype(bf16), a2_ref[...]))
    g = _dot(jax.nn.sigmoid(xg).astype(bf16), g2_ref[...])
    ones_bd = ones_ref[...]
    kk = k * kk_ref[...]
    kk = kk * lax.rsqrt(jnp.maximum(_dot((kk * kk).astype(bf16), ones_bd), 1e-24))
    kp = k * (1.0 + (a - 1.0) * ka_ref[...])
    if valid < C:
        live = (step < valid).astype(f32)
        ld = ld * live
        kk = kk * live
        kp = kp * live
        v = v * live
    b = kk * a

    ri = lax.broadcasted_iota(jnp.int32, (R, R), 0)
    rj = lax.broadcasted_iota(jnp.int32, (R, R), 1)
    tri = (((ri >> log2c) == (rj >> log2c)) & (ri >= rj)).astype(bf16)
    ld_hi, ld_mid, ld_lo = _split3(ld)
    cum = _dot(tri, ld_hi) + _dot(tri, ld_mid) + _dot(tri, ld_lo)
    last_rows = [cum[j * C + C - 1:j * C + C, :] for j in range(nb)]
    cum_last = jnp.concatenate([jnp.broadcast_to(x, (C, RWKV_W)) for x in last_rows], axis=0)
    e_in = jnp.exp(cum)
    e_ex = jnp.exp(cum - ld)
    e_neg = jnp.exp(-cum)
    e_end = jnp.exp(cum_last - cum)
    g_end = [jnp.exp(x) for x in last_rows]
    at = -kk * e_ex
    rt = r * e_in
    bt = (b * e_neg).astype(bf16)
    kt = (kp * e_neg).astype(bf16)
    bh = (b * e_end).astype(bf16)
    kh = (kp * e_end).astype(bf16)
    vb = v.astype(bf16)

    ti = lax.broadcasted_iota(jnp.int32, (C, C), 0)
    tj = lax.broadcasted_iota(jnp.int32, (C, C), 1)
    incl = ti >= tj
    strict = ti > tj
    ki = lax.broadcasted_iota(jnp.int32, (N, N), 0)
    kj = lax.broadcasted_iota(jnp.int32, (N, N), 1)
    eye_n = ki == kj
    x_lanes = lax.broadcasted_iota(jnp.int32, (C, 2 * N + C), 1) < 2 * N

    chains = [(j, h) for j in range(nb) for h in range(RWKV_HEADS)]
    n_ch = len(chains)

    def blk(x, j, h):
        return x[j * C:(j + 1) * C, N * h:N * (h + 1)]

    ar = [jnp.concatenate([blk(at, j, h), blk(rt, j, h)], axis=0).astype(bf16) for j, h in chains]
    gb = [_dot_nt(ar[i], blk(bt, j, h)) for i, (j, h) in enumerate(chains)]
    gk = [_dot_nt(ar[i], blk(kt, j, h)) for i, (j, h) in enumerate(chains)]
    lab = [jnp.where(strict, x[:C], 0.0) for x in gb]
    mrb = [jnp.where(incl, x[C:], 0.0).astype(bf16) for x in gb]
    lmk = [jnp.concatenate([jnp.where(strict, x[:C], 0.0), jnp.where(incl, x[C:], 0.0)], axis=0).astype(bf16)
           for x in gk]
    lmv = [_dot(lmk[i], blk(vb, j, h)) for i, (j, h) in enumerate(chains)]
    kv = [_dot_tn(blk(kh, j, h), blk(vb, j, h)) for j, h in chains]
    rhs = [jnp.concatenate([blk(at, j, h), lmv[i][:C], lab[i]], axis=1) for i, (j, h) in enumerate(chains)]
    lhs = lab
    for level in range(log2c - 1):
        out = [_dot(lhs[i].astype(bf16), rhs[i].astype(bf16)) for i in range(n_ch)]
        rhs = [out[i] + jnp.where(x_lanes, rhs[i], 0.0) for i in range(n_ch)]
        lhs = [x[:, 2 * N:] for x in out]
    xs = [x[:, :2 * N] for x in rhs]
    wu = [(xs[i] + _dot(lhs[i].astype(bf16), xs[i].astype(bf16))).astype(bf16) for i in range(n_ch)]
    pq = [_dot(mrb[i], wu[i]) for i in range(n_ch)]
    gd = [_dot_tn(blk(bh, j, h), wu[i]) for i, (j, h) in enumerate(chains)]
    for i, (j, h) in enumerate(chains):
        sl = slice(N * h, N * (h + 1))
        ry = blk(rt, j, h) + pq[i][:, :N]
        y0 = pq[i][:, N:] + lmv[i][C:]
        gmat = jnp.where(eye_n, g_end[j][:, sl], 0.0) + gd[i][:, :N]
        dmat = gd[i][:, N:] + kv[i]
        yh = _dot(jnp.concatenate([ry, gmat], axis=0).astype(bf16), h_scr[j, h].astype(bf16))
        y_scr[j * C:(j + 1) * C, sl] = yh[:C] + y0
        h_scr[j, h] = yh[C:] + dmat

    y = y_scr[...]
    inv_n = 1.0 / N
    mean = _dot(y.astype(bf16), ones_bd) * inv_n
    yc = y - mean
    var = _dot((yc * yc).astype(bf16), ones_bd) * inv_n
    out = yc * lax.rsqrt(var + RWKV_GN_EPS) * lng_ref[...] + lnb_ref[...]
    bonus = _dot((r * kp * rk_ref[...]).astype(bf16), ones_bd) * v
    o_ref[...] = ((out + bonus) * g).reshape(nb, C, RWKV_W)

    @pl.when(c == pl.num_programs(1) - 1)
    def _():
        h_out_ref[...] = h_scr[...]
        shift_ref[...] = prev_scr[...]


def _rwkv(p, prev0, h0, params, ones_bd, *, nb, rows, valid):
    B, T, _ = p.shape

    def maybe_bcast(arr, tail):
        nd = len(tail) + 1
        if arr.shape[0] == 1:
            return pl.BlockSpec((1,) + tail, lambda b, c: (0,) * nd)
        return pl.BlockSpec((nb,) + tail, lambda b, c: (b,) + (0,) * (nd - 1))

    seq_in = pl.BlockSpec((nb, rows, SHIFT_W), lambda b, c: (b, c, 0))
    seq_out = pl.BlockSpec((nb, rows, RWKV_W), lambda b, c: (b, c, 0))
    st_tail = (RWKV_HEADS, RWKV_HD, RWKV_HD)
    sh_tail = (1, SHIFT_W)
    param_specs = [_resident(x.shape) for x in params]
    return pl.pallas_call(
        functools.partial(_rwkv_kernel, nb=nb, rows=rows, valid=valid),
        grid=(B // nb, T // rows),
        in_specs=[seq_in, maybe_bcast(prev0, sh_tail), maybe_bcast(h0, st_tail)] + param_specs
                 + [_resident(ones_bd.shape)],
        out_specs=[seq_out, pl.BlockSpec((nb,) + st_tail, lambda b, c: (b, 0, 0, 0)),
                   pl.BlockSpec((nb,) + sh_tail, lambda b, c: (b, 0, 0))],
        out_shape=[jax.ShapeDtypeStruct((B, T, RWKV_W), f32),
                   jax.ShapeDtypeStruct((B,) + st_tail, f32),
                   jax.ShapeDtypeStruct((B,) + sh_tail, f32)],
        scratch_shapes=[pltpu.VMEM((nb,) + st_tail, f32), pltpu.VMEM((nb,) + sh_tail, f32),
                        pltpu.VMEM((nb * rows, RWKV_W), f32)],
        compiler_params=pltpu.CompilerParams(dimension_semantics=("parallel", "arbitrary"),
                                             vmem_limit_bytes=V7X_VMEM_LIMIT_BYTES),
        name="rwkv7",
    )(p, prev0, h0, *params, ones_bd)


def _merge_kernel(h_ref, oret_ref, orwkv_ref, gate_ref, wr_ref, ww_ref, wo_ref, o_ref):
    a = _dot(oret_ref[...].astype(bf16), wr_ref[...])
    b = _dot(orwkv_ref[...].astype(bf16), ww_ref[...])
    gate = gate_ref[...]
    merged = gate[:, :D_MODEL] * a + gate[:, D_MODEL:] * b
    o_ref[...] = h_ref[...] + _dot(merged.astype(bf16), wo_ref[...])


def _merge(h, o_ret, o_rwkv, gates, w_out_ret, w_out_rwkv, w_out, *, tm):
    rows = h.shape[0]

    def rowspec(width):
        return pl.BlockSpec((tm, width), lambda i: (i, 0))

    return pl.pallas_call(
        _merge_kernel,
        grid=(rows // tm,),
        in_specs=[rowspec(D_MODEL), rowspec(RET_V), rowspec(RWKV_W), rowspec(GATE_W),
                  _resident((RET_V, D_MODEL)), _resident((RWKV_W, D_MODEL)), _resident((D_MODEL, D_MODEL))],
        out_specs=rowspec(D_MODEL),
        out_shape=jax.ShapeDtypeStruct((rows, D_MODEL), f32),
        compiler_params=pltpu.CompilerParams(dimension_semantics=("parallel",),
                                             vmem_limit_bytes=V7X_VMEM_LIMIT_BYTES),
        name="merge",
    )(h, o_ret, o_rwkv, gates, w_out_ret, w_out_rwkv, w_out)


def _rotary_tables(pos):
    half = RET_DK // 2
    inv_freq = ROPE_BASE ** (-jnp.arange(half, dtype=f32) / half)
    ang = pos.astype(f32)[:, None] * inv_freq[None, :]
    cos = jnp.cos(ang)
    sin = jnp.sin(ang)
    cos_t = jnp.tile(jnp.concatenate([cos, cos], axis=1), (1, RET_HEADS))
    sin_t = jnp.tile(jnp.concatenate([-sin, sin], axis=1), (1, RET_HEADS))
    return cos_t, sin_t


def _row_tile(rows, target):
    tm = min(rows, target)
    while rows % tm:
        tm -= 8
    return tm


def kernel(x_prompt, x_sample, state_ret, state_wkv, state_shift, meta_tokens, ffn1_norm, ffn1_w_gate, ffn1_w_up, ffn1_w_down, mix_norm, w_in, ret_gn_g, mu_shift, w0, w2, a0, a2, g2, k_k, k_a, r_k, lnx_g, lnx_b, w_out_ret, w_out_rwkv, w_out, ffn2_norm, ffn2_w_gate, ffn2_w_up, ffn2_w_down, final_norm):
    Bp, Tp, _ = x_prompt.shape
    Bs, Ts, _ = x_sample.shape
    st_dtype = state_ret.dtype

    def row(x):
        return x.reshape(1, -1).astype(f32)

    ffn1 = (row(ffn1_norm), ffn1_w_gate.astype(bf16), ffn1_w_up.astype(bf16), ffn1_w_down.astype(bf16))
    ffn2 = (row(ffn2_norm), ffn2_w_gate.astype(bf16), ffn2_w_up.astype(bf16), ffn2_w_down.astype(bf16))
    fin = row(final_norm)
    w_in_b = w_in.astype(bf16)
    wr_b, ww_b, wo_b = w_out_ret.astype(bf16), w_out_rwkv.astype(bf16), w_out.astype(bf16)
    rwkv_params = (row(mu_shift), row(w0), w2.astype(bf16), row(a0), a2.astype(bf16), g2.astype(bf16),
                   row(k_k), row(k_a), row(r_k), row(lnx_g), row(lnx_b))
    head_id = jnp.arange(RWKV_W, dtype=jnp.int32) // RWKV_HD
    ones_bd = (head_id[:, None] == head_id[None, :]).astype(bf16)
    gn = row(ret_gn_g)

    def pre(x, cos, sin, tm_ffn, tm_proj):
        h = _ffn(x, *ffn1, fin, tm=tm_ffn, final_norm=False)
        return (h,) + tuple(_inproj(h, row(mix_norm), w_in_b, cos, sin, tm=tm_proj))

    def post(h, o_ret, o_rwkv, gates, tm):
        h2 = _merge(h, o_ret, o_rwkv, gates, wr_b, ww_b, wo_b, tm=tm)
        return _ffn(h2, *ffn2, fin, tm=tm, final_norm=True)

    n_s = Bs * Ts
    x_small = jnp.concatenate([x_sample.reshape(n_s, D_MODEL), meta_tokens.astype(x_sample.dtype)], axis=0)
    cos_s, sin_s = _rotary_tables(PAST_LEN + jnp.arange(Ts, dtype=jnp.int32))
    cos_m, sin_m = _rotary_tables(jnp.arange(N_META, dtype=jnp.int32))
    cos_small = jnp.concatenate([jnp.tile(cos_s, (Bs, 1)), cos_m], axis=0)
    sin_small = jnp.concatenate([jnp.tile(sin_s, (Bs, 1)), sin_m], axis=0)
    n_small = n_s + N_META
    h_small, q_s, k_s, v_s, sg_s, p_s, gate_s = pre(x_small, cos_small, sin_small, n_small, n_small)

    def meta(x):
        return x[n_s:].reshape(1, N_META, -1)

    zeros_ret = jnp.zeros((1, RET_HEADS, RET_DK, RET_DV), f32)
    _, s_ret_m = _retention(meta(q_s), meta(k_s), meta(v_s), meta(sg_s), zeros_ret, gn, rows=N_META, valid=N_META)
    _, h_wkv_m, shift_m = _rwkv(meta(p_s), jnp.zeros((1, 1, SHIFT_W), f32),
                                jnp.zeros((1, RWKV_HEADS, RWKV_HD, RWKV_HD), f32), rwkv_params, ones_bd,
                                nb=1, rows=N_META, valid=N_META)

    pad_t = -(-Ts // 8) * 8

    def samp(x):
        x = x[:n_s].reshape(Bs, Ts, -1)
        return jnp.pad(x, ((0, 0), (0, pad_t - Ts), (0, 0)))

    o_ret_s, s_ret_s = _retention(samp(q_s), samp(k_s), samp(v_s), samp(sg_s), state_ret.astype(f32), gn,
                                  rows=pad_t, valid=Ts)
    o_wkv_s, h_wkv_s, shift_s = _rwkv(samp(p_s), state_shift.astype(f32).reshape(Bs, 1, SHIFT_W),
                                      jnp.swapaxes(state_wkv.astype(f32), -1, -2), rwkv_params, ones_bd,
                                      nb=RWKV_SAMPLE_SEQS, rows=pad_t, valid=Ts)
    y_sample = post(h_small[:n_s], o_ret_s[:, :Ts].reshape(n_s, RET_V), o_wkv_s[:, :Ts].reshape(n_s, RWKV_W),
                    gate_s[:n_s], _row_tile(n_s, 512))

    n_p = Bp * Tp
    cos_p, sin_p = _rotary_tables(N_META + jnp.arange(Tp, dtype=jnp.int32))
    h_p, q_p, k_p, v_p, sg_p, p_p, gate_p = pre(x_prompt.reshape(n_p, D_MODEL), cos_p, sin_p,
                                                _row_tile(n_p, 512), _row_tile(Tp, 256))

    def seqs(x):
        return x.reshape(Bp, Tp, -1)

    o_ret_p, s_ret_p = _retention(seqs(q_p), seqs(k_p), seqs(v_p), seqs(sg_p), s_ret_m, gn,
                                  rows=RET_CHUNK, valid=RET_CHUNK)
    o_wkv_p, h_wkv_p, shift_p = _rwkv(seqs(p_p), shift_m, h_wkv_m, rwkv_params, ones_bd,
                                      nb=RWKV_PROMPT_SEQS, rows=RWKV_CHUNK, valid=RWKV_CHUNK)
    y_prompt = post(h_p, o_ret_p.reshape(n_p, RET_V), o_wkv_p.reshape(n_p, RWKV_W), gate_p, _row_tile(n_p, 512))

    return (y_prompt.reshape(Bp, Tp, D_MODEL).astype(x_prompt.dtype),
            y_sample.reshape(Bs, Ts, D_MODEL).astype(x_sample.dtype),
            s_ret_p.astype(st_dtype), jnp.swapaxes(h_wkv_p, -1, -2).astype(st_dtype),
            shift_p.reshape(Bp, SHIFT_W).astype(st_dtype),
            s_ret_s.astype(st_dtype), jnp.swapaxes(h_wkv_s, -1, -2).astype(st_dtype),
            shift_s.reshape(Bs, SHIFT_W).astype(st_dtype))
```

```python
import functools

import numpy as np
import jax
import jax.numpy as jnp
from jax import lax
from jax.experimental import pallas as pl
from jax.experimental.pallas import tpu as pltpu

D_MODEL = 1024
N_META = 16
PAST_LEN = 16384
RET_HEADS = 4
RET_DK = 64
RET_DV = 128
RET_CHUNK = 128
RWKV_HEADS = 8
RWKV_HD = 64
RWKV_W = RWKV_HEADS * RWKV_HD
DECAY_LORA = 64
AAA_LORA = 64
GATE_LORA = 128
D_FF = 2816
ROPE_BASE = 10000.0
NORM_EPS = 1e-6
RET_GN_EPS = 1e-6
RWKV_GN_EPS = 64e-5
RET_QK = RET_HEADS * RET_DK
RET_V = RET_HEADS * RET_DV
SHIFT_W = 3 * RWKV_W + DECAY_LORA + AAA_LORA + GATE_LORA
GATE_W = 2 * D_MODEL
PROJ_W = 2 * RET_QK + 2 * RET_V + SHIFT_W + GATE_W

_C_Q, _C_K, _C_V, _C_G = 0, RET_QK, 2 * RET_QK, 2 * RET_QK + RET_V
_C_P = 2 * RET_QK + 2 * RET_V
_C_GATE = _C_P + SHIFT_W

V7X_VMEM_LIMIT_BYTES = 56 * 1024 * 1024
FF_CHUNK = 256
RWKV_CHUNK = 64
RWKV_PROMPT_SEQS = 4
RWKV_SAMPLE_SEQS = 8
RET_PROMPT_SEQS = 8
RET_SAMPLE_SEQS = 32
LOG_GAMMA = tuple(float(np.log1p(-2.0 ** (-5.0 - h))) for h in range(RET_HEADS))

f32 = jnp.float32
bf16 = jnp.bfloat16


def _resident(shape):
    zeros = (0,) * len(shape)
    return pl.BlockSpec(shape, lambda *_: zeros, pipeline_mode=pl.Buffered(1))


def _rms(x, g):
    return x * lax.rsqrt(jnp.mean(x * x, axis=-1, keepdims=True) + NORM_EPS) * g


def _dot(a, b):
    return jnp.dot(a, b, preferred_element_type=f32)


def _dot_nt(a, b):
    return lax.dot_general(a, b, (((1,), (1,)), ((), ())), preferred_element_type=f32)


def _dot_tn(a, b):
    return lax.dot_general(a, b, (((0,), (0,)), ((), ())), preferred_element_type=f32)


def _ffn_kernel(x_ref, g_ref, wg_ref, wu_ref, wd_ref, fin_ref, o_ref, *, final_norm):
    x = x_ref[...]
    xn = _rms(x, g_ref[...]).astype(bf16)
    acc = jnp.zeros(x.shape, f32)
    for c in range(D_FF // FF_CHUNK):
        sl = slice(c * FF_CHUNK, (c + 1) * FF_CHUNK)
        gt = _dot(xn, wg_ref[:, sl])
        up = _dot(xn, wu_ref[:, sl])
        act = (gt * jax.nn.sigmoid(gt) * up).astype(bf16)
        acc = acc + _dot(act, wd_ref[sl, :])
    h = x + 0.5 * acc
    if final_norm:
        h = _rms(h, fin_ref[...])
    o_ref[...] = h


def _ffn(x, norm_g, wg, wu, wd, fin_g, *, tm, final_norm):
    rows = x.shape[0]
    row = pl.BlockSpec((tm, D_MODEL), lambda i: (i, 0))
    return pl.pallas_call(
        functools.partial(_ffn_kernel, final_norm=final_norm),
        grid=(rows // tm,),
        in_specs=[row, _resident((1, D_MODEL)), _resident((D_MODEL, D_FF)), _resident((D_MODEL, D_FF)),
                  _resident((D_FF, D_MODEL)), _resident((1, D_MODEL))],
        out_specs=row,
        out_shape=jax.ShapeDtypeStruct((rows, D_MODEL), f32),
        compiler_params=pltpu.CompilerParams(dimension_semantics=("parallel",),
                                             vmem_limit_bytes=V7X_VMEM_LIMIT_BYTES),
        name="ffn_final" if final_norm else "ffn",
    )(x, norm_g, wg, wu, wd, fin_g)


def _swap_halves(x):
    parts = []
    for j in range(x.shape[1] // 128):
        xs = x[:, 128 * j:128 * (j + 1)]
        fwd = pltpu.roll(xs, 32, 1)
        bwd = pltpu.roll(xs, 96, 1)
        lane = lax.broadcasted_iota(jnp.int32, xs.shape, 1)
        parts.append(jnp.where((lane % RET_DK) < RET_DK // 2, bwd, fwd))
    return jnp.concatenate(parts, axis=1)


def _inproj_kernel(h_ref, g_ref, w_ref, cos_ref, sin_ref, q_ref, k_ref, v_ref, sg_ref, p_ref, gate_ref):
    un = _rms(h_ref[...], g_ref[...]).astype(bf16)
    cos = cos_ref[...]
    sin = sin_ref[...]
    q = _dot(un, w_ref[:, _C_Q:_C_K])
    q_ref[...] = q * cos + _swap_halves(q) * sin
    k = _dot(un, w_ref[:, _C_K:_C_V])
    k_ref[...] = (k * cos + _swap_halves(k) * sin) * (RET_DK ** -0.5)
    v_ref[...] = _dot(un, w_ref[:, _C_V:_C_G]).astype(v_ref.dtype)
    gr = _dot(un, w_ref[:, _C_G:_C_P])
    sg_ref[...] = gr * jax.nn.sigmoid(gr)
    p_ref[...] = _dot(un, w_ref[:, _C_P:_C_GATE])
    gate_ref[...] = jax.nn.sigmoid(_dot(un, w_ref[:, _C_GATE:PROJ_W]))


def _inproj(h, norm_g, w_in, cos, sin, *, tm):
    rows = h.shape[0]
    tab_blocks = cos.shape[0] // tm

    def rowspec(width):
        return pl.BlockSpec((tm, width), lambda i: (i, 0))

    tab = pl.BlockSpec((tm, RET_QK), lambda i: (i % tab_blocks, 0))
    widths = (RET_QK, RET_QK, RET_V, RET_V, SHIFT_W, GATE_W)
    dtypes = (f32, f32, bf16, f32, f32, f32)
    return pl.pallas_call(
        _inproj_kernel,
        grid=(rows // tm,),
        in_specs=[rowspec(D_MODEL), _resident((1, D_MODEL)), _resident((D_MODEL, PROJ_W)), tab, tab],
        out_specs=[rowspec(w) for w in widths],
        out_shape=[jax.ShapeDtypeStruct((rows, w), d) for w, d in zip(widths, dtypes)],
        compiler_params=pltpu.CompilerParams(dimension_semantics=("parallel",),
                                             vmem_limit_bytes=V7X_VMEM_LIMIT_BYTES),
        name="inproj",
    )(h, norm_g, w_in, cos, sin)


def _ret_kernel(q_ref, k_ref, v_ref, sg_ref, s0_ref, gn_ref, o_ref, s_out_ref, s_scr, o_scr, *, nb, rows, valid):
    c = pl.program_id(1)
    L = rows

    @pl.when(c == 0)
    def _():
        s_scr[...] = jnp.broadcast_to(s0_ref[...], s_scr.shape)

    ii = lax.broadcasted_iota(jnp.int32, (L, L), 0)
    jj = lax.broadcasted_iota(jnp.int32, (L, L), 1)
    diff = (ii - jj).astype(f32)
    row = lax.broadcasted_iota(jnp.int32, (L, 1), 0).astype(f32)
    mask = [jnp.where(diff >= 0, jnp.exp(lg * jnp.maximum(diff, 0.0)), 0.0) for lg in LOG_GAMMA]
    q_decay = [jnp.exp(lg * (row + 1.0)) for lg in LOG_GAMMA]
    k_decay = [jnp.exp(lg * (valid - 1.0 - row)) for lg in LOG_GAMMA]
    s_decay = [float(np.exp(lg * valid)) for lg in LOG_GAMMA]

    chains = [(j, h) for j in range(nb) for h in range(RET_HEADS)]
    qh = [q_ref[j, :, RET_DK * h:RET_DK * (h + 1)] for j, h in chains]
    kh = [k_ref[j, :, RET_DK * h:RET_DK * (h + 1)] for j, h in chains]
    vh = [v_ref[j, :, RET_DV * h:RET_DV * (h + 1)] for j, h in chains]
    scores = [(_dot_nt(qh[i].astype(bf16), kh[i].astype(bf16)) * mask[h]).astype(bf16)
              for i, (j, h) in enumerate(chains)]
    qd = [(qh[i] * q_decay[h]).astype(bf16) for i, (j, h) in enumerate(chains)]
    kd = [(kh[i] * k_decay[h]).astype(bf16) for i, (j, h) in enumerate(chains)]
    s_old = [s_scr[j, h] for j, h in chains]
    for i, (j, h) in enumerate(chains):
        o_scr[j * L:(j + 1) * L, RET_DV * h:RET_DV * (h + 1)] = (
            _dot(scores[i], vh[i]) + _dot(qd[i], s_old[i].astype(bf16)))
    for i, (j, h) in enumerate(chains):
        s_scr[j, h] = s_decay[h] * s_old[i] + _dot_tn(kd[i], vh[i])

    gn = gn_ref[...]
    sg = sg_ref[...].reshape(nb * L, RET_V)
    for h in range(RET_HEADS):
        sl = slice(RET_DV * h, RET_DV * (h + 1))
        o = o_scr[:, sl]
        mu = jnp.mean(o, axis=-1, keepdims=True)
        oc = o - mu
        var = jnp.mean(oc * oc, axis=-1, keepdims=True)
        out = oc * lax.rsqrt(var + RET_GN_EPS) * gn[:, sl] * sg[:, sl]
        o_ref[:, :, sl] = out.reshape(nb, L, RET_DV).astype(o_ref.dtype)

    @pl.when(c == pl.num_programs(1) - 1)
    def _():
        s_out_ref[...] = s_scr[...]


def _retention(q, k, v, sg, s0, gn, *, nb, rows, valid):
    B, T, _ = q.shape
    bcast = s0.shape[0] == 1

    def seq(width):
        return pl.BlockSpec((nb, rows, width), lambda b, c: (b, c, 0))

    state = pl.BlockSpec((nb, RET_HEADS, RET_DK, RET_DV), lambda b, c: (b, 0, 0, 0))
    state_in = pl.BlockSpec((1, RET_HEADS, RET_DK, RET_DV), lambda b, c: (0, 0, 0, 0)) if bcast else state
    return pl.pallas_call(
        functools.partial(_ret_kernel, nb=nb, rows=rows, valid=valid),
        grid=(B // nb, T // rows),
        in_specs=[seq(RET_QK), seq(RET_QK), seq(RET_V), seq(RET_V), state_in, _resident((1, RET_V))],
        out_specs=[seq(RET_V), state],
        out_shape=[jax.ShapeDtypeStruct((B, T, RET_V), bf16),
                   jax.ShapeDtypeStruct((B, RET_HEADS, RET_DK, RET_DV), f32)],
        scratch_shapes=[pltpu.VMEM((nb, RET_HEADS, RET_DK, RET_DV), f32), pltpu.VMEM((nb * rows, RET_V), f32)],
        compiler_params=pltpu.CompilerParams(dimension_semantics=("parallel", "arbitrary"),
                                             vmem_limit_bytes=V7X_VMEM_LIMIT_BYTES),
        name="retention",
    )(q, k, v, sg, s0, gn)


def _split3(x):
    hi = x.astype(bf16)
    r1 = x - hi.astype(f32)
    mid = r1.astype(bf16)
    lo = (r1 - mid.astype(f32)).astype(bf16)
    return hi, mid, lo


def _rwkv_kernel(p_ref, prev0_ref, h0_ref, mu_ref, w0_ref, w2_ref, a0_ref, a2_ref, g2_ref, kk_ref, ka_ref,
                 rk_ref, lng_ref, lnb_ref, ones_ref, o_ref, h_out_ref, shift_ref, h_scr, prev_scr, y_scr,
                 *, nb, rows, valid):
    c = pl.program_id(1)
    C = rows
    N = RWKV_HD
    R = nb * C
    assert C & (C - 1) == 0
    log2c = C.bit_length() - 1

    @pl.when(c == 0)
    def _():
        h_scr[...] = jnp.broadcast_to(h0_ref[...], h_scr.shape)
        prev_scr[...] = jnp.broadcast_to(prev0_ref[...], prev_scr.shape)

    p = p_ref[...].reshape(R, SHIFT_W)
    rowid = lax.broadcasted_iota(jnp.int32, (R, 1), 0)
    step = rowid & (C - 1)
    p_prev = pltpu.roll(p, 1, 0)
    for j in range(nb):
        p_prev = jnp.where(rowid == j * C, prev_scr[j], p_prev)
    for j in range(nb):
        prev_scr[j] = p[j * C + valid - 1:j * C + valid, :]
    pm = p + (p_prev - p) * mu_ref[...]
    r = pm[:, 0:RWKV_W]
    k = pm[:, RWKV_W:2 * RWKV_W]
    v = pm[:, 2 * RWKV_W:3 * RWKV_W]
    o_w = 3 * RWKV_W
    xw = pm[:, o_w:o_w + DECAY_LORA]
    xa = pm[:, o_w + DECAY_LORA:o_w + DECAY_LORA + AAA_LORA]
    xg = pm[:, o_w + DECAY_LORA + AAA_LORA:SHIFT_W]

    z = w0_ref[...] + _dot(jnp.tanh(xw).astype(bf16), w2_ref[...])
    nz = -z
    softplus = jnp.maximum(nz, 0.0) + jnp.log(1.0 + jnp.exp(-jnp.abs(nz)))
    ld = -jnp.exp(-softplus - 0.5)
    a = jax.nn.sigmoid(a0_ref[...] + _dot(xa.astype(bf16), a2_ref[...]))
    g = _dot(jax.nn.sigmoid(xg).astype(bf16), g2_ref[...])
    ones_bd = ones_ref[...]
    kk = k * kk_ref[...]
    kk = kk * lax.rsqrt(jnp.maximum(_dot((kk * kk).astype(bf16), ones_bd), 1e-24))
    kp = k * (1.0 + (a - 1.0) * ka_ref[...])
    if valid < C:
        live = (step < valid).astype(f32)
        ld = ld * live
        kk = kk * live
        kp = kp * live
        v = v * live
    b = kk * a

    ri = lax.broadcasted_iota(jnp.int32, (R, R), 0)
    rj = lax.broadcasted_iota(jnp.int32, (R, R), 1)
    tri = (((ri >> log2c) == (rj >> log2c)) & (ri >= rj)).astype(bf16)
    ld_hi, ld_mid, ld_lo = _split3(ld)
    cum = _dot(tri, ld_hi) + _dot(tri, ld_mid) + _dot(tri, ld_lo)
    last_rows = [cum[j * C + C - 1:j * C + C, :] for j in range(nb)]
    cum_last = jnp.concatenate([jnp.broadcast_to(x, (C, RWKV_W)) for x in last_rows], axis=0)
    e_in = jnp.exp(cum)
    e_ex = jnp.exp(cum - ld)
    e_neg = jnp.exp(-cum)
    e_end = jnp.exp(cum_last - cum)
    g_end = [jnp.exp(x) for x in last_rows]
    at = -kk * e_ex
    rt = r * e_in
    bt = (b * e_neg).astype(bf16)
    kt = (kp * e_neg).astype(bf16)
    bh = (b * e_end).astype(bf16)
    kh = (kp * e_end).astype(bf16)
    vb = v.astype(bf16)

    ti = lax.broadcasted_iota(jnp.int32, (C, C), 0)
    tj = lax.broadcasted_iota(jnp.int32, (C, C), 1)
    incl = ti >= tj
    strict = ti > tj
    ki = lax.broadcasted_iota(jnp.int32, (N, N), 0)
    kj = lax.broadcasted_iota(jnp.int32, (N, N), 1)
    eye_n = ki == kj
    x_lanes = lax.broadcasted_iota(jnp.int32, (C, 2 * N + C), 1) < 2 * N

    chains = [(j, h) for j in range(nb) for h in range(RWKV_HEADS)]
    n_ch = len(chains)

    def blk(x, j, h):
        return x[j * C:(j + 1) * C, N * h:N * (h + 1)]

    ar = [jnp.concatenate([blk(at, j, h), blk(rt, j, h)], axis=0).astype(bf16) for j, h in chains]
    gb = [_dot_nt(ar[i], blk(bt, j, h)) for i, (j, h) in enumerate(chains)]
    gk = [_dot_nt(ar[i], blk(kt, j, h)) for i, (j, h) in enumerate(chains)]
    lab = [jnp.where(strict, x[:C], 0.0) for x in gb]
    mrb = [jnp.where(incl, x[C:], 0.0).astype(bf16) for x in gb]
    lmk = [jnp.concatenate([jnp.where(strict, x[:C], 0.0), jnp.where(incl, x[C:], 0.0)], axis=0).astype(bf16)
           for x in gk]
    lmv = [_dot(lmk[i], blk(vb, j, h)) for i, (j, h) in enumerate(chains)]
    rhs = [jnp.concatenate([blk(at, j, h), lmv[i][:C], lab[i]], axis=1) for i, (j, h) in enumerate(chains)]
    lhs = lab
    for level in range(log2c - 1):
        out = [_dot(lhs[i].astype(bf16), rhs[i].astype(bf16)) for i in range(n_ch)]
        rhs = [out[i] + jnp.where(x_lanes, rhs[i], 0.0) for i in range(n_ch)]
        lhs = [x[:, 2 * N:] for x in out]
    xs = [x[:, :2 * N] for x in rhs]
    wu = [(xs[i] + _dot(lhs[i].astype(bf16), xs[i].astype(bf16))).astype(bf16) for i in range(n_ch)]
    pq = [_dot(mrb[i], wu[i]) for i in range(n_ch)]
    gw = [_dot_tn(blk(bh, j, h), wu[i][:, :N]) for i, (j, h) in enumerate(chains)]
    dk = [_dot_tn(jnp.concatenate([wu[i][:, N:], blk(vb, j, h)], axis=0),
                  jnp.concatenate([blk(bh, j, h), blk(kh, j, h)], axis=0))
          for i, (j, h) in enumerate(chains)]
    for i, (j, h) in enumerate(chains):
        sl = slice(N * h, N * (h + 1))
        ry = blk(rt, j, h) + pq[i][:, :N]
        y0 = pq[i][:, N:] + lmv[i][C:]
        gmat = jnp.where(eye_n, g_end[j][:, sl], 0.0) + gw[i]
        s_old = h_scr[j, h].astype(bf16)
        y_scr[j * C:(j + 1) * C, sl] = _dot_nt(ry.astype(bf16), s_old) + y0
        h_scr[j, h] = _dot_nt(s_old, gmat.astype(bf16)) + dk[i]

    y = y_scr[...]
    inv_n = 1.0 / N
    mean = _dot(y.astype(bf16), ones_bd) * inv_n
    yc = y - mean
    var = _dot((yc * yc).astype(bf16), ones_bd) * inv_n
    out = yc * lax.rsqrt(var + RWKV_GN_EPS) * lng_ref[...] + lnb_ref[...]
    bonus = _dot((r * kp * rk_ref[...]).astype(bf16), ones_bd) * v
    o_ref[...] = ((out + bonus) * g).reshape(nb, C, RWKV_W).astype(o_ref.dtype)

    @pl.when(c == pl.num_programs(1) - 1)
    def _():
        h_out_ref[...] = h_scr[...]
        shift_ref[...] = prev_scr[...]


def _rwkv(p, prev0, h0, params, ones_bd, *, nb, rows, valid):
    B, T, _ = p.shape

    def maybe_bcast(arr, tail):
        nd = len(tail) + 1
        if arr.shape[0] == 1:
            return pl.BlockSpec((1,) + tail, lambda b, c: (0,) * nd)
        return pl.BlockSpec((nb,) + tail, lambda b, c: (b,) + (0,) * (nd - 1))

    seq_in = pl.BlockSpec((nb, rows, SHIFT_W), lambda b, c: (b, c, 0))
    seq_out = pl.BlockSpec((nb, rows, RWKV_W), lambda b, c: (b, c, 0))
    st_tail = (RWKV_HEADS, RWKV_HD, RWKV_HD)
    sh_tail = (1, SHIFT_W)
    param_specs = [_resident(x.shape) for x in params]
    return pl.pallas_call(
        functools.partial(_rwkv_kernel, nb=nb, rows=rows, valid=valid),
        grid=(B // nb, T // rows),
        in_specs=[seq_in, maybe_bcast(prev0, sh_tail), maybe_bcast(h0, st_tail)] + param_specs
                 + [_resident(ones_bd.shape)],
        out_specs=[seq_out, pl.BlockSpec((nb,) + st_tail, lambda b, c: (b, 0, 0, 0)),
                   pl.BlockSpec((nb,) + sh_tail, lambda b, c: (b, 0, 0))],
        out_shape=[jax.ShapeDtypeStruct((B, T, RWKV_W), bf16),
                   jax.ShapeDtypeStruct((B,) + st_tail, f32),
                   jax.ShapeDtypeStruct((B,) + sh_tail, f32)],
        scratch_shapes=[pltpu.VMEM((nb,) + st_tail, f32), pltpu.VMEM((nb,) + sh_tail, f32),
                        pltpu.VMEM((nb * rows, RWKV_W), f32)],
        compiler_params=pltpu.CompilerParams(dimension_semantics=("parallel", "arbitrary"),
                                             vmem_limit_bytes=V7X_VMEM_LIMIT_BYTES),
        name="rwkv7",
    )(p, prev0, h0, *params, ones_bd)


def _merge_kernel(h_ref, oret_ref, orwkv_ref, gate_ref, wr_ref, ww_ref, wo_ref, o_ref):
    a = _dot(oret_ref[...].astype(bf16), wr_ref[...])
    b = _dot(orwkv_ref[...].astype(bf16), ww_ref[...])
    gate = gate_ref[...]
    merged = gate[:, :D_MODEL] * a + gate[:, D_MODEL:] * b
    o_ref[...] = h_ref[...] + _dot(merged.astype(bf16), wo_ref[...])


def _merge(h, o_ret, o_rwkv, gates, w_out_ret, w_out_rwkv, w_out, *, tm):
    rows = h.shape[0]

    def rowspec(width):
        return pl.BlockSpec((tm, width), lambda i: (i, 0))

    return pl.pallas_call(
        _merge_kernel,
        grid=(rows // tm,),
        in_specs=[rowspec(D_MODEL), rowspec(RET_V), rowspec(RWKV_W), rowspec(GATE_W),
                  _resident((RET_V, D_MODEL)), _resident((RWKV_W, D_MODEL)), _resident((D_MODEL, D_MODEL))],
        out_specs=rowspec(D_MODEL),
        out_shape=jax.ShapeDtypeStruct((rows, D_MODEL), f32),
        compiler_params=pltpu.CompilerParams(dimension_semantics=("parallel",),
                                             vmem_limit_bytes=V7X_VMEM_LIMIT_BYTES),
        name="merge",
    )(h, o_ret, o_rwkv, gates, w_out_ret, w_out_rwkv, w_out)


def _rotary_tables(pos):
    half = RET_DK // 2
    inv_freq = ROPE_BASE ** (-jnp.arange(half, dtype=f32) / half)
    ang = pos.astype(f32)[:, None] * inv_freq[None, :]
    cos = jnp.cos(ang)
    sin = jnp.sin(ang)
    cos_t = jnp.tile(jnp.concatenate([cos, cos], axis=1), (1, RET_HEADS))
    sin_t = jnp.tile(jnp.concatenate([-sin, sin], axis=1), (1, RET_HEADS))
    return cos_t, sin_t


def _row_tile(rows, target):
    tm = min(rows, target)
    while rows % tm:
        tm -= 8
    return tm


def kernel(x_prompt, x_sample, state_ret, state_wkv, state_shift, meta_tokens, ffn1_norm, ffn1_w_gate, ffn1_w_up, ffn1_w_down, mix_norm, w_in, ret_gn_g, mu_shift, w0, w2, a0, a2, g2, k_k, k_a, r_k, lnx_g, lnx_b, w_out_ret, w_out_rwkv, w_out, ffn2_norm, ffn2_w_gate, ffn2_w_up, ffn2_w_down, final_norm):
    Bp, Tp, _ = x_prompt.shape
    Bs, Ts, _ = x_sample.shape
    st_dtype = state_ret.dtype

    def row(x):
        return x.reshape(1, -1).astype(f32)

    ffn1 = (row(ffn1_norm), ffn1_w_gate.astype(bf16), ffn1_w_up.astype(bf16), ffn1_w_down.astype(bf16))
    ffn2 = (row(ffn2_norm), ffn2_w_gate.astype(bf16), ffn2_w_up.astype(bf16), ffn2_w_down.astype(bf16))
    fin = row(final_norm)
    w_in_b = w_in.astype(bf16)
    wr_b, ww_b, wo_b = w_out_ret.astype(bf16), w_out_rwkv.astype(bf16), w_out.astype(bf16)
    rwkv_params = (row(mu_shift), row(w0), w2.astype(bf16), row(a0), a2.astype(bf16), g2.astype(bf16),
                   row(k_k), row(k_a), row(r_k), row(lnx_g), row(lnx_b))
    head_id = jnp.arange(RWKV_W, dtype=jnp.int32) // RWKV_HD
    ones_bd = (head_id[:, None] == head_id[None, :]).astype(bf16)
    gn = row(ret_gn_g)

    def pre(x, cos, sin, tm_ffn, tm_proj):
        h = _ffn(x, *ffn1, fin, tm=tm_ffn, final_norm=False)
        return (h,) + tuple(_inproj(h, row(mix_norm), w_in_b, cos, sin, tm=tm_proj))

    def post(h, o_ret, o_rwkv, gates, tm):
        h2 = _merge(h, o_ret, o_rwkv, gates, wr_b, ww_b, wo_b, tm=tm)
        return _ffn(h2, *ffn2, fin, tm=tm, final_norm=True)

    n_s = Bs * Ts
    x_small = jnp.concatenate([x_sample.reshape(n_s, D_MODEL), meta_tokens.astype(x_sample.dtype)], axis=0)
    cos_s, sin_s = _rotary_tables(PAST_LEN + jnp.arange(Ts, dtype=jnp.int32))
    cos_m, sin_m = _rotary_tables(jnp.arange(N_META, dtype=jnp.int32))
    cos_small = jnp.concatenate([jnp.tile(cos_s, (Bs, 1)), cos_m], axis=0)
    sin_small = jnp.concatenate([jnp.tile(sin_s, (Bs, 1)), sin_m], axis=0)
    n_small = n_s + N_META
    h_small, q_s, k_s, v_s, sg_s, p_s, gate_s = pre(x_small, cos_small, sin_small, n_small, n_small)

    def meta(x):
        return x[n_s:].reshape(1, N_META, -1)

    zeros_ret = jnp.zeros((1, RET_HEADS, RET_DK, RET_DV), f32)
    _, s_ret_m = _retention(meta(q_s), meta(k_s), meta(v_s), meta(sg_s), zeros_ret, gn,
                            nb=1, rows=N_META, valid=N_META)
    _, h_wkv_m, shift_m = _rwkv(meta(p_s), jnp.zeros((1, 1, SHIFT_W), f32),
                                jnp.zeros((1, RWKV_HEADS, RWKV_HD, RWKV_HD), f32), rwkv_params, ones_bd,
                                nb=1, rows=N_META, valid=N_META)

    pad_t = -(-Ts // 8) * 8

    def samp(x):
        x = x[:n_s].reshape(Bs, Ts, -1)
        return jnp.pad(x, ((0, 0), (0, pad_t - Ts), (0, 0)))

    o_ret_s, s_ret_s = _retention(samp(q_s), samp(k_s), samp(v_s), samp(sg_s), state_ret.astype(f32), gn,
                                  nb=RET_SAMPLE_SEQS, rows=pad_t, valid=Ts)
    o_wkv_s, h_wkv_s, shift_s = _rwkv(samp(p_s), state_shift.astype(f32).reshape(Bs, 1, SHIFT_W),
                                      state_wkv.astype(f32), rwkv_params, ones_bd,
                                      nb=RWKV_SAMPLE_SEQS, rows=pad_t, valid=Ts)
    y_sample = post(h_small[:n_s], o_ret_s[:, :Ts].reshape(n_s, RET_V), o_wkv_s[:, :Ts].reshape(n_s, RWKV_W),
                    gate_s[:n_s], _row_tile(n_s, 512))

    n_p = Bp * Tp
    cos_p, sin_p = _rotary_tables(N_META + jnp.arange(Tp, dtype=jnp.int32))
    h_p, q_p, k_p, v_p, sg_p, p_p, gate_p = pre(x_prompt.reshape(n_p, D_MODEL), cos_p, sin_p,
                                                _row_tile(n_p, 512), _row_tile(Tp, 256))

    def seqs(x):
        return x.reshape(Bp, Tp, -1)

    o_ret_p, s_ret_p = _retention(seqs(q_p), seqs(k_p), seqs(v_p), seqs(sg_p), s_ret_m, gn,
                                  nb=RET_PROMPT_SEQS, rows=RET_CHUNK, valid=RET_CHUNK)
    o_wkv_p, h_wkv_p, shift_p = _rwkv(seqs(p_p), shift_m, h_wkv_m, rwkv_params, ones_bd,
                                      nb=RWKV_PROMPT_SEQS, rows=RWKV_CHUNK, valid=RWKV_CHUNK)
    y_prompt = post(h_p, o_ret_p.reshape(n_p, RET_V), o_wkv_p.reshape(n_p, RWKV_W), gate_p, _row_tile(n_p, 512))

    return (y_prompt.reshape(Bp, Tp, D_MODEL).astype(x_prompt.dtype),
            y_sample.reshape(Bs, Ts, D_MODEL).astype(x_sample.dtype),
            s_ret_p.astype(st_dtype), h_wkv_p.astype(st_dtype),
            shift_p.reshape(Bp, SHIFT_W).astype(st_dtype),
            s_ret_s.astype(st_dtype), h_wkv_s.astype(st_dtype),
            shift_s.reshape(Bs, SHIFT_W).astype(st_dtype))
```

```python
import functools

import numpy as np
import jax
import jax.numpy as jnp
from jax import lax
from jax.experimental import pallas as pl
from jax.experimental.pallas import tpu as pltpu

D_MODEL = 1024
N_META = 16
PAST_LEN = 16384
RET_HEADS = 4
RET_DK = 64
RET_DV = 128
RET_CHUNK = 128
RWKV_HEADS = 8
RWKV_HD = 64
RWKV_W = RWKV_HEADS * RWKV_HD
DECAY_LORA = 64
AAA_LORA = 64
GATE_LORA = 128
D_FF = 2816
ROPE_BASE = 10000.0
NORM_EPS = 1e-6
RET_GN_EPS = 1e-6
RWKV_GN_EPS = 64e-5
RET_QK = RET_HEADS * RET_DK
RET_V = RET_HEADS * RET_DV
SHIFT_W = 3 * RWKV_W + DECAY_LORA + AAA_LORA + GATE_LORA
GATE_W = 2 * D_MODEL
PROJ_W = 2 * RET_QK + 2 * RET_V + SHIFT_W + GATE_W

_C_Q, _C_K, _C_V, _C_G = 0, RET_QK, 2 * RET_QK, 2 * RET_QK + RET_V
_C_P = 2 * RET_QK + 2 * RET_V
_C_GATE = _C_P + SHIFT_W

V7X_VMEM_LIMIT_BYTES = 56 * 1024 * 1024
FF_CHUNK = 256
RWKV_CHUNK = 64
RWKV_PROMPT_SEQS = 4
RWKV_SAMPLE_SEQS = 8
RET_PROMPT_SEQS = 8
RET_SAMPLE_SEQS = 32
LOG_GAMMA = tuple(float(np.log1p(-2.0 ** (-5.0 - h))) for h in range(RET_HEADS))

f32 = jnp.float32
bf16 = jnp.bfloat16


def _resident(shape):
    zeros = (0,) * len(shape)
    return pl.BlockSpec(shape, lambda *_: zeros, pipeline_mode=pl.Buffered(1))


def _rms(x, g):
    return x * lax.rsqrt(jnp.mean(x * x, axis=-1, keepdims=True) + NORM_EPS) * g


def _dot(a, b):
    return jnp.dot(a, b, preferred_element_type=f32)


def _dot_nt(a, b):
    return lax.dot_general(a, b, (((1,), (1,)), ((), ())), preferred_element_type=f32)


def _dot_tn(a, b):
    return lax.dot_general(a, b, (((0,), (0,)), ((), ())), preferred_element_type=f32)


def _swiglu_half_step(x, g_ref, wg_ref, wu_ref, wd_ref):
    xn = _rms(x, g_ref[...]).astype(bf16)
    acc = jnp.zeros(x.shape, f32)
    for c in range(D_FF // FF_CHUNK):
        sl = slice(c * FF_CHUNK, (c + 1) * FF_CHUNK)
        gt = _dot(xn, wg_ref[:, sl])
        up = _dot(xn, wu_ref[:, sl])
        act = (gt * jax.nn.sigmoid(gt) * up).astype(bf16)
        acc = acc + _dot(act, wd_ref[sl, :])
    return x + 0.5 * acc


def _ffn_kernel(x_ref, g_ref, wg_ref, wu_ref, wd_ref, o_ref):
    o_ref[...] = _swiglu_half_step(x_ref[...], g_ref, wg_ref, wu_ref, wd_ref)


_FFN_WEIGHT_SPECS = ((1, D_MODEL), (D_MODEL, D_FF), (D_MODEL, D_FF), (D_FF, D_MODEL))


def _ffn(x, norm_g, wg, wu, wd, *, tm):
    rows = x.shape[0]
    row = pl.BlockSpec((tm, D_MODEL), lambda i: (i, 0))
    return pl.pallas_call(
        _ffn_kernel,
        grid=(rows // tm,),
        in_specs=[row] + [_resident(s) for s in _FFN_WEIGHT_SPECS],
        out_specs=row,
        out_shape=jax.ShapeDtypeStruct((rows, D_MODEL), f32),
        compiler_params=pltpu.CompilerParams(dimension_semantics=("parallel",),
                                             vmem_limit_bytes=V7X_VMEM_LIMIT_BYTES),
        name="ffn",
    )(x, norm_g, wg, wu, wd)


def _swap_halves(x):
    parts = []
    for j in range(x.shape[1] // 128):
        xs = x[:, 128 * j:128 * (j + 1)]
        fwd = pltpu.roll(xs, 32, 1)
        bwd = pltpu.roll(xs, 96, 1)
        lane = lax.broadcasted_iota(jnp.int32, xs.shape, 1)
        parts.append(jnp.where((lane % RET_DK) < RET_DK // 2, bwd, fwd))
    return jnp.concatenate(parts, axis=1)


def _inproj_kernel(h_ref, g_ref, w_ref, cos_ref, sin_ref, q_ref, k_ref, v_ref, sg_ref, p_ref, gate_ref):
    un = _rms(h_ref[...], g_ref[...]).astype(bf16)
    cos = cos_ref[...]
    sin = sin_ref[...]
    q = _dot(un, w_ref[:, _C_Q:_C_K])
    q_ref[...] = q * cos + _swap_halves(q) * sin
    k = _dot(un, w_ref[:, _C_K:_C_V])
    k_ref[...] = (k * cos + _swap_halves(k) * sin) * (RET_DK ** -0.5)
    v_ref[...] = _dot(un, w_ref[:, _C_V:_C_G]).astype(v_ref.dtype)
    gr = _dot(un, w_ref[:, _C_G:_C_P])
    sg_ref[...] = gr * jax.nn.sigmoid(gr)
    p_ref[...] = _dot(un, w_ref[:, _C_P:_C_GATE])
    gate_ref[...] = jax.nn.sigmoid(_dot(un, w_ref[:, _C_GATE:PROJ_W]))


def _inproj(h, norm_g, w_in, cos, sin, *, tm):
    rows = h.shape[0]
    tab_blocks = cos.shape[0] // tm

    def rowspec(width):
        return pl.BlockSpec((tm, width), lambda i: (i, 0))

    tab = pl.BlockSpec((tm, RET_QK), lambda i: (i % tab_blocks, 0))
    widths = (RET_QK, RET_QK, RET_V, RET_V, SHIFT_W, GATE_W)
    dtypes = (f32, f32, bf16, f32, f32, f32)
    return pl.pallas_call(
        _inproj_kernel,
        grid=(rows // tm,),
        in_specs=[rowspec(D_MODEL), _resident((1, D_MODEL)), _resident((D_MODEL, PROJ_W)), tab, tab],
        out_specs=[rowspec(w) for w in widths],
        out_shape=[jax.ShapeDtypeStruct((rows, w), d) for w, d in zip(widths, dtypes)],
        compiler_params=pltpu.CompilerParams(dimension_semantics=("parallel",),
                                             vmem_limit_bytes=V7X_VMEM_LIMIT_BYTES),
        name="inproj",
    )(h, norm_g, w_in, cos, sin)


def _ret_kernel(q_ref, k_ref, v_ref, sg_ref, s0_ref, gn_ref, o_ref, s_out_ref, s_scr, o_scr, *, nb, rows, valid):
    c = pl.program_id(1)
    L = rows

    @pl.when(c == 0)
    def _():
        s_scr[...] = jnp.broadcast_to(s0_ref[...], s_scr.shape)

    ii = lax.broadcasted_iota(jnp.int32, (L, L), 0)
    jj = lax.broadcasted_iota(jnp.int32, (L, L), 1)
    diff = (ii - jj).astype(f32)
    row = lax.broadcasted_iota(jnp.int32, (L, 1), 0).astype(f32)
    mask = [jnp.where(diff >= 0, jnp.exp(lg * jnp.maximum(diff, 0.0)), 0.0) for lg in LOG_GAMMA]
    q_decay = [jnp.exp(lg * (row + 1.0)) for lg in LOG_GAMMA]
    k_decay = [jnp.exp(lg * (valid - 1.0 - row)) for lg in LOG_GAMMA]
    s_decay = [float(np.exp(lg * valid)) for lg in LOG_GAMMA]

    chains = [(j, h) for j in range(nb) for h in range(RET_HEADS)]
    qh = [q_ref[j, :, RET_DK * h:RET_DK * (h + 1)] for j, h in chains]
    kh = [k_ref[j, :, RET_DK * h:RET_DK * (h + 1)] for j, h in chains]
    vh = [v_ref[j, :, RET_DV * h:RET_DV * (h + 1)] for j, h in chains]
    scores = [(_dot_nt(qh[i].astype(bf16), kh[i].astype(bf16)) * mask[h]).astype(bf16)
              for i, (j, h) in enumerate(chains)]
    qd = [(qh[i] * q_decay[h]).astype(bf16) for i, (j, h) in enumerate(chains)]
    kd = [(kh[i] * k_decay[h]).astype(bf16) for i, (j, h) in enumerate(chains)]
    s_old = [s_scr[j, h] for j, h in chains]
    for i, (j, h) in enumerate(chains):
        o_scr[j * L:(j + 1) * L, RET_DV * h:RET_DV * (h + 1)] = (
            _dot(scores[i], vh[i]) + _dot(qd[i], s_old[i].astype(bf16)))
    for i, (j, h) in enumerate(chains):
        s_scr[j, h] = s_decay[h] * s_old[i] + _dot_tn(kd[i], vh[i])

    gn = gn_ref[...]
    sg = sg_ref[...].reshape(nb * L, RET_V)
    for h in range(RET_HEADS):
        sl = slice(RET_DV * h, RET_DV * (h + 1))
        o = o_scr[:, sl]
        mu = jnp.mean(o, axis=-1, keepdims=True)
        oc = o - mu
        var = jnp.mean(oc * oc, axis=-1, keepdims=True)
        out = oc * lax.rsqrt(var + RET_GN_EPS) * gn[:, sl] * sg[:, sl]
        o_ref[:, :, sl] = out.reshape(nb, L, RET_DV).astype(o_ref.dtype)

    @pl.when(c == pl.num_programs(1) - 1)
    def _():
        s_out_ref[...] = s_scr[...]


def _retention(q, k, v, sg, s0, gn, *, nb, rows, valid):
    B, T, _ = q.shape
    bcast = s0.shape[0] == 1

    def seq(width):
        return pl.BlockSpec((nb, rows, width), lambda b, c: (b, c, 0))

    state = pl.BlockSpec((nb, RET_HEADS, RET_DK, RET_DV), lambda b, c: (b, 0, 0, 0))
    state_in = pl.BlockSpec((1, RET_HEADS, RET_DK, RET_DV), lambda b, c: (0, 0, 0, 0)) if bcast else state
    return pl.pallas_call(
        functools.partial(_ret_kernel, nb=nb, rows=rows, valid=valid),
        grid=(B // nb, T // rows),
        in_specs=[seq(RET_QK), seq(RET_QK), seq(RET_V), seq(RET_V), state_in, _resident((1, RET_V))],
        out_specs=[seq(RET_V), state],
        out_shape=[jax.ShapeDtypeStruct((B, T, RET_V), bf16),
                   jax.ShapeDtypeStruct((B, RET_HEADS, RET_DK, RET_DV), f32)],
        scratch_shapes=[pltpu.VMEM((nb, RET_HEADS, RET_DK, RET_DV), f32), pltpu.VMEM((nb * rows, RET_V), f32)],
        compiler_params=pltpu.CompilerParams(dimension_semantics=("parallel", "arbitrary"),
                                             vmem_limit_bytes=V7X_VMEM_LIMIT_BYTES),
        name="retention",
    )(q, k, v, sg, s0, gn)


def _split3(x):
    hi = x.astype(bf16)
    r1 = x - hi.astype(f32)
    mid = r1.astype(bf16)
    lo = (r1 - mid.astype(f32)).astype(bf16)
    return hi, mid, lo


def _rwkv_kernel(p_ref, prev0_ref, h0_ref, mu_ref, w0_ref, w2_ref, a0_ref, a2_ref, g2_ref, kk_ref, ka_ref,
                 rk_ref, lng_ref, lnb_ref, ones_ref, o_ref, h_out_ref, shift_ref, h_scr, prev_scr, y_scr,
                 *, nb, rows, valid):
    c = pl.program_id(1)
    C = rows
    N = RWKV_HD
    R = nb * C
    assert C & (C - 1) == 0
    log2c = C.bit_length() - 1

    @pl.when(c == 0)
    def _():
        h_scr[...] = jnp.broadcast_to(h0_ref[...], h_scr.shape)
        prev_scr[...] = jnp.broadcast_to(prev0_ref[...], prev_scr.shape)

    p = p_ref[...].reshape(R, SHIFT_W)
    rowid = lax.broadcasted_iota(jnp.int32, (R, 1), 0)
    step = rowid & (C - 1)
    p_prev = pltpu.roll(p, 1, 0)
    for j in range(nb):
        p_prev = jnp.where(rowid == j * C, prev_scr[j], p_prev)
    for j in range(nb):
        prev_scr[j] = p[j * C + valid - 1:j * C + valid, :]
    pm = p + (p_prev - p) * mu_ref[...]
    r = pm[:, 0:RWKV_W]
    k = pm[:, RWKV_W:2 * RWKV_W]
    v = pm[:, 2 * RWKV_W:3 * RWKV_W]
    o_w = 3 * RWKV_W
    xw = pm[:, o_w:o_w + DECAY_LORA]
    xa = pm[:, o_w + DECAY_LORA:o_w + DECAY_LORA + AAA_LORA]
    xg = pm[:, o_w + DECAY_LORA + AAA_LORA:SHIFT_W]

    z = w0_ref[...] + _dot(jnp.tanh(xw).astype(bf16), w2_ref[...])
    nz = -z
    softplus = jnp.maximum(nz, 0.0) + jnp.log(1.0 + jnp.exp(-jnp.abs(nz)))
    ld = -jnp.exp(-softplus - 0.5)
    a = jax.nn.sigmoid(a0_ref[...] + _dot(xa.astype(bf16), a2_ref[...]))
    g = _dot(jax.nn.sigmoid(xg).astype(bf16), g2_ref[...])
    ones_bd = ones_ref[...]
    kk = k * kk_ref[...]
    kk = kk * lax.rsqrt(jnp.maximum(_dot((kk * kk).astype(bf16), ones_bd), 1e-24))
    kp = k * (1.0 + (a - 1.0) * ka_ref[...])
    if valid < C:
        live = (step < valid).astype(f32)
        ld = ld * live
        kk = kk * live
        kp = kp * live
        v = v * live
    b = kk * a

    ri = lax.broadcasted_iota(jnp.int32, (R, R), 0)
    rj = lax.broadcasted_iota(jnp.int32, (R, R), 1)
    tri = (((ri >> log2c) == (rj >> log2c)) & (ri >= rj)).astype(bf16)
    ld_hi, ld_mid, ld_lo = _split3(ld)
    cum = _dot(tri, ld_hi) + _dot(tri, ld_mid) + _dot(tri, ld_lo)
    last_rows = [cum[j * C + C - 1:j * C + C, :] for j in range(nb)]
    cum_last = jnp.concatenate([jnp.broadcast_to(x, (C, RWKV_W)) for x in last_rows], axis=0)
    e_in = jnp.exp(cum)
    e_ex = jnp.exp(cum - ld)
    e_neg = jnp.exp(-cum)
    e_end = jnp.exp(cum_last - cum)
    g_end = [jnp.exp(x) for x in last_rows]
    at = -kk * e_ex
    rt = r * e_in
    bt = (b * e_neg).astype(bf16)
    kt = (kp * e_neg).astype(bf16)
    bh = (b * e_end).astype(bf16)
    kh = (kp * e_end).astype(bf16)
    vb = v.astype(bf16)

    ti = lax.broadcasted_iota(jnp.int32, (C, C), 0)
    tj = lax.broadcasted_iota(jnp.int32, (C, C), 1)
    incl = ti >= tj
    strict = ti > tj
    ki = lax.broadcasted_iota(jnp.int32, (N, N), 0)
    kj = lax.broadcasted_iota(jnp.int32, (N, N), 1)
    eye_n = ki == kj
    x_lanes = lax.broadcasted_iota(jnp.int32, (C, 2 * N + C), 1) < 2 * N

    chains = [(j, h) for j in range(nb) for h in range(RWKV_HEADS)]
    n_ch = len(chains)

    def blk(x, j, h):
        return x[j * C:(j + 1) * C, N * h:N * (h + 1)]

    ar = [jnp.concatenate([blk(at, j, h), blk(rt, j, h)], axis=0).astype(bf16) for j, h in chains]
    gb = [_dot_nt(ar[i], blk(bt, j, h)) for i, (j, h) in enumerate(chains)]
    gk = [_dot_nt(ar[i], blk(kt, j, h)) for i, (j, h) in enumerate(chains)]
    lab = [jnp.where(strict, x[:C], 0.0) for x in gb]
    mrb = [jnp.where(incl, x[C:], 0.0).astype(bf16) for x in gb]
    lmk = [jnp.concatenate([jnp.where(strict, x[:C], 0.0), jnp.where(incl, x[C:], 0.0)], axis=0).astype(bf16)
           for x in gk]
    lmv = [_dot(lmk[i], blk(vb, j, h)) for i, (j, h) in enumerate(chains)]
    rhs = [jnp.concatenate([blk(at, j, h), lmv[i][:C], lab[i]], axis=1) for i, (j, h) in enumerate(chains)]
    lhs = lab
    for level in range(log2c - 1):
        out = [_dot(lhs[i].astype(bf16), rhs[i].astype(bf16)) for i in range(n_ch)]
        rhs = [out[i] + jnp.where(x_lanes, rhs[i], 0.0) for i in range(n_ch)]
        lhs = [x[:, 2 * N:] for x in out]
    xs = [x[:, :2 * N] for x in rhs]
    wu = [(xs[i] + _dot(lhs[i].astype(bf16), xs[i].astype(bf16))).astype(bf16) for i in range(n_ch)]
    pq = [_dot(mrb[i], wu[i]) for i in range(n_ch)]
    gw = [_dot_tn(blk(bh, j, h), wu[i][:, :N]) for i, (j, h) in enumerate(chains)]
    dk = [_dot_tn(jnp.concatenate([wu[i][:, N:], blk(vb, j, h)], axis=0),
                  jnp.concatenate([blk(bh, j, h), blk(kh, j, h)], axis=0))
          for i, (j, h) in enumerate(chains)]
    for i, (j, h) in enumerate(chains):
        sl = slice(N * h, N * (h + 1))
        ry = blk(rt, j, h) + pq[i][:, :N]
        y0 = pq[i][:, N:] + lmv[i][C:]
        gmat = jnp.where(eye_n, g_end[j][:, sl], 0.0) + gw[i]
        s_old = h_scr[j, h].astype(bf16)
        y_scr[j * C:(j + 1) * C, sl] = _dot_nt(ry.astype(bf16), s_old) + y0
        h_scr[j, h] = _dot_nt(s_old, gmat.astype(bf16)) + dk[i]

    y = y_scr[...]
    inv_n = 1.0 / N
    mean = _dot(y.astype(bf16), ones_bd) * inv_n
    yc = y - mean
    var = _dot((yc * yc).astype(bf16), ones_bd) * inv_n
    out = yc * lax.rsqrt(var + RWKV_GN_EPS) * lng_ref[...] + lnb_ref[...]
    bonus = _dot((r * kp * rk_ref[...]).astype(bf16), ones_bd) * v
    o_ref[...] = ((out + bonus) * g).reshape(nb, C, RWKV_W).astype(o_ref.dtype)

    @pl.when(c == pl.num_programs(1) - 1)
    def _():
        h_out_ref[...] = h_scr[...]
        shift_ref[...] = prev_scr[...]


def _rwkv(p, prev0, h0, params, ones_bd, *, nb, rows, valid):
    B, T, _ = p.shape

    def maybe_bcast(arr, tail):
        nd = len(tail) + 1
        if arr.shape[0] == 1:
            return pl.BlockSpec((1,) + tail, lambda b, c: (0,) * nd)
        return pl.BlockSpec((nb,) + tail, lambda b, c: (b,) + (0,) * (nd - 1))

    seq_in = pl.BlockSpec((nb, rows, SHIFT_W), lambda b, c: (b, c, 0))
    seq_out = pl.BlockSpec((nb, rows, RWKV_W), lambda b, c: (b, c, 0))
    st_tail = (RWKV_HEADS, RWKV_HD, RWKV_HD)
    sh_tail = (1, SHIFT_W)
    param_specs = [_resident(x.shape) for x in params]
    return pl.pallas_call(
        functools.partial(_rwkv_kernel, nb=nb, rows=rows, valid=valid),
        grid=(B // nb, T // rows),
        in_specs=[seq_in, maybe_bcast(prev0, sh_tail), maybe_bcast(h0, st_tail)] + param_specs
                 + [_resident(ones_bd.shape)],
        out_specs=[seq_out, pl.BlockSpec((nb,) + st_tail, lambda b, c: (b, 0, 0, 0)),
                   pl.BlockSpec((nb,) + sh_tail, lambda b, c: (b, 0, 0))],
        out_shape=[jax.ShapeDtypeStruct((B, T, RWKV_W), bf16),
                   jax.ShapeDtypeStruct((B,) + st_tail, f32),
                   jax.ShapeDtypeStruct((B,) + sh_tail, f32)],
        scratch_shapes=[pltpu.VMEM((nb,) + st_tail, f32), pltpu.VMEM((nb,) + sh_tail, f32),
                        pltpu.VMEM((nb * rows, RWKV_W), f32)],
        compiler_params=pltpu.CompilerParams(dimension_semantics=("parallel", "arbitrary"),
                                             vmem_limit_bytes=V7X_VMEM_LIMIT_BYTES),
        name="rwkv7",
    )(p, prev0, h0, *params, ones_bd)


def _merge_ffn_kernel(h_ref, oret_ref, orwkv_ref, gate_ref, wr_ref, ww_ref, wo_ref,
                      g_ref, wg_ref, wu_ref, wd_ref, fin_ref, o_ref):
    a = _dot(oret_ref[...], wr_ref[...])
    b = _dot(orwkv_ref[...], ww_ref[...])
    merged = gate_ref[:, :D_MODEL] * a + gate_ref[:, D_MODEL:] * b
    h = h_ref[...] + _dot(merged.astype(bf16), wo_ref[...])
    h = _swiglu_half_step(h, g_ref, wg_ref, wu_ref, wd_ref)
    o_ref[...] = _rms(h, fin_ref[...])


def _merge_ffn(h, o_ret, o_rwkv, gates, w_out_ret, w_out_rwkv, w_out, norm_g, wg, wu, wd, fin_g, *, tm):
    rows = h.shape[0]

    def rowspec(width):
        return pl.BlockSpec((tm, width), lambda i: (i, 0))

    return pl.pallas_call(
        _merge_ffn_kernel,
        grid=(rows // tm,),
        in_specs=[rowspec(D_MODEL), rowspec(RET_V), rowspec(RWKV_W), rowspec(GATE_W),
                  _resident((RET_V, D_MODEL)), _resident((RWKV_W, D_MODEL)), _resident((D_MODEL, D_MODEL))]
                 + [_resident(s) for s in _FFN_WEIGHT_SPECS] + [_resident((1, D_MODEL))],
        out_specs=rowspec(D_MODEL),
        out_shape=jax.ShapeDtypeStruct((rows, D_MODEL), f32),
        compiler_params=pltpu.CompilerParams(dimension_semantics=("parallel",),
                                             vmem_limit_bytes=V7X_VMEM_LIMIT_BYTES),
        name="merge_ffn",
    )(h, o_ret, o_rwkv, gates, w_out_ret, w_out_rwkv, w_out, norm_g, wg, wu, wd, fin_g)


def _rotary_tables(pos):
    half = RET_DK // 2
    inv_freq = ROPE_BASE ** (-jnp.arange(half, dtype=f32) / half)
    ang = pos.astype(f32)[:, None] * inv_freq[None, :]
    cos = jnp.cos(ang)
    sin = jnp.sin(ang)
    cos_t = jnp.tile(jnp.concatenate([cos, cos], axis=1), (1, RET_HEADS))
    sin_t = jnp.tile(jnp.concatenate([-sin, sin], axis=1), (1, RET_HEADS))
    return cos_t, sin_t


def _row_tile(rows, target):
    tm = min(rows, target)
    while rows % tm:
        tm -= 8
    return tm


def kernel(x_prompt, x_sample, state_ret, state_wkv, state_shift, meta_tokens, ffn1_norm, ffn1_w_gate, ffn1_w_up, ffn1_w_down, mix_norm, w_in, ret_gn_g, mu_shift, w0, w2, a0, a2, g2, k_k, k_a, r_k, lnx_g, lnx_b, w_out_ret, w_out_rwkv, w_out, ffn2_norm, ffn2_w_gate, ffn2_w_up, ffn2_w_down, final_norm):
    Bp, Tp, _ = x_prompt.shape
    Bs, Ts, _ = x_sample.shape
    st_dtype = state_ret.dtype

    def row(x):
        return x.reshape(1, -1).astype(f32)

    ffn1 = (row(ffn1_norm), ffn1_w_gate.astype(bf16), ffn1_w_up.astype(bf16), ffn1_w_down.astype(bf16))
    ffn2 = (row(ffn2_norm), ffn2_w_gate.astype(bf16), ffn2_w_up.astype(bf16), ffn2_w_down.astype(bf16))
    fin = row(final_norm)
    w_in_b = w_in.astype(bf16)
    wr_b, ww_b, wo_b = w_out_ret.astype(bf16), w_out_rwkv.astype(bf16), w_out.astype(bf16)
    rwkv_params = (row(mu_shift), row(w0), w2.astype(bf16), row(a0), a2.astype(bf16), g2.astype(bf16),
                   row(k_k), row(k_a), row(r_k), row(lnx_g), row(lnx_b))
    head_id = jnp.arange(RWKV_W, dtype=jnp.int32) // RWKV_HD
    ones_bd = (head_id[:, None] == head_id[None, :]).astype(bf16)
    gn = row(ret_gn_g)

    def pre(x, cos, sin, tm_ffn, tm_proj):
        h = _ffn(x, *ffn1, tm=tm_ffn)
        return (h,) + tuple(_inproj(h, row(mix_norm), w_in_b, cos, sin, tm=tm_proj))

    def post(h, o_ret, o_rwkv, gates, tm):
        return _merge_ffn(h, o_ret, o_rwkv, gates, wr_b, ww_b, wo_b, *ffn2, fin, tm=tm)

    n_s = Bs * Ts
    x_small = jnp.concatenate([x_sample.reshape(n_s, D_MODEL), meta_tokens.astype(x_sample.dtype)], axis=0)
    cos_s, sin_s = _rotary_tables(PAST_LEN + jnp.arange(Ts, dtype=jnp.int32))
    cos_m, sin_m = _rotary_tables(jnp.arange(N_META, dtype=jnp.int32))
    cos_small = jnp.concatenate([jnp.tile(cos_s, (Bs, 1)), cos_m], axis=0)
    sin_small = jnp.concatenate([jnp.tile(sin_s, (Bs, 1)), sin_m], axis=0)
    n_small = n_s + N_META
    h_small, q_s, k_s, v_s, sg_s, p_s, gate_s = pre(x_small, cos_small, sin_small, n_small, n_small)

    def meta(x):
        return x[n_s:].reshape(1, N_META, -1)

    zeros_ret = jnp.zeros((1, RET_HEADS, RET_DK, RET_DV), f32)
    _, s_ret_m = _retention(meta(q_s), meta(k_s), meta(v_s), meta(sg_s), zeros_ret, gn,
                            nb=1, rows=N_META, valid=N_META)
    _, h_wkv_m, shift_m = _rwkv(meta(p_s), jnp.zeros((1, 1, SHIFT_W), f32),
                                jnp.zeros((1, RWKV_HEADS, RWKV_HD, RWKV_HD), f32), rwkv_params, ones_bd,
                                nb=1, rows=N_META, valid=N_META)

    pad_t = -(-Ts // 8) * 8

    def samp(x):
        x = x[:n_s].reshape(Bs, Ts, -1)
        return jnp.pad(x, ((0, 0), (0, pad_t - Ts), (0, 0)))

    o_ret_s, s_ret_s = _retention(samp(q_s), samp(k_s), samp(v_s), samp(sg_s), state_ret.astype(f32), gn,
                                  nb=RET_SAMPLE_SEQS, rows=pad_t, valid=Ts)
    o_wkv_s, h_wkv_s, shift_s = _rwkv(samp(p_s), state_shift.astype(f32).reshape(Bs, 1, SHIFT_W),
                                      state_wkv.astype(f32), rwkv_params, ones_bd,
                                      nb=RWKV_SAMPLE_SEQS, rows=pad_t, valid=Ts)
    y_sample = post(h_small[:n_s], o_ret_s[:, :Ts].reshape(n_s, RET_V), o_wkv_s[:, :Ts].reshape(n_s, RWKV_W),
                    gate_s[:n_s], _row_tile(n_s, 512))

    n_p = Bp * Tp
    cos_p, sin_p = _rotary_tables(N_META + jnp.arange(Tp, dtype=jnp.int32))
    h_p, q_p, k_p, v_p, sg_p, p_p, gate_p = pre(x_prompt.reshape(n_p, D_MODEL), cos_p, sin_p,
                                                _row_tile(n_p, 512), _row_tile(Tp, 512))

    def seqs(x):
        return x.reshape(Bp, Tp, -1)

    o_ret_p, s_ret_p = _retention(seqs(q_p), seqs(k_p), seqs(v_p), seqs(sg_p), s_ret_m, gn,
                                  nb=RET_PROMPT_SEQS, rows=RET_CHUNK, valid=RET_CHUNK)
    o_wkv_p, h_wkv_p, shift_p = _rwkv(seqs(p_p), shift_m, h_wkv_m, rwkv_params, ones_bd,
                                      nb=RWKV_PROMPT_SEQS, rows=RWKV_CHUNK, valid=RWKV_CHUNK)
    y_prompt = post(h_p, o_ret_p.reshape(n_p, RET_V), o_wkv_p.reshape(n_p, RWKV_W), gate_p, _row_tile(n_p, 512))

    return (y_prompt.reshape(Bp, Tp, D_MODEL).astype(x_prompt.dtype),
            y_sample.reshape(Bs, Ts, D_MODEL).astype(x_sample.dtype),
            s_ret_p.astype(st_dtype), h_wkv_p.astype(st_dtype),
            shift_p.reshape(Bp, SHIFT_W).astype(st_dtype),
            s_ret_s.astype(st_dtype), h_wkv_s.astype(st_dtype),
            shift_s.reshape(Bs, SHIFT_W).astype(st_dtype))
```

```python
import functools

import numpy as np
import jax
import jax.numpy as jnp
from jax import lax
from jax.experimental import pallas as pl
from jax.experimental.pallas import tpu as pltpu

D_MODEL = 1024
N_META = 16
PAST_LEN = 16384
RET_HEADS = 4
RET_DK = 64
RET_DV = 128
RET_CHUNK = 128
RWKV_HEADS = 8
RWKV_HD = 64
RWKV_W = RWKV_HEADS * RWKV_HD
DECAY_LORA = 64
AAA_LORA = 64
GATE_LORA = 128
D_FF = 2816
ROPE_BASE = 10000.0
NORM_EPS = 1e-6
RET_GN_EPS = 1e-6
RWKV_GN_EPS = 64e-5
RET_QK = RET_HEADS * RET_DK
RET_V = RET_HEADS * RET_DV
SHIFT_W = 3 * RWKV_W + DECAY_LORA + AAA_LORA + GATE_LORA
GATE_W = 2 * D_MODEL
PROJ_W = 2 * RET_QK + 2 * RET_V + SHIFT_W + GATE_W

_C_Q, _C_K, _C_V, _C_G = 0, RET_QK, 2 * RET_QK, 2 * RET_QK + RET_V
_C_P = 2 * RET_QK + 2 * RET_V
_C_GATE = _C_P + SHIFT_W

V7X_VMEM_LIMIT_BYTES = 56 * 1024 * 1024
FF_CHUNK = 256
RWKV_CHUNK = 64
RWKV_LANE_HEADS = 2
RWKV_PROMPT_SEQS = 8
RWKV_SAMPLE_SEQS = 8
RET_PROMPT_SEQS = 8
RET_SAMPLE_SEQS = 32
LOG_GAMMA = tuple(float(np.log1p(-2.0 ** (-5.0 - h))) for h in range(RET_HEADS))

f32 = jnp.float32
bf16 = jnp.bfloat16


def _resident(shape):
    zeros = (0,) * len(shape)
    return pl.BlockSpec(shape, lambda *_: zeros, pipeline_mode=pl.Buffered(1))


def _rms(x, g):
    return x * lax.rsqrt(jnp.mean(x * x, axis=-1, keepdims=True) + NORM_EPS) * g


def _dot(a, b):
    return jnp.dot(a, b, preferred_element_type=f32)


def _dot_nt(a, b):
    return lax.dot_general(a, b, (((1,), (1,)), ((), ())), preferred_element_type=f32)


def _dot_tn(a, b):
    return lax.dot_general(a, b, (((0,), (0,)), ((), ())), preferred_element_type=f32)


def _swiglu_half_step(x, g_ref, wg_ref, wu_ref, wd_ref):
    xn = _rms(x, g_ref[...]).astype(bf16)
    acc = jnp.zeros(x.shape, f32)
    for c in range(D_FF // FF_CHUNK):
        sl = slice(c * FF_CHUNK, (c + 1) * FF_CHUNK)
        gt = _dot(xn, wg_ref[:, sl])
        up = _dot(xn, wu_ref[:, sl])
        act = (gt * jax.nn.sigmoid(gt) * up).astype(bf16)
        acc = acc + _dot(act, wd_ref[sl, :])
    return x + 0.5 * acc


def _ffn_kernel(x_ref, g_ref, wg_ref, wu_ref, wd_ref, o_ref):
    o_ref[...] = _swiglu_half_step(x_ref[...], g_ref, wg_ref, wu_ref, wd_ref)


_FFN_WEIGHT_SPECS = ((1, D_MODEL), (D_MODEL, D_FF), (D_MODEL, D_FF), (D_FF, D_MODEL))


def _ffn(x, norm_g, wg, wu, wd, *, tm):
    rows = x.shape[0]
    row = pl.BlockSpec((tm, D_MODEL), lambda i: (i, 0))
    return pl.pallas_call(
        _ffn_kernel,
        grid=(rows // tm,),
        in_specs=[row] + [_resident(s) for s in _FFN_WEIGHT_SPECS],
        out_specs=row,
        out_shape=jax.ShapeDtypeStruct((rows, D_MODEL), f32),
        compiler_params=pltpu.CompilerParams(dimension_semantics=("parallel",),
                                             vmem_limit_bytes=V7X_VMEM_LIMIT_BYTES),
        name="ffn",
    )(x, norm_g, wg, wu, wd)


def _swap_halves(x):
    parts = []
    for j in range(x.shape[1] // 128):
        xs = x[:, 128 * j:128 * (j + 1)]
        fwd = pltpu.roll(xs, 32, 1)
        bwd = pltpu.roll(xs, 96, 1)
        lane = lax.broadcasted_iota(jnp.int32, xs.shape, 1)
        parts.append(jnp.where((lane % RET_DK) < RET_DK // 2, bwd, fwd))
    return jnp.concatenate(parts, axis=1)


def _inproj_kernel(h_ref, g_ref, w_ref, cos_ref, sin_ref, q_ref, k_ref, v_ref, sg_ref, p_ref, gate_ref):
    un = _rms(h_ref[...], g_ref[...]).astype(bf16)
    cos = cos_ref[...]
    sin = sin_ref[...]
    q = _dot(un, w_ref[:, _C_Q:_C_K])
    q_ref[...] = q * cos + _swap_halves(q) * sin
    k = _dot(un, w_ref[:, _C_K:_C_V])
    k_ref[...] = (k * cos + _swap_halves(k) * sin) * (RET_DK ** -0.5)
    v_ref[...] = _dot(un, w_ref[:, _C_V:_C_G]).astype(v_ref.dtype)
    gr = _dot(un, w_ref[:, _C_G:_C_P])
    sg_ref[...] = gr * jax.nn.sigmoid(gr)
    p_ref[...] = _dot(un, w_ref[:, _C_P:_C_GATE])
    gate_ref[...] = jax.nn.sigmoid(_dot(un, w_ref[:, _C_GATE:PROJ_W]))


def _inproj(h, norm_g, w_in, cos, sin, *, tm):
    rows = h.shape[0]
    tab_blocks = cos.shape[0] // tm

    def rowspec(width):
        return pl.BlockSpec((tm, width), lambda i: (i, 0))

    tab = pl.BlockSpec((tm, RET_QK), lambda i: (i % tab_blocks, 0))
    widths = (RET_QK, RET_QK, RET_V, RET_V, SHIFT_W, GATE_W)
    dtypes = (f32, f32, bf16, f32, f32, f32)
    return pl.pallas_call(
        _inproj_kernel,
        grid=(rows // tm,),
        in_specs=[rowspec(D_MODEL), _resident((1, D_MODEL)), _resident((D_MODEL, PROJ_W)), tab, tab],
        out_specs=[rowspec(w) for w in widths],
        out_shape=[jax.ShapeDtypeStruct((rows, w), d) for w, d in zip(widths, dtypes)],
        compiler_params=pltpu.CompilerParams(dimension_semantics=("parallel",),
                                             vmem_limit_bytes=V7X_VMEM_LIMIT_BYTES),
        name="inproj",
    )(h, norm_g, w_in, cos, sin)


def _ret_kernel(q_ref, k_ref, v_ref, sg_ref, s0_ref, gn_ref, o_ref, s_out_ref, s_scr, o_scr, *, nb, rows, valid):
    c = pl.program_id(1)
    L = rows

    @pl.when(c == 0)
    def _():
        s_scr[...] = jnp.broadcast_to(s0_ref[...], s_scr.shape)

    ii = lax.broadcasted_iota(jnp.int32, (L, L), 0)
    jj = lax.broadcasted_iota(jnp.int32, (L, L), 1)
    diff = (ii - jj).astype(f32)
    row = lax.broadcasted_iota(jnp.int32, (L, 1), 0).astype(f32)
    mask = [jnp.where(diff >= 0, jnp.exp(lg * jnp.maximum(diff, 0.0)), 0.0) for lg in LOG_GAMMA]
    q_decay = [jnp.exp(lg * (row + 1.0)) for lg in LOG_GAMMA]
    k_decay = [jnp.exp(lg * (valid - 1.0 - row)) for lg in LOG_GAMMA]
    s_decay = [float(np.exp(lg * valid)) for lg in LOG_GAMMA]

    chains = [(j, h) for j in range(nb) for h in range(RET_HEADS)]
    qh = [q_ref[j, :, RET_DK * h:RET_DK * (h + 1)] for j, h in chains]
    kh = [k_ref[j, :, RET_DK * h:RET_DK * (h + 1)] for j, h in chains]
    vh = [v_ref[j, :, RET_DV * h:RET_DV * (h + 1)] for j, h in chains]
    scores = [(_dot_nt(qh[i].astype(bf16), kh[i].astype(bf16)) * mask[h]).astype(bf16)
              for i, (j, h) in enumerate(chains)]
    qd = [(qh[i] * q_decay[h]).astype(bf16) for i, (j, h) in enumerate(chains)]
    kd = [(kh[i] * k_decay[h]).astype(bf16) for i, (j, h) in enumerate(chains)]
    s_old = [s_scr[j, h] for j, h in chains]
    for i, (j, h) in enumerate(chains):
        o_scr[j * L:(j + 1) * L, RET_DV * h:RET_DV * (h + 1)] = (
            _dot(scores[i], vh[i]) + _dot(qd[i], s_old[i].astype(bf16)))
    for i, (j, h) in enumerate(chains):
        s_scr[j, h] = s_decay[h] * s_old[i] + _dot_tn(kd[i], vh[i])

    gn = gn_ref[...]
    sg = sg_ref[...].reshape(nb * L, RET_V)
    for h in range(RET_HEADS):
        sl = slice(RET_DV * h, RET_DV * (h + 1))
        o = o_scr[:, sl]
        mu = jnp.mean(o, axis=-1, keepdims=True)
        oc = o - mu
        var = jnp.mean(oc * oc, axis=-1, keepdims=True)
        out = oc * lax.rsqrt(var + RET_GN_EPS) * gn[:, sl] * sg[:, sl]
        o_ref[:, :, sl] = out.reshape(nb, L, RET_DV).astype(o_ref.dtype)

    @pl.when(c == pl.num_programs(1) - 1)
    def _():
        s_out_ref[...] = s_scr[...]


def _retention(q, k, v, sg, s0, gn, *, nb, rows, valid):
    B, T, _ = q.shape
    bcast = s0.shape[0] == 1

    def seq(width):
        return pl.BlockSpec((nb, rows, width), lambda b, c: (b, c, 0))

    state = pl.BlockSpec((nb, RET_HEADS, RET_DK, RET_DV), lambda b, c: (b, 0, 0, 0))
    state_in = pl.BlockSpec((1, RET_HEADS, RET_DK, RET_DV), lambda b, c: (0, 0, 0, 0)) if bcast else state
    return pl.pallas_call(
        functools.partial(_ret_kernel, nb=nb, rows=rows, valid=valid),
        grid=(B // nb, T // rows),
        in_specs=[seq(RET_QK), seq(RET_QK), seq(RET_V), seq(RET_V), state_in, _resident((1, RET_V))],
        out_specs=[seq(RET_V), state],
        out_shape=[jax.ShapeDtypeStruct((B, T, RET_V), bf16),
                   jax.ShapeDtypeStruct((B, RET_HEADS, RET_DK, RET_DV), f32)],
        scratch_shapes=[pltpu.VMEM((nb, RET_HEADS, RET_DK, RET_DV), f32), pltpu.VMEM((nb * rows, RET_V), f32)],
        compiler_params=pltpu.CompilerParams(dimension_semantics=("parallel", "arbitrary"),
                                             vmem_limit_bytes=V7X_VMEM_LIMIT_BYTES),
        name="retention",
    )(q, k, v, sg, s0, gn)


def _split3(x):
    hi = x.astype(bf16)
    r1 = x - hi.astype(f32)
    mid = r1.astype(bf16)
    lo = (r1 - mid.astype(f32)).astype(bf16)
    return hi, mid, lo


def _rwkv_kernel(p_ref, prev0_ref, h0_ref, mu_ref, w0_ref, w2_ref, a0_ref, a2_ref, g2_ref, kk_ref, ka_ref,
                 rk_ref, lng_ref, lnb_ref, ones_ref, o_ref, h_out_ref, shift_ref, s_scr, prev_scr, y_scr,
                 *, nb, rows, valid):
    c = pl.program_id(1)
    C = rows
    N = RWKV_HD
    R = nb * C
    assert C & (C - 1) == 0
    log2c = C.bit_length() - 1

    @pl.when(c == 0)
    def _():
        for j in range(nb):
            for h in range(RWKV_HEADS):
                s_scr[j, :, N * h:N * (h + 1)] = h0_ref[j if h0_ref.shape[0] > 1 else 0, h]
        prev_scr[...] = jnp.broadcast_to(prev0_ref[...], prev_scr.shape)

    p = p_ref[...].reshape(R, SHIFT_W)
    rowid = lax.broadcasted_iota(jnp.int32, (R, 1), 0)
    step = rowid & (C - 1)
    p_prev = pltpu.roll(p, 1, 0)
    for j in range(nb):
        p_prev = jnp.where(rowid == j * C, prev_scr[j], p_prev)
    for j in range(nb):
        prev_scr[j] = p[j * C + valid - 1:j * C + valid, :]
    pm = p + (p_prev - p) * mu_ref[...]
    r = pm[:, 0:RWKV_W]
    k = pm[:, RWKV_W:2 * RWKV_W]
    v = pm[:, 2 * RWKV_W:3 * RWKV_W]
    o_w = 3 * RWKV_W
    xw = pm[:, o_w:o_w + DECAY_LORA]
    xa = pm[:, o_w + DECAY_LORA:o_w + DECAY_LORA + AAA_LORA]
    xg = pm[:, o_w + DECAY_LORA + AAA_LORA:SHIFT_W]

    z = w0_ref[...] + _dot(jnp.tanh(xw).astype(bf16), w2_ref[...])
    nz = -z
    softplus = jnp.maximum(nz, 0.0) + jnp.log(1.0 + jnp.exp(-jnp.abs(nz)))
    ld = -jnp.exp(-softplus - 0.5)
    a = jax.nn.sigmoid(a0_ref[...] + _dot(xa.astype(bf16), a2_ref[...]))
    g = _dot(jax.nn.sigmoid(xg).astype(bf16), g2_ref[...])
    ones_bd = ones_ref[...]
    kk = k * kk_ref[...]
    kk = kk * lax.rsqrt(jnp.maximum(_dot((kk * kk).astype(bf16), ones_bd), 1e-24))
    kp = k * (1.0 + (a - 1.0) * ka_ref[...])
    if valid < C:
        live = (step < valid).astype(f32)
        ld = ld * live
        kk = kk * live
        kp = kp * live
        v = v * live
    b = kk * a

    ri = lax.broadcasted_iota(jnp.int32, (R, R), 0)
    rj = lax.broadcasted_iota(jnp.int32, (R, R), 1)
    tri = (((ri >> log2c) == (rj >> log2c)) & (ri >= rj)).astype(bf16)
    ld_hi, ld_mid, ld_lo = _split3(ld)
    cum = _dot(tri, ld_hi) + _dot(tri, ld_mid) + _dot(tri, ld_lo)
    last_rows = [cum[j * C + C - 1:j * C + C, :] for j in range(nb)]
    cum_last = jnp.concatenate([jnp.broadcast_to(x, (C, RWKV_W)) for x in last_rows], axis=0)
    e_in = jnp.exp(cum)
    e_ex = jnp.exp(cum - ld)
    e_neg = jnp.exp(-cum)
    e_end = jnp.exp(cum_last - cum)
    g_end = [jnp.exp(x) for x in last_rows]
    at = -kk * e_ex
    rt = r * e_in
    bt = (b * e_neg).astype(bf16)
    kt = (kp * e_neg).astype(bf16)
    bh = (b * e_end).astype(bf16)
    kh = (kp * e_end).astype(bf16)
    vb = v.astype(bf16)

    G = RWKV_LANE_HEADS
    GW = G * N
    log2n = N.bit_length() - 1
    def head_masks(width, log2_block):
        lane = lax.broadcasted_iota(jnp.int32, (1, width), 1)
        return [((lane >> log2_block) & (G - 1)) == h for h in range(G)]

    head_of_lane = head_masks(GW, log2n)
    head_of_lane2 = head_masks(2 * GW, log2n)
    head_of_col = head_masks(G * C, log2c)
    head_of_col2 = head_masks(2 * G * C, log2c)
    ti = lax.broadcasted_iota(jnp.int32, (C, G * C), 0)
    tj = lax.broadcasted_iota(jnp.int32, (C, G * C), 1) & (C - 1)
    incl = ti >= tj
    strict = ti > tj
    unit = ti == tj
    gi = lax.broadcasted_iota(jnp.int32, (GW, GW), 0)
    gj = lax.broadcasted_iota(jnp.int32, (GW, GW), 1)
    eye_g = gi == gj
    same_head = (gi >> log2n) == (gj >> log2n)

    def bd(a, masks):
        return jnp.concatenate([jnp.where(m, a, jnp.zeros_like(a)) for m in masks], axis=0)

    groups = [(j, q) for j in range(nb) for q in range(RWKV_HEADS // G)]
    n_gr = len(groups)

    def grp(x, j, q):
        return x[j * C:(j + 1) * C, GW * q:GW * (q + 1)]

    ar = [jnp.concatenate([grp(at, j, q), grp(rt, j, q)], axis=0).astype(bf16) for j, q in groups]
    gb = [_dot_nt(ar[i], bd(grp(bt, j, q), head_of_lane)) for i, (j, q) in enumerate(groups)]
    gk = [_dot_nt(ar[i], bd(grp(kt, j, q), head_of_lane)) for i, (j, q) in enumerate(groups)]
    lab = [jnp.where(strict, x[:C], 0.0) for x in gb]
    mrb = [jnp.where(incl, x[C:], 0.0).astype(bf16) for x in gb]
    lmk = [jnp.concatenate([jnp.where(strict, x[:C], 0.0), jnp.where(incl, x[C:], 0.0)], axis=0).astype(bf16)
           for x in gk]
    lmv = [_dot(lmk[i], bd(grp(vb, j, q), head_of_lane)) for i, (j, q) in enumerate(groups)]
    m = [_dot(x.astype(bf16), bd(x.astype(bf16), head_of_col)) for x in lab]
    t_inv = [jnp.where(unit, 1.0, x) for x in lab]
    for level in range(1, log2c):
        mb = [x.astype(bf16) for x in m]
        if level < log2c - 1:
            out = [_dot(mb[i], bd(jnp.concatenate([t_inv[i].astype(bf16), mb[i]], axis=1), head_of_col2))
                   for i in range(n_gr)]
            t_inv = [t_inv[i] + out[i][:, :G * C] for i in range(n_gr)]
            m = [x[:, G * C:] for x in out]
        else:
            t_inv = [t_inv[i] + _dot(mb[i], bd(t_inv[i].astype(bf16), head_of_col)) for i in range(n_gr)]
    wu = [_dot(t_inv[i].astype(bf16),
               bd(jnp.concatenate([grp(at, j, q).astype(bf16), lmv[i][:C].astype(bf16)], axis=1), head_of_lane2)
               ).astype(bf16) for i, (j, q) in enumerate(groups)]
    wa = [x[:, :GW] for x in wu]
    uv = [x[:, GW:] for x in wu]
    pq = [_dot(mrb[i], bd(wu[i], head_of_lane2)) for i in range(n_gr)]
    ry = [grp(rt, j, q) + pq[i][:, :GW] for i, (j, q) in enumerate(groups)]
    y0 = [lmv[i][C:] + pq[i][:, GW:] for i in range(n_gr)]
    gwt = [_dot_tn(wa[i], grp(bh, j, q)) for i, (j, q) in enumerate(groups)]
    dkf = [_dot_tn(jnp.concatenate([uv[i], grp(vb, j, q)], axis=0),
                   jnp.concatenate([grp(bh, j, q), grp(kh, j, q)], axis=0))
           for i, (j, q) in enumerate(groups)]
    for i, (j, q) in enumerate(groups):
        sl = slice(GW * q, GW * (q + 1))
        wg = jnp.where(same_head, gwt[i], 0.0) + jnp.where(eye_g, g_end[j][:, sl], 0.0)
        dk = jnp.where(head_of_lane[0], dkf[i][0:N], 0.0)
        for h in range(1, G):
            dk = dk + jnp.where(head_of_lane[h], dkf[i][N * h:N * (h + 1)], 0.0)
        s_old = s_scr[j, :, sl].astype(bf16)
        y_scr[j * C:(j + 1) * C, sl] = _dot_nt(ry[i].astype(bf16), bd(s_old, head_of_lane)) + y0[i]
        s_scr[j, :, sl] = _dot(s_old, wg.astype(bf16)) + dk

    y = y_scr[...]
    inv_n = 1.0 / N
    mean = _dot(y.astype(bf16), ones_bd) * inv_n
    yc = y - mean
    var = _dot((yc * yc).astype(bf16), ones_bd) * inv_n
    out = yc * lax.rsqrt(var + RWKV_GN_EPS) * lng_ref[...] + lnb_ref[...]
    bonus = _dot((r * kp * rk_ref[...]).astype(bf16), ones_bd) * v
    o_ref[...] = ((out + bonus) * g).reshape(nb, C, RWKV_W).astype(o_ref.dtype)

    @pl.when(c == pl.num_programs(1) - 1)
    def _():
        for j in range(nb):
            for h in range(RWKV_HEADS):
                h_out_ref[j, h] = s_scr[j, :, N * h:N * (h + 1)]
        shift_ref[...] = prev_scr[...]


def _rwkv(p, prev0, h0, params, ones_bd, *, nb, rows, valid):
    B, T, _ = p.shape

    def maybe_bcast(arr, tail):
        nd = len(tail) + 1
        if arr.shape[0] == 1:
            return pl.BlockSpec((1,) + tail, lambda b, c: (0,) * nd)
        return pl.BlockSpec((nb,) + tail, lambda b, c: (b,) + (0,) * (nd - 1))

    seq_in = pl.BlockSpec((nb, rows, SHIFT_W), lambda b, c: (b, c, 0))
    seq_out = pl.BlockSpec((nb, rows, RWKV_W), lambda b, c: (b, c, 0))
    st_tail = (RWKV_HEADS, RWKV_HD, RWKV_HD)
    sh_tail = (1, SHIFT_W)
    param_specs = [_resident(x.shape) for x in params]
    return pl.pallas_call(
        functools.partial(_rwkv_kernel, nb=nb, rows=rows, valid=valid),
        grid=(B // nb, T // rows),
        in_specs=[seq_in, maybe_bcast(prev0, sh_tail), maybe_bcast(h0, st_tail)] + param_specs
                 + [_resident(ones_bd.shape)],
        out_specs=[seq_out, pl.BlockSpec((nb,) + st_tail, lambda b, c: (b, 0, 0, 0)),
                   pl.BlockSpec((nb,) + sh_tail, lambda b, c: (b, 0, 0))],
        out_shape=[jax.ShapeDtypeStruct((B, T, RWKV_W), bf16),
                   jax.ShapeDtypeStruct((B,) + st_tail, f32),
                   jax.ShapeDtypeStruct((B,) + sh_tail, f32)],
        scratch_shapes=[pltpu.VMEM((nb, RWKV_HD, RWKV_W), f32), pltpu.VMEM((nb,) + sh_tail, f32),
                        pltpu.VMEM((nb * rows, RWKV_W), f32)],
        compiler_params=pltpu.CompilerParams(dimension_semantics=("parallel", "arbitrary"),
                                             vmem_limit_bytes=V7X_VMEM_LIMIT_BYTES),
        name="rwkv7",
    )(p, prev0, h0, *params, ones_bd)


def _merge_ffn_kernel(h_ref, oret_ref, orwkv_ref, gate_ref, wr_ref, ww_ref, wo_ref,
                      g_ref, wg_ref, wu_ref, wd_ref, fin_ref, o_ref):
    a = _dot(oret_ref[...], wr_ref[...])
    b = _dot(orwkv_ref[...], ww_ref[...])
    merged = gate_ref[:, :D_MODEL] * a + gate_ref[:, D_MODEL:] * b
    h = h_ref[...] + _dot(merged.astype(bf16), wo_ref[...])
    h = _swiglu_half_step(h, g_ref, wg_ref, wu_ref, wd_ref)
    o_ref[...] = _rms(h, fin_ref[...])


def _merge_ffn(h, o_ret, o_rwkv, gates, w_out_ret, w_out_rwkv, w_out, norm_g, wg, wu, wd, fin_g, *, tm):
    rows = h.shape[0]

    def rowspec(width):
        return pl.BlockSpec((tm, width), lambda i: (i, 0))

    return pl.pallas_call(
        _merge_ffn_kernel,
        grid=(rows // tm,),
        in_specs=[rowspec(D_MODEL), rowspec(RET_V), rowspec(RWKV_W), rowspec(GATE_W),
                  _resident((RET_V, D_MODEL)), _resident((RWKV_W, D_MODEL)), _resident((D_MODEL, D_MODEL))]
                 + [_resident(s) for s in _FFN_WEIGHT_SPECS] + [_resident((1, D_MODEL))],
        out_specs=rowspec(D_MODEL),
        out_shape=jax.ShapeDtypeStruct((rows, D_MODEL), f32),
        compiler_params=pltpu.CompilerParams(dimension_semantics=("parallel",),
                                             vmem_limit_bytes=V7X_VMEM_LIMIT_BYTES),
        name="merge_ffn",
    )(h, o_ret, o_rwkv, gates, w_out_ret, w_out_rwkv, w_out, norm_g, wg, wu, wd, fin_g)


def _rotary_tables(pos):
    half = RET_DK // 2
    inv_freq = ROPE_BASE ** (-jnp.arange(half, dtype=f32) / half)
    ang = pos.astype(f32)[:, None] * inv_freq[None, :]
    cos = jnp.cos(ang)
    sin = jnp.sin(ang)
    cos_t = jnp.tile(jnp.concatenate([cos, cos], axis=1), (1, RET_HEADS))
    sin_t = jnp.tile(jnp.concatenate([-sin, sin], axis=1), (1, RET_HEADS))
    return cos_t, sin_t


def _row_tile(rows, target):
    tm = min(rows, target)
    while rows % tm:
        tm -= 8
    return tm


def kernel(x_prompt, x_sample, state_ret, state_wkv, state_shift, meta_tokens, ffn1_norm, ffn1_w_gate, ffn1_w_up, ffn1_w_down, mix_norm, w_in, ret_gn_g, mu_shift, w0, w2, a0, a2, g2, k_k, k_a, r_k, lnx_g, lnx_b, w_out_ret, w_out_rwkv, w_out, ffn2_norm, ffn2_w_gate, ffn2_w_up, ffn2_w_down, final_norm):
    Bp, Tp, _ = x_prompt.shape
    Bs, Ts, _ = x_sample.shape
    st_dtype = state_ret.dtype

    def row(x):
        return x.reshape(1, -1).astype(f32)

    ffn1 = (row(ffn1_norm), ffn1_w_gate.astype(bf16), ffn1_w_up.astype(bf16), ffn1_w_down.astype(bf16))
    ffn2 = (row(ffn2_norm), ffn2_w_gate.astype(bf16), ffn2_w_up.astype(bf16), ffn2_w_down.astype(bf16))
    fin = row(final_norm)
    w_in_b = w_in.astype(bf16)
    wr_b, ww_b, wo_b = w_out_ret.astype(bf16), w_out_rwkv.astype(bf16), w_out.astype(bf16)
    rwkv_params = (row(mu_shift), row(w0), w2.astype(bf16), row(a0), a2.astype(bf16), g2.astype(bf16),
                   row(k_k), row(k_a), row(r_k), row(lnx_g), row(lnx_b))
    head_id = jnp.arange(RWKV_W, dtype=jnp.int32) // RWKV_HD
    ones_bd = (head_id[:, None] == head_id[None, :]).astype(bf16)
    gn = row(ret_gn_g)

    def pre(x, cos, sin, tm_ffn, tm_proj):
        h = _ffn(x, *ffn1, tm=tm_ffn)
        return (h,) + tuple(_inproj(h, row(mix_norm), w_in_b, cos, sin, tm=tm_proj))

    def post(h, o_ret, o_rwkv, gates, tm):
        return _merge_ffn(h, o_ret, o_rwkv, gates, wr_b, ww_b, wo_b, *ffn2, fin, tm=tm)

    n_s = Bs * Ts
    x_small = jnp.concatenate([x_sample.reshape(n_s, D_MODEL), meta_tokens.astype(x_sample.dtype)], axis=0)
    cos_s, sin_s = _rotary_tables(PAST_LEN + jnp.arange(Ts, dtype=jnp.int32))
    cos_m, sin_m = _rotary_tables(jnp.arange(N_META, dtype=jnp.int32))
    cos_small = jnp.concatenate([jnp.tile(cos_s, (Bs, 1)), cos_m], axis=0)
    sin_small = jnp.concatenate([jnp.tile(sin_s, (Bs, 1)), sin_m], axis=0)
    n_small = n_s + N_META
    h_small, q_s, k_s, v_s, sg_s, p_s, gate_s = pre(x_small, cos_small, sin_small, n_small, n_small)

    def meta(x):
        return x[n_s:].reshape(1, N_META, -1)

    zeros_ret = jnp.zeros((1, RET_HEADS, RET_DK, RET_DV), f32)
    _, s_ret_m = _retention(meta(q_s), meta(k_s), meta(v_s), meta(sg_s), zeros_ret, gn,
                            nb=1, rows=N_META, valid=N_META)
    _, h_wkv_m, shift_m = _rwkv(meta(p_s), jnp.zeros((1, 1, SHIFT_W), f32),
                                jnp.zeros((1, RWKV_HEADS, RWKV_HD, RWKV_HD), f32), rwkv_params, ones_bd,
                                nb=1, rows=N_META, valid=N_META)

    pad_t = -(-Ts // 8) * 8

    def samp(x):
        x = x[:n_s].reshape(Bs, Ts, -1)
        return jnp.pad(x, ((0, 0), (0, pad_t - Ts), (0, 0)))

    o_ret_s, s_ret_s = _retention(samp(q_s), samp(k_s), samp(v_s), samp(sg_s), state_ret.astype(f32), gn,
                                  nb=RET_SAMPLE_SEQS, rows=pad_t, valid=Ts)
    o_wkv_s, h_wkv_s, shift_s = _rwkv(samp(p_s), state_shift.astype(f32).reshape(Bs, 1, SHIFT_W),
                                      state_wkv.astype(f32), rwkv_params, ones_bd,
                                      nb=RWKV_SAMPLE_SEQS, rows=pad_t, valid=Ts)
    y_sample = post(h_small[:n_s], o_ret_s[:, :Ts].reshape(n_s, RET_V), o_wkv_s[:, :Ts].reshape(n_s, RWKV_W),
                    gate_s[:n_s], _row_tile(n_s, 512))

    n_p = Bp * Tp
    cos_p, sin_p = _rotary_tables(N_META + jnp.arange(Tp, dtype=jnp.int32))
    h_p, q_p, k_p, v_p, sg_p, p_p, gate_p = pre(x_prompt.reshape(n_p, D_MODEL), cos_p, sin_p,
                                                _row_tile(n_p, 512), _row_tile(Tp, 512))

    def seqs(x):
        return x.reshape(Bp, Tp, -1)

    o_ret_p, s_ret_p = _retention(seqs(q_p), seqs(k_p), seqs(v_p), seqs(sg_p), s_ret_m, gn,
                                  nb=RET_PROMPT_SEQS, rows=RET_CHUNK, valid=RET_CHUNK)
    o_wkv_p, h_wkv_p, shift_p = _rwkv(seqs(p_p), shift_m, h_wkv_m, rwkv_params, ones_bd,
                                      nb=RWKV_PROMPT_SEQS, rows=RWKV_CHUNK, valid=RWKV_CHUNK)
    y_prompt = post(h_p, o_ret_p.reshape(n_p, RET_V), o_wkv_p.reshape(n_p, RWKV_W), gate_p, _row_tile(n_p, 512))

    return (y_prompt.reshape(Bp, Tp, D_MODEL).astype(x_prompt.dtype),
            y_sample.reshape(Bs, Ts, D_MODEL).astype(x_sample.dtype),
            s_ret_p.astype(st_dtype), h_wkv_p.astype(st_dtype),
            shift_p.reshape(Bp, SHIFT_W).astype(st_dtype),
            s_ret_s.astype(st_dtype), h_wkv_s.astype(st_dtype),
            shift_s.reshape(Bs, SHIFT_W).astype(st_dtype))
```

```python
import functools

import numpy as np
import jax
import jax.numpy as jnp
from jax import lax
from jax.experimental import pallas as pl
from jax.experimental.pallas import tpu as pltpu

D_MODEL = 1024
N_META = 16
PAST_LEN = 16384
RET_HEADS = 4
RET_DK = 64
RET_DV = 128
RET_CHUNK = 128
RWKV_HEADS = 8
RWKV_HD = 64
RWKV_W = RWKV_HEADS * RWKV_HD
DECAY_LORA = 64
AAA_LORA = 64
GATE_LORA = 128
D_FF = 2816
ROPE_BASE = 10000.0
NORM_EPS = 1e-6
RET_GN_EPS = 1e-6
RWKV_GN_EPS = 64e-5
RET_QK = RET_HEADS * RET_DK
RET_V = RET_HEADS * RET_DV
SHIFT_W = 3 * RWKV_W + DECAY_LORA + AAA_LORA + GATE_LORA
GATE_W = 2 * D_MODEL
PROJ_W = 2 * RET_QK + 2 * RET_V + SHIFT_W + GATE_W

_C_Q, _C_K, _C_V, _C_G = 0, RET_QK, 2 * RET_QK, 2 * RET_QK + RET_V
_C_P = 2 * RET_QK + 2 * RET_V
_C_GATE = _C_P + SHIFT_W

V7X_VMEM_LIMIT_BYTES = 56 * 1024 * 1024
FF_CHUNK = 256
RWKV_CHUNK = 64
RWKV_LANE_HEADS = 2
RWKV_PROMPT_SEQS = 8
RET_PROMPT_SEQS = 8
RET_SAMPLE_SEQS = 32
LOG_GAMMA = tuple(float(np.log1p(-2.0 ** (-5.0 - h))) for h in range(RET_HEADS))

f32 = jnp.float32
bf16 = jnp.bfloat16


def _resident(shape):
    zeros = (0,) * len(shape)
    return pl.BlockSpec(shape, lambda *_: zeros, pipeline_mode=pl.Buffered(1))


def _rms(x, g):
    return x * lax.rsqrt(jnp.mean(x * x, axis=-1, keepdims=True) + NORM_EPS) * g


def _dot(a, b):
    return jnp.dot(a, b, preferred_element_type=f32)


def _dot_nt(a, b):
    return lax.dot_general(a, b, (((1,), (1,)), ((), ())), preferred_element_type=f32)


def _dot_tn(a, b):
    return lax.dot_general(a, b, (((0,), (0,)), ((), ())), preferred_element_type=f32)


def _swiglu_half_step(x, g_ref, wg_ref, wu_ref, wd_ref):
    xn = _rms(x, g_ref[...]).astype(bf16)
    acc = jnp.zeros(x.shape, f32)
    for c in range(D_FF // FF_CHUNK):
        sl = slice(c * FF_CHUNK, (c + 1) * FF_CHUNK)
        gt = _dot(xn, wg_ref[:, sl])
        up = _dot(xn, wu_ref[:, sl])
        act = (gt * jax.nn.sigmoid(gt) * up).astype(bf16)
        acc = acc + _dot(act, wd_ref[sl, :])
    return x + 0.5 * acc


def _ffn_kernel(x_ref, g_ref, wg_ref, wu_ref, wd_ref, o_ref):
    o_ref[...] = _swiglu_half_step(x_ref[...], g_ref, wg_ref, wu_ref, wd_ref)


_FFN_WEIGHT_SPECS = ((1, D_MODEL), (D_MODEL, D_FF), (D_MODEL, D_FF), (D_FF, D_MODEL))


def _ffn(x, norm_g, wg, wu, wd, *, tm):
    rows = x.shape[0]
    row = pl.BlockSpec((tm, D_MODEL), lambda i: (i, 0))
    return pl.pallas_call(
        _ffn_kernel,
        grid=(rows // tm,),
        in_specs=[row] + [_resident(s) for s in _FFN_WEIGHT_SPECS],
        out_specs=row,
        out_shape=jax.ShapeDtypeStruct((rows, D_MODEL), f32),
        compiler_params=pltpu.CompilerParams(dimension_semantics=("parallel",),
                                             vmem_limit_bytes=V7X_VMEM_LIMIT_BYTES),
        name="ffn",
    )(x, norm_g, wg, wu, wd)


def _swap_halves(x):
    parts = []
    for j in range(x.shape[1] // 128):
        xs = x[:, 128 * j:128 * (j + 1)]
        fwd = pltpu.roll(xs, 32, 1)
        bwd = pltpu.roll(xs, 96, 1)
        lane = lax.broadcasted_iota(jnp.int32, xs.shape, 1)
        parts.append(jnp.where((lane % RET_DK) < RET_DK // 2, bwd, fwd))
    return jnp.concatenate(parts, axis=1)


def _inproj_kernel(h_ref, g_ref, w_ref, cos_ref, sin_ref, q_ref, k_ref, v_ref, sg_ref, p_ref, gate_ref):
    un = _rms(h_ref[...], g_ref[...]).astype(bf16)
    cos = cos_ref[...]
    sin = sin_ref[...]
    q = _dot(un, w_ref[:, _C_Q:_C_K])
    q_ref[...] = q * cos + _swap_halves(q) * sin
    k = _dot(un, w_ref[:, _C_K:_C_V])
    k_ref[...] = (k * cos + _swap_halves(k) * sin) * (RET_DK ** -0.5)
    v_ref[...] = _dot(un, w_ref[:, _C_V:_C_G]).astype(v_ref.dtype)
    gr = _dot(un, w_ref[:, _C_G:_C_P])
    sg_ref[...] = gr * jax.nn.sigmoid(gr)
    p_ref[...] = _dot(un, w_ref[:, _C_P:_C_GATE])
    gate_ref[...] = jax.nn.sigmoid(_dot(un, w_ref[:, _C_GATE:PROJ_W]))


def _inproj(h, norm_g, w_in, cos, sin, *, tm):
    rows = h.shape[0]
    tab_blocks = cos.shape[0] // tm

    def rowspec(width):
        return pl.BlockSpec((tm, width), lambda i: (i, 0))

    tab = pl.BlockSpec((tm, RET_QK), lambda i: (i % tab_blocks, 0))
    widths = (RET_QK, RET_QK, RET_V, RET_V, SHIFT_W, GATE_W)
    dtypes = (f32, f32, bf16, f32, f32, f32)
    return pl.pallas_call(
        _inproj_kernel,
        grid=(rows // tm,),
        in_specs=[rowspec(D_MODEL), _resident((1, D_MODEL)), _resident((D_MODEL, PROJ_W)), tab, tab],
        out_specs=[rowspec(w) for w in widths],
        out_shape=[jax.ShapeDtypeStruct((rows, w), d) for w, d in zip(widths, dtypes)],
        compiler_params=pltpu.CompilerParams(dimension_semantics=("parallel",),
                                             vmem_limit_bytes=V7X_VMEM_LIMIT_BYTES),
        name="inproj",
    )(h, norm_g, w_in, cos, sin)


def _ret_kernel(q_ref, k_ref, v_ref, sg_ref, s0_ref, gn_ref, o_ref, s_out_ref, s_scr, o_scr, *, nb, rows, valid):
    c = pl.program_id(1)
    L = rows

    @pl.when(c == 0)
    def _():
        s_scr[...] = jnp.broadcast_to(s0_ref[...], s_scr.shape)

    ii = lax.broadcasted_iota(jnp.int32, (L, L), 0)
    jj = lax.broadcasted_iota(jnp.int32, (L, L), 1)
    diff = (ii - jj).astype(f32)
    row = lax.broadcasted_iota(jnp.int32, (L, 1), 0).astype(f32)
    mask = [jnp.where(diff >= 0, jnp.exp(lg * jnp.maximum(diff, 0.0)), 0.0) for lg in LOG_GAMMA]
    q_decay = [jnp.exp(lg * (row + 1.0)) for lg in LOG_GAMMA]
    k_decay = [jnp.exp(lg * (valid - 1.0 - row)) for lg in LOG_GAMMA]
    s_decay = [float(np.exp(lg * valid)) for lg in LOG_GAMMA]

    chains = [(j, h) for j in range(nb) for h in range(RET_HEADS)]
    qh = [q_ref[j, :, RET_DK * h:RET_DK * (h + 1)] for j, h in chains]
    kh = [k_ref[j, :, RET_DK * h:RET_DK * (h + 1)] for j, h in chains]
    vh = [v_ref[j, :, RET_DV * h:RET_DV * (h + 1)] for j, h in chains]
    scores = [(_dot_nt(qh[i].astype(bf16), kh[i].astype(bf16)) * mask[h]).astype(bf16)
              for i, (j, h) in enumerate(chains)]
    qd = [(qh[i] * q_decay[h]).astype(bf16) for i, (j, h) in enumerate(chains)]
    kd = [(kh[i] * k_decay[h]).astype(bf16) for i, (j, h) in enumerate(chains)]
    s_old = [s_scr[j, h] for j, h in chains]
    for i, (j, h) in enumerate(chains):
        o_scr[j * L:(j + 1) * L, RET_DV * h:RET_DV * (h + 1)] = (
            _dot(scores[i], vh[i]) + _dot(qd[i], s_old[i].astype(bf16)))
    for i, (j, h) in enumerate(chains):
        s_scr[j, h] = s_decay[h] * s_old[i] + _dot_tn(kd[i], vh[i])

    gn = gn_ref[...]
    sg = sg_ref[...].reshape(nb * L, RET_V)
    for h in range(RET_HEADS):
        sl = slice(RET_DV * h, RET_DV * (h + 1))
        o = o_scr[:, sl]
        mu = jnp.mean(o, axis=-1, keepdims=True)
        oc = o - mu
        var = jnp.mean(oc * oc, axis=-1, keepdims=True)
        out = oc * lax.rsqrt(var + RET_GN_EPS) * gn[:, sl] * sg[:, sl]
        o_ref[:, :, sl] = out.reshape(nb, L, RET_DV).astype(o_ref.dtype)

    @pl.when(c == pl.num_programs(1) - 1)
    def _():
        s_out_ref[...] = s_scr[...]


def _retention(q, k, v, sg, s0, gn, *, nb, rows, valid):
    B, T, _ = q.shape
    bcast = s0.shape[0] == 1

    def seq(width):
        return pl.BlockSpec((nb, rows, width), lambda b, c: (b, c, 0))

    state = pl.BlockSpec((nb, RET_HEADS, RET_DK, RET_DV), lambda b, c: (b, 0, 0, 0))
    state_in = pl.BlockSpec((1, RET_HEADS, RET_DK, RET_DV), lambda b, c: (0, 0, 0, 0)) if bcast else state
    return pl.pallas_call(
        functools.partial(_ret_kernel, nb=nb, rows=rows, valid=valid),
        grid=(B // nb, T // rows),
        in_specs=[seq(RET_QK), seq(RET_QK), seq(RET_V), seq(RET_V), state_in, _resident((1, RET_V))],
        out_specs=[seq(RET_V), state],
        out_shape=[jax.ShapeDtypeStruct((B, T, RET_V), bf16),
                   jax.ShapeDtypeStruct((B, RET_HEADS, RET_DK, RET_DV), f32)],
        scratch_shapes=[pltpu.VMEM((nb, RET_HEADS, RET_DK, RET_DV), f32), pltpu.VMEM((nb * rows, RET_V), f32)],
        compiler_params=pltpu.CompilerParams(dimension_semantics=("parallel", "arbitrary"),
                                             vmem_limit_bytes=V7X_VMEM_LIMIT_BYTES),
        name="retention",
    )(q, k, v, sg, s0, gn)


def _split3(x):
    hi = x.astype(bf16)
    r1 = x - hi.astype(f32)
    mid = r1.astype(bf16)
    lo = (r1 - mid.astype(f32)).astype(bf16)
    return hi, mid, lo


def _rwkv_features(p, p_prev, mu_ref, w0_ref, w2_ref, a0_ref, a2_ref, g2_ref, kk_ref, ka_ref, ones_bd):
    pm = p + (p_prev - p) * mu_ref[...]
    r = pm[:, 0:RWKV_W]
    k = pm[:, RWKV_W:2 * RWKV_W]
    v = pm[:, 2 * RWKV_W:3 * RWKV_W]
    o_w = 3 * RWKV_W
    xw = pm[:, o_w:o_w + DECAY_LORA]
    xa = pm[:, o_w + DECAY_LORA:o_w + DECAY_LORA + AAA_LORA]
    xg = pm[:, o_w + DECAY_LORA + AAA_LORA:SHIFT_W]
    z = w0_ref[...] + _dot(jnp.tanh(xw).astype(bf16), w2_ref[...])
    nz = -z
    softplus = jnp.maximum(nz, 0.0) + jnp.log(1.0 + jnp.exp(-jnp.abs(nz)))
    ld = -jnp.exp(-softplus - 0.5)
    a = jax.nn.sigmoid(a0_ref[...] + _dot(xa.astype(bf16), a2_ref[...]))
    g = _dot(jax.nn.sigmoid(xg).astype(bf16), g2_ref[...])
    kk = k * kk_ref[...]
    kk = kk * lax.rsqrt(jnp.maximum(_dot((kk * kk).astype(bf16), ones_bd), 1e-24))
    kp = k * (1.0 + (a - 1.0) * ka_ref[...])
    return r, kp, v, kk, a, ld, g


def _rwkv_output(y, bonus_rkv, g, lng_ref, lnb_ref, ones_bd):
    inv_n = 1.0 / RWKV_HD
    mean = _dot(y.astype(bf16), ones_bd) * inv_n
    yc = y - mean
    var = _dot((yc * yc).astype(bf16), ones_bd) * inv_n
    out = yc * lax.rsqrt(var + RWKV_GN_EPS) * lng_ref[...] + lnb_ref[...]
    rkr, v = bonus_rkv
    bonus = _dot(rkr.astype(bf16), ones_bd) * v
    return (out + bonus) * g


def _rwkv_kernel(p_ref, prev0_ref, h0_ref, mu_ref, w0_ref, w2_ref, a0_ref, a2_ref, g2_ref, kk_ref, ka_ref,
                 rk_ref, lng_ref, lnb_ref, ones_ref, o_ref, h_out_ref, shift_ref, s_scr, prev_scr, y_scr,
                 *, nb, rows, valid):
    c = pl.program_id(1)
    C = rows
    N = RWKV_HD
    R = nb * C
    assert C & (C - 1) == 0
    log2c = C.bit_length() - 1

    @pl.when(c == 0)
    def _():
        for j in range(nb):
            for h in range(RWKV_HEADS):
                s_scr[j, :, N * h:N * (h + 1)] = h0_ref[j if h0_ref.shape[0] > 1 else 0, h]
        prev_scr[...] = jnp.broadcast_to(prev0_ref[...], prev_scr.shape)

    p = p_ref[...].reshape(R, SHIFT_W)
    rowid = lax.broadcasted_iota(jnp.int32, (R, 1), 0)
    step = rowid & (C - 1)
    p_prev = pltpu.roll(p, 1, 0)
    for j in range(nb):
        p_prev = jnp.where(rowid == j * C, prev_scr[j], p_prev)
    for j in range(nb):
        prev_scr[j] = p[j * C + valid - 1:j * C + valid, :]
    ones_bd = ones_ref[...]
    r, kp, v, kk, a, ld, g = _rwkv_features(p, p_prev, mu_ref, w0_ref, w2_ref, a0_ref, a2_ref, g2_ref, kk_ref,
                                            ka_ref, ones_bd)
    if valid < C:
        live = (step < valid).astype(f32)
        ld = ld * live
        kk = kk * live
        kp = kp * live
        v = v * live
    b = kk * a

    ri = lax.broadcasted_iota(jnp.int32, (R, R), 0)
    rj = lax.broadcasted_iota(jnp.int32, (R, R), 1)
    tri = (((ri >> log2c) == (rj >> log2c)) & (ri >= rj)).astype(bf16)
    ld_hi, ld_mid, ld_lo = _split3(ld)
    cum = _dot(tri, ld_hi) + _dot(tri, ld_mid) + _dot(tri, ld_lo)
    last_rows = [cum[j * C + C - 1:j * C + C, :] for j in range(nb)]
    cum_last = jnp.concatenate([jnp.broadcast_to(x, (C, RWKV_W)) for x in last_rows], axis=0)
    e_in = jnp.exp(cum)
    e_ex = jnp.exp(cum - ld)
    e_neg = jnp.exp(-cum)
    e_end = jnp.exp(cum_last - cum)
    g_end = [jnp.exp(x) for x in last_rows]
    at = -kk * e_ex
    rt = r * e_in
    bt = (b * e_neg).astype(bf16)
    kt = (kp * e_neg).astype(bf16)
    bh = (b * e_end).astype(bf16)
    kh = (kp * e_end).astype(bf16)
    vb = v.astype(bf16)

    G = RWKV_LANE_HEADS
    GW = G * N
    log2n = N.bit_length() - 1
    def head_masks(width, log2_block):
        lane = lax.broadcasted_iota(jnp.int32, (1, width), 1)
        return [((lane >> log2_block) & (G - 1)) == h for h in range(G)]

    head_of_lane = head_masks(GW, log2n)
    head_of_lane2 = head_masks(2 * GW, log2n)
    head_of_col = head_masks(G * C, log2c)
    head_of_col2 = head_masks(2 * G * C, log2c)
    ti = lax.broadcasted_iota(jnp.int32, (C, G * C), 0)
    tj = lax.broadcasted_iota(jnp.int32, (C, G * C), 1) & (C - 1)
    incl = ti >= tj
    strict = ti > tj
    unit = ti == tj
    gi = lax.broadcasted_iota(jnp.int32, (GW, GW), 0)
    gj = lax.broadcasted_iota(jnp.int32, (GW, GW), 1)
    eye_g = gi == gj
    same_head = (gi >> log2n) == (gj >> log2n)

    def bd(a, masks):
        return jnp.concatenate([jnp.where(m, a, jnp.zeros_like(a)) for m in masks], axis=0)

    groups = [(j, q) for j in range(nb) for q in range(RWKV_HEADS // G)]
    n_gr = len(groups)

    def grp(x, j, q):
        return x[j * C:(j + 1) * C, GW * q:GW * (q + 1)]

    ar = [jnp.concatenate([grp(at, j, q), grp(rt, j, q)], axis=0).astype(bf16) for j, q in groups]
    gb = [_dot_nt(ar[i], bd(grp(bt, j, q), head_of_lane)) for i, (j, q) in enumerate(groups)]
    gk = [_dot_nt(ar[i], bd(grp(kt, j, q), head_of_lane)) for i, (j, q) in enumerate(groups)]
    lab = [jnp.where(strict, x[:C], 0.0) for x in gb]
    mrb = [jnp.where(incl, x[C:], 0.0).astype(bf16) for x in gb]
    lmk = [jnp.concatenate([jnp.where(strict, x[:C], 0.0), jnp.where(incl, x[C:], 0.0)], axis=0).astype(bf16)
           for x in gk]
    lmv = [_dot(lmk[i], bd(grp(vb, j, q), head_of_lane)) for i, (j, q) in enumerate(groups)]
    m = [_dot(x.astype(bf16), bd(x.astype(bf16), head_of_col)) for x in lab]
    t_inv = [jnp.where(unit, 1.0, x) for x in lab]
    for level in range(1, log2c):
        mb = [x.astype(bf16) for x in m]
        if level < log2c - 1:
            out = [_dot(mb[i], bd(jnp.concatenate([t_inv[i].astype(bf16), mb[i]], axis=1), head_of_col2))
                   for i in range(n_gr)]
            t_inv = [t_inv[i] + out[i][:, :G * C] for i in range(n_gr)]
            m = [x[:, G * C:] for x in out]
        else:
            t_inv = [t_inv[i] + _dot(mb[i], bd(t_inv[i].astype(bf16), head_of_col)) for i in range(n_gr)]
    wu = [_dot(t_inv[i].astype(bf16),
               bd(jnp.concatenate([grp(at, j, q).astype(bf16), lmv[i][:C].astype(bf16)], axis=1), head_of_lane2)
               ).astype(bf16) for i, (j, q) in enumerate(groups)]
    wa = [x[:, :GW] for x in wu]
    uv = [x[:, GW:] for x in wu]
    pq = [_dot(mrb[i], bd(wu[i], head_of_lane2)) for i in range(n_gr)]
    ry = [grp(rt, j, q) + pq[i][:, :GW] for i, (j, q) in enumerate(groups)]
    y0 = [lmv[i][C:] + pq[i][:, GW:] for i in range(n_gr)]
    gwt = [_dot_tn(wa[i], grp(bh, j, q)) for i, (j, q) in enumerate(groups)]
    dkf = [_dot_tn(jnp.concatenate([uv[i], grp(vb, j, q)], axis=0),
                   jnp.concatenate([grp(bh, j, q), grp(kh, j, q)], axis=0))
           for i, (j, q) in enumerate(groups)]
    for i, (j, q) in enumerate(groups):
        sl = slice(GW * q, GW * (q + 1))
        wg = jnp.where(same_head, gwt[i], 0.0) + jnp.where(eye_g, g_end[j][:, sl], 0.0)
        dk = jnp.where(head_of_lane[0], dkf[i][0:N], 0.0)
        for h in range(1, G):
            dk = dk + jnp.where(head_of_lane[h], dkf[i][N * h:N * (h + 1)], 0.0)
        s_old = s_scr[j, :, sl].astype(bf16)
        y_scr[j * C:(j + 1) * C, sl] = _dot_nt(ry[i].astype(bf16), bd(s_old, head_of_lane)) + y0[i]
        s_scr[j, :, sl] = _dot(s_old, wg.astype(bf16)) + dk

    out = _rwkv_output(y_scr[...], (r * kp * rk_ref[...], v), g, lng_ref, lnb_ref, ones_bd)
    o_ref[...] = out.reshape(nb, C, RWKV_W).astype(o_ref.dtype)

    @pl.when(c == pl.num_programs(1) - 1)
    def _():
        for j in range(nb):
            for h in range(RWKV_HEADS):
                h_out_ref[j, h] = s_scr[j, :, N * h:N * (h + 1)]
        shift_ref[...] = prev_scr[...]


def _rwkv(p, prev0, h0, params, ones_bd, *, nb, rows, valid):
    B, T, _ = p.shape

    def maybe_bcast(arr, tail):
        nd = len(tail) + 1
        if arr.shape[0] == 1:
            return pl.BlockSpec((1,) + tail, lambda b, c: (0,) * nd)
        return pl.BlockSpec((nb,) + tail, lambda b, c: (b,) + (0,) * (nd - 1))

    seq_in = pl.BlockSpec((nb, rows, SHIFT_W), lambda b, c: (b, c, 0))
    seq_out = pl.BlockSpec((nb, rows, RWKV_W), lambda b, c: (b, c, 0))
    st_tail = (RWKV_HEADS, RWKV_HD, RWKV_HD)
    sh_tail = (1, SHIFT_W)
    param_specs = [_resident(x.shape) for x in params]
    return pl.pallas_call(
        functools.partial(_rwkv_kernel, nb=nb, rows=rows, valid=valid),
        grid=(B // nb, T // rows),
        in_specs=[seq_in, maybe_bcast(prev0, sh_tail), maybe_bcast(h0, st_tail)] + param_specs
                 + [_resident(ones_bd.shape)],
        out_specs=[seq_out, pl.BlockSpec((nb,) + st_tail, lambda b, c: (b, 0, 0, 0)),
                   pl.BlockSpec((nb,) + sh_tail, lambda b, c: (b, 0, 0))],
        out_shape=[jax.ShapeDtypeStruct((B, T, RWKV_W), bf16),
                   jax.ShapeDtypeStruct((B,) + st_tail, f32),
                   jax.ShapeDtypeStruct((B,) + sh_tail, f32)],
        scratch_shapes=[pltpu.VMEM((nb, RWKV_HD, RWKV_W), f32), pltpu.VMEM((nb,) + sh_tail, f32),
                        pltpu.VMEM((nb * rows, RWKV_W), f32)],
        compiler_params=pltpu.CompilerParams(dimension_semantics=("parallel", "arbitrary"),
                                             vmem_limit_bytes=V7X_VMEM_LIMIT_BYTES),
        name="rwkv7",
    )(p, prev0, h0, *params, ones_bd)


def _rwkv_step_kernel(p_ref, shift0_ref, s_ref, mu_ref, w0_ref, w2_ref, a0_ref, a2_ref, g2_ref, kk_ref, ka_ref,
                      rk_ref, lng_ref, lnb_ref, ones_ref, o_ref, s_out_ref, shift_ref,
                      feat_scr, y_scr, g_scr, rkr_scr, v_scr, *, steps, batch):
    h = pl.program_id(0)
    T, B, N = steps, batch, RWKV_HD
    key_tiles = N // 8
    f_r, f_w, f_k, f_v, f_a, f_b = range(6)

    @pl.when(h == 0)
    def _():
        p = p_ref[...]
        p_prev = jnp.concatenate([shift0_ref[...], p[:(T - 1) * B]], axis=0)
        r, kp, v, kk, a, ld, g = _rwkv_features(p, p_prev, mu_ref, w0_ref, w2_ref, a0_ref, a2_ref, g2_ref,
                                                kk_ref, ka_ref, ones_ref[...])
        for idx, x in enumerate((r, jnp.exp(ld), kp, v, -kk, kk * a)):
            for t in range(T):
                feat_scr[idx, t] = x[t * B:(t + 1) * B, :].T
        g_scr[...] = g
        rkr_scr[...] = r * kp * rk_ref[...]
        v_scr[...] = v
        shift_ref[...] = p[(T - 1) * B:]

    base = pl.multiple_of(h * N, N)

    def tile(idx, t, kt):
        return feat_scr[idx, t, pl.ds(pl.multiple_of(base + 8 * kt, 8), 8), :]

    def keysum(s, idx, t):
        acc = s[0] * tile(idx, t, 0)
        for kt in range(1, key_tiles):
            acc = acc + s[kt] * tile(idx, t, kt)
        return jnp.sum(acc, axis=0, keepdims=True)

    def value_group(vg, carry):
        v_rows = pl.ds(pl.multiple_of(base + 8 * vg, 8), 8)
        v_tiles = [feat_scr[f_v, t, v_rows, :] for t in range(T)]
        y_rows = [[] for _ in range(T)]
        for i in range(8):
            row0 = pl.multiple_of((8 * vg + i) * N, N)
            s = [s_ref[0, pl.ds(row0 + 8 * kt, 8), :] for kt in range(key_tiles)]
            for t in range(T):
                sa = jnp.broadcast_to(keysum(s, f_a, t), (8, B))
                vv = jnp.broadcast_to(v_tiles[t][i:i + 1, :], (8, B))
                s = [s[kt] * tile(f_w, t, kt) + sa * tile(f_b, t, kt) + vv * tile(f_k, t, kt)
                     for kt in range(key_tiles)]
                y_rows[t].append(keysum(s, f_r, t))
            for kt in range(key_tiles):
                s_out_ref[0, pl.ds(row0 + 8 * kt, 8), :] = s[kt]
        for t in range(T):
            y_scr[t, v_rows, :] = jnp.concatenate(y_rows[t], axis=0)
        return carry

    lax.fori_loop(0, N // 8, value_group, 0)

    @pl.when(h == pl.num_programs(0) - 1)
    def _():
        y = jnp.concatenate([y_scr[t].T for t in range(T)], axis=0)
        out = _rwkv_output(y, (rkr_scr[...], v_scr[...]), g_scr[...], lng_ref, lnb_ref, ones_ref[...])
        o_ref[...] = out.astype(o_ref.dtype)


def _rwkv_step(p, shift0, state, params, ones_bd, *, steps):
    B = state.shape[0]
    rows = steps * B
    n_state = RWKV_HD * RWKV_HD
    s_in = jnp.transpose(state, (1, 2, 3, 0)).reshape(RWKV_HEADS, n_state, B)
    s_spec = pl.BlockSpec((1, n_state, B), lambda h: (h, 0, 0))
    o, s_out, shift = pl.pallas_call(
        functools.partial(_rwkv_step_kernel, steps=steps, batch=B),
        grid=(RWKV_HEADS,),
        in_specs=[_resident((rows, SHIFT_W)), _resident((B, SHIFT_W)), s_spec]
                 + [_resident(x.shape) for x in params] + [_resident(ones_bd.shape)],
        out_specs=[pl.BlockSpec((rows, RWKV_W), lambda h: (0, 0)), s_spec,
                   pl.BlockSpec((B, SHIFT_W), lambda h: (0, 0))],
        out_shape=[jax.ShapeDtypeStruct((rows, RWKV_W), bf16),
                   jax.ShapeDtypeStruct((RWKV_HEADS, n_state, B), f32),
                   jax.ShapeDtypeStruct((B, SHIFT_W), f32)],
        scratch_shapes=[pltpu.VMEM((6, steps, RWKV_W, B), f32), pltpu.VMEM((steps, RWKV_W, B), f32),
                        pltpu.VMEM((rows, RWKV_W), f32), pltpu.VMEM((rows, RWKV_W), f32),
                        pltpu.VMEM((rows, RWKV_W), f32)],
        compiler_params=pltpu.CompilerParams(dimension_semantics=("arbitrary",),
                                             vmem_limit_bytes=V7X_VMEM_LIMIT_BYTES),
        name="rwkv7_step",
    )(p, shift0, s_in, *params, ones_bd)
    s_out = jnp.transpose(s_out.reshape(RWKV_HEADS, RWKV_HD, RWKV_HD, B), (3, 0, 1, 2))
    return o, s_out, shift


def _merge_ffn_kernel(h_ref, oret_ref, orwkv_ref, gate_ref, wr_ref, ww_ref, wo_ref,
                      g_ref, wg_ref, wu_ref, wd_ref, fin_ref, o_ref):
    a = _dot(oret_ref[...], wr_ref[...])
    b = _dot(orwkv_ref[...], ww_ref[...])
    merged = gate_ref[:, :D_MODEL] * a + gate_ref[:, D_MODEL:] * b
    h = h_ref[...] + _dot(merged.astype(bf16), wo_ref[...])
    h = _swiglu_half_step(h, g_ref, wg_ref, wu_ref, wd_ref)
    o_ref[...] = _rms(h, fin_ref[...])


def _merge_ffn(h, o_ret, o_rwkv, gates, w_out_ret, w_out_rwkv, w_out, norm_g, wg, wu, wd, fin_g, *, tm):
    rows = h.shape[0]

    def rowspec(width):
        return pl.BlockSpec((tm, width), lambda i: (i, 0))

    return pl.pallas_call(
        _merge_ffn_kernel,
        grid=(rows // tm,),
        in_specs=[rowspec(D_MODEL), rowspec(RET_V), rowspec(RWKV_W), rowspec(GATE_W),
                  _resident((RET_V, D_MODEL)), _resident((RWKV_W, D_MODEL)), _resident((D_MODEL, D_MODEL))]
                 + [_resident(s) for s in _FFN_WEIGHT_SPECS] + [_resident((1, D_MODEL))],
        out_specs=rowspec(D_MODEL),
        out_shape=jax.ShapeDtypeStruct((rows, D_MODEL), f32),
        compiler_params=pltpu.CompilerParams(dimension_semantics=("parallel",),
                                             vmem_limit_bytes=V7X_VMEM_LIMIT_BYTES),
        name="merge_ffn",
    )(h, o_ret, o_rwkv, gates, w_out_ret, w_out_rwkv, w_out, norm_g, wg, wu, wd, fin_g)


def _rotary_tables(pos):
    half = RET_DK // 2
    inv_freq = ROPE_BASE ** (-jnp.arange(half, dtype=f32) / half)
    ang = pos.astype(f32)[:, None] * inv_freq[None, :]
    cos = jnp.cos(ang)
    sin = jnp.sin(ang)
    cos_t = jnp.tile(jnp.concatenate([cos, cos], axis=1), (1, RET_HEADS))
    sin_t = jnp.tile(jnp.concatenate([-sin, sin], axis=1), (1, RET_HEADS))
    return cos_t, sin_t


def _row_tile(rows, target):
    tm = min(rows, target)
    while rows % tm:
        tm -= 8
    return tm


def kernel(x_prompt, x_sample, state_ret, state_wkv, state_shift, meta_tokens, ffn1_norm, ffn1_w_gate, ffn1_w_up, ffn1_w_down, mix_norm, w_in, ret_gn_g, mu_shift, w0, w2, a0, a2, g2, k_k, k_a, r_k, lnx_g, lnx_b, w_out_ret, w_out_rwkv, w_out, ffn2_norm, ffn2_w_gate, ffn2_w_up, ffn2_w_down, final_norm):
    Bp, Tp, _ = x_prompt.shape
    Bs, Ts, _ = x_sample.shape
    st_dtype = state_ret.dtype

    def row(x):
        return x.reshape(1, -1).astype(f32)

    ffn1 = (row(ffn1_norm), ffn1_w_gate.astype(bf16), ffn1_w_up.astype(bf16), ffn1_w_down.astype(bf16))
    ffn2 = (row(ffn2_norm), ffn2_w_gate.astype(bf16), ffn2_w_up.astype(bf16), ffn2_w_down.astype(bf16))
    fin = row(final_norm)
    w_in_b = w_in.astype(bf16)
    wr_b, ww_b, wo_b = w_out_ret.astype(bf16), w_out_rwkv.astype(bf16), w_out.astype(bf16)
    rwkv_params = (row(mu_shift), row(w0), w2.astype(bf16), row(a0), a2.astype(bf16), g2.astype(bf16),
                   row(k_k), row(k_a), row(r_k), row(lnx_g), row(lnx_b))
    head_id = jnp.arange(RWKV_W, dtype=jnp.int32) // RWKV_HD
    ones_bd = (head_id[:, None] == head_id[None, :]).astype(bf16)
    gn = row(ret_gn_g)

    def pre(x, cos, sin, tm_ffn, tm_proj):
        h = _ffn(x, *ffn1, tm=tm_ffn)
        return (h,) + tuple(_inproj(h, row(mix_norm), w_in_b, cos, sin, tm=tm_proj))

    def post(h, o_ret, o_rwkv, gates, tm):
        return _merge_ffn(h, o_ret, o_rwkv, gates, wr_b, ww_b, wo_b, *ffn2, fin, tm=tm)

    n_s = Bs * Ts
    x_small = jnp.concatenate([jnp.swapaxes(x_sample, 0, 1).reshape(n_s, D_MODEL),
                               meta_tokens.astype(x_sample.dtype)], axis=0)
    cos_s, sin_s = _rotary_tables(PAST_LEN + jnp.arange(Ts, dtype=jnp.int32))
    cos_m, sin_m = _rotary_tables(jnp.arange(N_META, dtype=jnp.int32))
    cos_small = jnp.concatenate([jnp.repeat(cos_s, Bs, axis=0), cos_m], axis=0)
    sin_small = jnp.concatenate([jnp.repeat(sin_s, Bs, axis=0), sin_m], axis=0)
    n_small = n_s + N_META
    h_small, q_s, k_s, v_s, sg_s, p_s, gate_s = pre(x_small, cos_small, sin_small, n_small, n_small)

    def meta(x):
        return x[n_s:].reshape(1, N_META, -1)

    zeros_ret = jnp.zeros((1, RET_HEADS, RET_DK, RET_DV), f32)
    _, s_ret_m = _retention(meta(q_s), meta(k_s), meta(v_s), meta(sg_s), zeros_ret, gn,
                            nb=1, rows=N_META, valid=N_META)
    _, h_wkv_m, shift_m = _rwkv(meta(p_s), jnp.zeros((1, 1, SHIFT_W), f32),
                                jnp.zeros((1, RWKV_HEADS, RWKV_HD, RWKV_HD), f32), rwkv_params, ones_bd,
                                nb=1, rows=N_META, valid=N_META)

    pad_t = -(-Ts // 8) * 8

    def samp(x):
        x = jnp.swapaxes(x[:n_s].reshape(Ts, Bs, -1), 0, 1)
        return jnp.pad(x, ((0, 0), (0, pad_t - Ts), (0, 0)))

    o_ret_s, s_ret_s = _retention(samp(q_s), samp(k_s), samp(v_s), samp(sg_s), state_ret.astype(f32), gn,
                                  nb=RET_SAMPLE_SEQS, rows=pad_t, valid=Ts)
    o_ret_s = jnp.swapaxes(o_ret_s[:, :Ts], 0, 1).reshape(n_s, RET_V)
    o_wkv_s, h_wkv_s, shift_s = _rwkv_step(p_s[:n_s], state_shift.astype(f32), state_wkv.astype(f32),
                                           rwkv_params, ones_bd, steps=Ts)
    y_sample = post(h_small[:n_s], o_ret_s, o_wkv_s, gate_s[:n_s], _row_tile(n_s, 512))
    y_sample = jnp.swapaxes(y_sample.reshape(Ts, Bs, D_MODEL), 0, 1)

    n_p = Bp * Tp
    cos_p, sin_p = _rotary_tables(N_META + jnp.arange(Tp, dtype=jnp.int32))
    h_p, q_p, k_p, v_p, sg_p, p_p, gate_p = pre(x_prompt.reshape(n_p, D_MODEL), cos_p, sin_p,
                                                _row_tile(n_p, 512), _row_tile(Tp, 512))

    def seqs(x):
        return x.reshape(Bp, Tp, -1)

    o_ret_p, s_ret_p = _retention(seqs(q_p), seqs(k_p), seqs(v_p), seqs(sg_p), s_ret_m, gn,
                                  nb=RET_PROMPT_SEQS, rows=RET_CHUNK, valid=RET_CHUNK)
    o_wkv_p, h_wkv_p, shift_p = _rwkv(seqs(p_p), shift_m, h_wkv_m, rwkv_params, ones_bd,
                                      nb=RWKV_PROMPT_SEQS, rows=RWKV_CHUNK, valid=RWKV_CHUNK)
    y_prompt = post(h_p, o_ret_p.reshape(n_p, RET_V), o_wkv_p.reshape(n_p, RWKV_W), gate_p, _row_tile(n_p, 512))

    return (y_prompt.reshape(Bp, Tp, D_MODEL).astype(x_prompt.dtype),
            y_sample.reshape(Bs, Ts, D_MODEL).astype(x_sample.dtype),
            s_ret_p.astype(st_dtype), h_wkv_p.astype(st_dtype),
            shift_p.reshape(Bp, SHIFT_W).astype(st_dtype),
            s_ret_s.astype(st_dtype), h_wkv_s.astype(st_dtype),
            shift_s.reshape(Bs, SHIFT_W).astype(st_dtype))
```

```python
import functools

import numpy as np
import jax
import jax.numpy as jnp
from jax import lax
from jax.experimental import pallas as pl
from jax.experimental.pallas import tpu as pltpu

D_MODEL = 1024
N_META = 16
PAST_LEN = 16384
RET_HEADS = 4
RET_DK = 64
RET_DV = 128
RET_CHUNK = 128
RWKV_HEADS = 8
RWKV_HD = 64
RWKV_W = RWKV_HEADS * RWKV_HD
DECAY_LORA = 64
AAA_LORA = 64
GATE_LORA = 128
D_FF = 2816
ROPE_BASE = 10000.0
NORM_EPS = 1e-6
RET_GN_EPS = 1e-6
RWKV_GN_EPS = 64e-5
RET_QK = RET_HEADS * RET_DK
RET_V = RET_HEADS * RET_DV
SHIFT_W = 3 * RWKV_W + DECAY_LORA + AAA_LORA + GATE_LORA
GATE_W = 2 * D_MODEL
PROJ_W = 2 * RET_QK + 2 * RET_V + SHIFT_W + GATE_W

_C_Q, _C_K, _C_V, _C_G = 0, RET_QK, 2 * RET_QK, 2 * RET_QK + RET_V
_C_P = 2 * RET_QK + 2 * RET_V
_C_GATE = _C_P + SHIFT_W

V7X_VMEM_LIMIT_BYTES = 56 * 1024 * 1024
FF_CHUNK = 256
RWKV_CHUNK = 64
RWKV_LANE_HEADS = 2
RWKV_PROMPT_SEQS = 8
RET_PROMPT_SEQS = 8
RET_SAMPLE_SEQS = 32
LOG_GAMMA = tuple(float(np.log1p(-2.0 ** (-5.0 - h))) for h in range(RET_HEADS))

f32 = jnp.float32
bf16 = jnp.bfloat16


def _resident(shape):
    zeros = (0,) * len(shape)
    return pl.BlockSpec(shape, lambda *_: zeros, pipeline_mode=pl.Buffered(1))


def _rms(x, g):
    return x * lax.rsqrt(jnp.mean(x * x, axis=-1, keepdims=True) + NORM_EPS) * g


def _dot(a, b):
    return jnp.dot(a, b, preferred_element_type=f32)


def _dot_nt(a, b):
    return lax.dot_general(a, b, (((1,), (1,)), ((), ())), preferred_element_type=f32)


def _dot_tn(a, b):
    return lax.dot_general(a, b, (((0,), (0,)), ((), ())), preferred_element_type=f32)


FF_CHUNKS = D_FF // FF_CHUNK


def _swiglu_half_step(x, g_ref, gate, up, down):
    xn = _rms(x, g_ref[...]).astype(bf16)
    acc = jnp.zeros(x.shape, f32)
    for c in range(FF_CHUNKS):
        gt = _dot(xn, gate(c))
        ut = _dot(xn, up(c))
        act = (gt * jax.nn.sigmoid(gt) * ut).astype(bf16)
        acc = acc + _dot(act, down(c))
    return x + 0.5 * acc


def _column_chunks(w_ref):
    return lambda c: w_ref[:, c * FF_CHUNK:(c + 1) * FF_CHUNK]


def _row_chunks(w_ref):
    return lambda c: w_ref[c * FF_CHUNK:(c + 1) * FF_CHUNK, :]


def _ffn_kernel(x_ref, g_ref, wg_ref, wu_ref, wd_ref, o_ref):
    o_ref[...] = _swiglu_half_step(x_ref[...], g_ref, _column_chunks(wg_ref), _column_chunks(wu_ref),
                                   _row_chunks(wd_ref))


_FFN_WEIGHT_SPECS = ((1, D_MODEL), (D_MODEL, D_FF), (D_MODEL, D_FF), (D_FF, D_MODEL))


def _ffn_chunk_specs(step_of):
    col = pl.BlockSpec((D_MODEL, FF_CHUNK), lambda i: (0, step_of(i)))
    return [col, col, pl.BlockSpec((FF_CHUNK, D_MODEL), lambda i: (step_of(i), 0))]


def _ffn_bf16_shapes():
    return [jax.ShapeDtypeStruct(s, bf16) for s in _FFN_WEIGHT_SPECS[1:]]


def _ffn_cast_kernel(x_ref, g_ref, wg_f, wu_f, wd_f, win_f, o_ref, wg_o, wu_o, wd_o, win_o, wg_s, wu_s, wd_s):
    i = pl.program_id(0)

    @pl.when(i < FF_CHUNKS)
    def _():
        for w_f, w_s, w_o in ((wg_f, wg_s, wg_o), (wu_f, wu_s, wu_o), (wd_f, wd_s, wd_o)):
            w = w_f[...].astype(bf16)
            w_s[i] = w
            w_o[...] = w

    @pl.when(i >= FF_CHUNKS)
    def _():
        o_ref[...] = _swiglu_half_step(x_ref[...], g_ref, lambda c: wg_s[c], lambda c: wu_s[c], lambda c: wd_s[c])
        win_o[...] = win_f[...].astype(bf16)


def _ffn_cast(x, norm_g, wg, wu, wd, w_in, *, tm):
    rows = x.shape[0]
    tiles = rows // tm
    in_chunks = PROJ_W // FF_CHUNK
    assert in_chunks <= tiles
    row = pl.BlockSpec((tm, D_MODEL), lambda i: (jnp.maximum(i - FF_CHUNKS, 0), 0))
    win = pl.BlockSpec((D_MODEL, FF_CHUNK), lambda i: (0, jnp.clip(i - FF_CHUNKS, 0, in_chunks - 1)))
    weight_chunks = _ffn_chunk_specs(lambda i: jnp.minimum(i, FF_CHUNKS - 1))
    return pl.pallas_call(
        _ffn_cast_kernel,
        grid=(FF_CHUNKS + tiles,),
        in_specs=[row, _resident((1, D_MODEL))] + weight_chunks + [win],
        out_specs=[row] + weight_chunks + [win],
        out_shape=[jax.ShapeDtypeStruct((rows, D_MODEL), f32)] + _ffn_bf16_shapes()
                  + [jax.ShapeDtypeStruct((D_MODEL, PROJ_W), bf16)],
        scratch_shapes=[pltpu.VMEM((FF_CHUNKS, D_MODEL, FF_CHUNK), bf16), pltpu.VMEM((FF_CHUNKS, D_MODEL, FF_CHUNK), bf16),
                        pltpu.VMEM((FF_CHUNKS, FF_CHUNK, D_MODEL), bf16)],
        compiler_params=pltpu.CompilerParams(dimension_semantics=("arbitrary",),
                                             vmem_limit_bytes=V7X_VMEM_LIMIT_BYTES),
        name="ffn_cast",
    )(x, norm_g, wg, wu, wd, w_in)


def _ffn(x, norm_g, wg, wu, wd, *, tm):
    rows = x.shape[0]
    row = pl.BlockSpec((tm, D_MODEL), lambda i: (i, 0))
    return pl.pallas_call(
        _ffn_kernel,
        grid=(rows // tm,),
        in_specs=[row] + [_resident(s) for s in _FFN_WEIGHT_SPECS],
        out_specs=row,
        out_shape=jax.ShapeDtypeStruct((rows, D_MODEL), f32),
        compiler_params=pltpu.CompilerParams(dimension_semantics=("parallel",),
                                             vmem_limit_bytes=V7X_VMEM_LIMIT_BYTES),
        name="ffn",
    )(x, norm_g, wg, wu, wd)


def _swap_halves(x):
    parts = []
    for j in range(x.shape[1] // 128):
        xs = x[:, 128 * j:128 * (j + 1)]
        fwd = pltpu.roll(xs, 32, 1)
        bwd = pltpu.roll(xs, 96, 1)
        lane = lax.broadcasted_iota(jnp.int32, xs.shape, 1)
        parts.append(jnp.where((lane % RET_DK) < RET_DK // 2, bwd, fwd))
    return jnp.concatenate(parts, axis=1)


def _inproj_cast_kernel(h_ref, g_ref, w_ref, cos_ref, sin_ref, wg_f, wu_f, wd_f,
                        q_ref, k_ref, v_ref, sg_ref, p_ref, gate_ref, wg_o, wu_o, wd_o):
    _inproj_kernel(h_ref, g_ref, w_ref, cos_ref, sin_ref, q_ref, k_ref, v_ref, sg_ref, p_ref, gate_ref)

    @pl.when(pl.program_id(0) < FF_CHUNKS)
    def _():
        for w_f, w_o in ((wg_f, wg_o), (wu_f, wu_o), (wd_f, wd_o)):
            w_o[...] = w_f[...].astype(bf16)


def _inproj_kernel(h_ref, g_ref, w_ref, cos_ref, sin_ref, q_ref, k_ref, v_ref, sg_ref, p_ref, gate_ref):
    un = _rms(h_ref[...], g_ref[...]).astype(bf16)
    gate_ref[...] = jax.nn.sigmoid(_dot(un, w_ref[:, _C_GATE:PROJ_W]))
    gr = _dot(un, w_ref[:, _C_G:_C_P])
    sg_ref[...] = gr * jax.nn.sigmoid(gr)
    cos = cos_ref[...]
    sin = sin_ref[...]
    q = _dot(un, w_ref[:, _C_Q:_C_K])
    q_ref[...] = q * cos + _swap_halves(q) * sin
    k = _dot(un, w_ref[:, _C_K:_C_V])
    k_ref[...] = (k * cos + _swap_halves(k) * sin) * (RET_DK ** -0.5)
    v_ref[...] = _dot(un, w_ref[:, _C_V:_C_G]).astype(v_ref.dtype)
    p_ref[...] = _dot(un, w_ref[:, _C_P:_C_GATE])


def _inproj(h, norm_g, w_in, cos, sin, *, tm, cast_ffn=None):
    rows = h.shape[0]
    tiles = rows // tm
    tab_blocks = cos.shape[0] // tm

    def rowspec(width):
        return pl.BlockSpec((tm, width), lambda i: (i, 0))

    tab = pl.BlockSpec((tm, RET_QK), lambda i: (i % tab_blocks, 0))
    widths = (RET_QK, RET_QK, RET_V, RET_V, SHIFT_W, GATE_W)
    dtypes = (f32, f32, bf16, f32, f32, f32)
    in_specs = [rowspec(D_MODEL), _resident((1, D_MODEL)), _resident((D_MODEL, PROJ_W)), tab, tab]
    out_specs = [rowspec(w) for w in widths]
    out_shape = [jax.ShapeDtypeStruct((rows, w), d) for w, d in zip(widths, dtypes)]
    operands = (h, norm_g, w_in, cos, sin)
    if cast_ffn is not None:
        assert FF_CHUNKS <= tiles
        weight_chunks = _ffn_chunk_specs(lambda i: jnp.minimum(i, FF_CHUNKS - 1))
        in_specs += weight_chunks
        out_specs += weight_chunks
        out_shape += _ffn_bf16_shapes()
        operands += tuple(cast_ffn)
    return pl.pallas_call(
        _inproj_kernel if cast_ffn is None else _inproj_cast_kernel,
        grid=(tiles,),
        in_specs=in_specs,
        out_specs=out_specs,
        out_shape=out_shape,
        compiler_params=pltpu.CompilerParams(dimension_semantics=("arbitrary",),
                                             vmem_limit_bytes=V7X_VMEM_LIMIT_BYTES),
        name="inproj",
    )(*operands)


def _ret_kernel(q_ref, k_ref, v_ref, sg_ref, s0_ref, gn_ref, o_ref, s_out_ref, s_scr, o_scr, *, nb, rows, valid):
    c = pl.program_id(1)
    L = rows

    @pl.when(c == 0)
    def _():
        s_scr[...] = jnp.broadcast_to(s0_ref[...], s_scr.shape)

    ii = lax.broadcasted_iota(jnp.int32, (L, L), 0)
    jj = lax.broadcasted_iota(jnp.int32, (L, L), 1)
    diff = (ii - jj).astype(f32)
    row = lax.broadcasted_iota(jnp.int32, (L, 1), 0).astype(f32)
    mask = [jnp.where(diff >= 0, jnp.exp(lg * jnp.maximum(diff, 0.0)), 0.0) for lg in LOG_GAMMA]
    q_decay = [jnp.exp(lg * (row + 1.0)) for lg in LOG_GAMMA]
    k_decay = [jnp.exp(lg * (valid - 1.0 - row)) for lg in LOG_GAMMA]
    s_decay = [float(np.exp(lg * valid)) for lg in LOG_GAMMA]

    chains = [(j, h) for j in range(nb) for h in range(RET_HEADS)]
    qh = [q_ref[j, :, RET_DK * h:RET_DK * (h + 1)] for j, h in chains]
    kh = [k_ref[j, :, RET_DK * h:RET_DK * (h + 1)] for j, h in chains]
    vh = [v_ref[j, :, RET_DV * h:RET_DV * (h + 1)] for j, h in chains]
    scores = [(_dot_nt(qh[i].astype(bf16), kh[i].astype(bf16)) * mask[h]).astype(bf16)
              for i, (j, h) in enumerate(chains)]
    qd = [(qh[i] * q_decay[h]).astype(bf16) for i, (j, h) in enumerate(chains)]
    kd = [(kh[i] * k_decay[h]).astype(bf16) for i, (j, h) in enumerate(chains)]
    s_old = [s_scr[j, h] for j, h in chains]
    for i, (j, h) in enumerate(chains):
        o_scr[j * L:(j + 1) * L, RET_DV * h:RET_DV * (h + 1)] = (
            _dot(scores[i], vh[i]) + _dot(qd[i], s_old[i].astype(bf16)))
    for i, (j, h) in enumerate(chains):
        s_scr[j, h] = s_decay[h] * s_old[i] + _dot_tn(kd[i], vh[i])

    gn = gn_ref[...]
    sg = sg_ref[...].reshape(nb * L, RET_V)
    for h in range(RET_HEADS):
        sl = slice(RET_DV * h, RET_DV * (h + 1))
        o = o_scr[:, sl]
        mu = jnp.mean(o, axis=-1, keepdims=True)
        oc = o - mu
        var = jnp.mean(oc * oc, axis=-1, keepdims=True)
        out = oc * lax.rsqrt(var + RET_GN_EPS) * gn[:, sl] * sg[:, sl]
        o_ref[:, :, sl] = out.reshape(nb, L, RET_DV).astype(o_ref.dtype)

    @pl.when(c == pl.num_programs(1) - 1)
    def _():
        s_out_ref[...] = s_scr[...]


def _retention(q, k, v, sg, s0, gn, *, nb, rows, valid):
    B, T, _ = q.shape
    bcast = s0.shape[0] == 1

    def seq(width):
        return pl.BlockSpec((nb, rows, width), lambda b, c: (b, c, 0))

    state = pl.BlockSpec((nb, RET_HEADS, RET_DK, RET_DV), lambda b, c: (b, 0, 0, 0))
    state_in = pl.BlockSpec((1, RET_HEADS, RET_DK, RET_DV), lambda b, c: (0, 0, 0, 0)) if bcast else state
    return pl.pallas_call(
        functools.partial(_ret_kernel, nb=nb, rows=rows, valid=valid),
        grid=(B // nb, T // rows),
        in_specs=[seq(RET_QK), seq(RET_QK), seq(RET_V), seq(RET_V), state_in, _resident((1, RET_V))],
        out_specs=[seq(RET_V), state],
        out_shape=[jax.ShapeDtypeStruct((B, T, RET_V), bf16),
                   jax.ShapeDtypeStruct((B, RET_HEADS, RET_DK, RET_DV), f32)],
        scratch_shapes=[pltpu.VMEM((nb, RET_HEADS, RET_DK, RET_DV), f32), pltpu.VMEM((nb * rows, RET_V), f32)],
        compiler_params=pltpu.CompilerParams(dimension_semantics=("parallel", "arbitrary"),
                                             vmem_limit_bytes=V7X_VMEM_LIMIT_BYTES),
        name="retention",
    )(q, k, v, sg, s0, gn)


def _split3(x):
    hi = x.astype(bf16)
    r1 = x - hi.astype(f32)
    mid = r1.astype(bf16)
    lo = (r1 - mid.astype(f32)).astype(bf16)
    return hi, mid, lo


def _rwkv_features(p, p_prev, mu_ref, w0_ref, w2_ref, a0_ref, a2_ref, g2_ref, kk_ref, ka_ref, ones_bd):
    pm = p + (p_prev - p) * mu_ref[...]
    r = pm[:, 0:RWKV_W]
    k = pm[:, RWKV_W:2 * RWKV_W]
    v = pm[:, 2 * RWKV_W:3 * RWKV_W]
    o_w = 3 * RWKV_W
    xw = pm[:, o_w:o_w + DECAY_LORA]
    xa = pm[:, o_w + DECAY_LORA:o_w + DECAY_LORA + AAA_LORA]
    xg = pm[:, o_w + DECAY_LORA + AAA_LORA:SHIFT_W]
    z = w0_ref[...] + _dot(jnp.tanh(xw).astype(bf16), w2_ref[...])
    nz = -z
    softplus = jnp.maximum(nz, 0.0) + jnp.log(1.0 + jnp.exp(-jnp.abs(nz)))
    ld = -jnp.exp(-softplus - 0.5)
    a = jax.nn.sigmoid(a0_ref[...] + _dot(xa.astype(bf16), a2_ref[...]))
    g = _dot(jax.nn.sigmoid(xg).astype(bf16), g2_ref[...])
    kk = k * kk_ref[...]
    kk = kk * lax.rsqrt(jnp.maximum(_dot((kk * kk).astype(bf16), ones_bd), 1e-24))
    kp = k * (1.0 + (a - 1.0) * ka_ref[...])
    return r, kp, v, kk, a, ld, g


def _rwkv_output(y, bonus_rkv, g, lng_ref, lnb_ref, ones_bd):
    inv_n = 1.0 / RWKV_HD
    mean = _dot(y.astype(bf16), ones_bd) * inv_n
    yc = y - mean
    var = _dot((yc * yc).astype(bf16), ones_bd) * inv_n
    out = yc * lax.rsqrt(var + RWKV_GN_EPS) * lng_ref[...] + lnb_ref[...]
    rkr, v = bonus_rkv
    bonus = _dot(rkr.astype(bf16), ones_bd) * v
    return (out + bonus) * g


def _rwkv_kernel(p_ref, prev0_ref, h0_ref, mu_ref, w0_ref, w2_ref, a0_ref, a2_ref, g2_ref, kk_ref, ka_ref,
                 rk_ref, lng_ref, lnb_ref, ones_ref, o_ref, h_out_ref, shift_ref, s_scr, prev_scr, y_scr,
                 *, nb, rows, valid):
    c = pl.program_id(1)
    C = rows
    N = RWKV_HD
    R = nb * C
    assert C & (C - 1) == 0
    log2c = C.bit_length() - 1

    @pl.when(c == 0)
    def _():
        for j in range(nb):
            for h in range(RWKV_HEADS):
                s_scr[j, :, N * h:N * (h + 1)] = h0_ref[j if h0_ref.shape[0] > 1 else 0, h]
        prev_scr[...] = jnp.broadcast_to(prev0_ref[...], prev_scr.shape)

    p = p_ref[...].reshape(R, SHIFT_W)
    rowid = lax.broadcasted_iota(jnp.int32, (R, 1), 0)
    step = rowid & (C - 1)
    p_prev = pltpu.roll(p, 1, 0)
    for j in range(nb):
        p_prev = jnp.where(rowid == j * C, prev_scr[j], p_prev)
    for j in range(nb):
        prev_scr[j] = p[j * C + valid - 1:j * C + valid, :]
    ones_bd = ones_ref[...]
    r, kp, v, kk, a, ld, g = _rwkv_features(p, p_prev, mu_ref, w0_ref, w2_ref, a0_ref, a2_ref, g2_ref, kk_ref,
                                            ka_ref, ones_bd)
    if valid < C:
        live = (step < valid).astype(f32)
        ld = ld * live
        kk = kk * live
        kp = kp * live
        v = v * live
    b = kk * a

    ri = lax.broadcasted_iota(jnp.int32, (R, R), 0)
    rj = lax.broadcasted_iota(jnp.int32, (R, R), 1)
    tri = (((ri >> log2c) == (rj >> log2c)) & (ri >= rj)).astype(bf16)
    ld_hi, ld_mid, ld_lo = _split3(ld)
    cum = _dot(tri, ld_hi) + _dot(tri, ld_mid) + _dot(tri, ld_lo)
    last_rows = [cum[j * C + C - 1:j * C + C, :] for j in range(nb)]
    cum_last = jnp.concatenate([jnp.broadcast_to(x, (C, RWKV_W)) for x in last_rows], axis=0)
    e_in = jnp.exp(cum)
    e_ex = jnp.exp(cum - ld)
    e_neg = jnp.exp(-cum)
    e_end = jnp.exp(cum_last - cum)
    g_end = [jnp.exp(x) for x in last_rows]
    at = -kk * e_ex
    rt = r * e_in
    bt = (b * e_neg).astype(bf16)
    kt = (kp * e_neg).astype(bf16)
    bh = (b * e_end).astype(bf16)
    kh = (kp * e_end).astype(bf16)
    vb = v.astype(bf16)

    G = RWKV_LANE_HEADS
    GW = G * N
    log2n = N.bit_length() - 1
    def head_masks(width, log2_block):
        lane = lax.broadcasted_iota(jnp.int32, (1, width), 1)
        return [((lane >> log2_block) & (G - 1)) == h for h in range(G)]

    head_of_lane = head_masks(GW, log2n)
    head_of_lane2 = head_masks(2 * GW, log2n)
    head_of_col = head_masks(G * C, log2c)
    head_of_col2 = head_masks(2 * G * C, log2c)
    ti = lax.broadcasted_iota(jnp.int32, (C, G * C), 0)
    tj = lax.broadcasted_iota(jnp.int32, (C, G * C), 1) & (C - 1)
    incl = ti >= tj
    strict = ti > tj
    unit = ti == tj
    gi = lax.broadcasted_iota(jnp.int32, (GW, GW), 0)
    gj = lax.broadcasted_iota(jnp.int32, (GW, GW), 1)
    eye_g = gi == gj
    same_head = (gi >> log2n) == (gj >> log2n)

    def bd(a, masks):
        return jnp.concatenate([jnp.where(m, a, jnp.zeros_like(a)) for m in masks], axis=0)

    groups = [(j, q) for j in range(nb) for q in range(RWKV_HEADS // G)]
    n_gr = len(groups)

    def grp(x, j, q):
        return x[j * C:(j + 1) * C, GW * q:GW * (q + 1)]

    ar = [jnp.concatenate([grp(at, j, q), grp(rt, j, q)], axis=0).astype(bf16) for j, q in groups]
    gb = [_dot_nt(ar[i], bd(grp(bt, j, q), head_of_lane)) for i, (j, q) in enumerate(groups)]
    gk = [_dot_nt(ar[i], bd(grp(kt, j, q), head_of_lane)) for i, (j, q) in enumerate(groups)]
    lab = [jnp.where(strict, x[:C], 0.0) for x in gb]
    mrb = [jnp.where(incl, x[C:], 0.0).astype(bf16) for x in gb]
    lmk = [jnp.concatenate([jnp.where(strict, x[:C], 0.0), jnp.where(incl, x[C:], 0.0)], axis=0).astype(bf16)
           for x in gk]
    lmv = [_dot(lmk[i], bd(grp(vb, j, q), head_of_lane)) for i, (j, q) in enumerate(groups)]
    m = [_dot(x.astype(bf16), bd(x.astype(bf16), head_of_col)) for x in lab]
    t_inv = [jnp.where(unit, 1.0, x) for x in lab]
    for level in range(1, log2c):
        mb = [x.astype(bf16) for x in m]
        if level < log2c - 1:
            out = [_dot(mb[i], bd(jnp.concatenate([t_inv[i].astype(bf16), mb[i]], axis=1), head_of_col2))
                   for i in range(n_gr)]
            t_inv = [t_inv[i] + out[i][:, :G * C] for i in range(n_gr)]
            m = [x[:, G * C:] for x in out]
        else:
            t_inv = [t_inv[i] + _dot(mb[i], bd(t_inv[i].astype(bf16), head_of_col)) for i in range(n_gr)]
    wu = [_dot(t_inv[i].astype(bf16),
               bd(jnp.concatenate([grp(at, j, q).astype(bf16), lmv[i][:C].astype(bf16)], axis=1), head_of_lane2)
               ).astype(bf16) for i, (j, q) in enumerate(groups)]
    wa = [x[:, :GW] for x in wu]
    uv = [x[:, GW:] for x in wu]
    pq = [_dot(mrb[i], bd(wu[i], head_of_lane2)) for i in range(n_gr)]
    ry = [grp(rt, j, q) + pq[i][:, :GW] for i, (j, q) in enumerate(groups)]
    y0 = [lmv[i][C:] + pq[i][:, GW:] for i in range(n_gr)]
    gwt = [_dot_tn(wa[i], grp(bh, j, q)) for i, (j, q) in enumerate(groups)]
    dkf = [_dot_tn(jnp.concatenate([uv[i], grp(vb, j, q)], axis=0),
                   jnp.concatenate([grp(bh, j, q), grp(kh, j, q)], axis=0))
           for i, (j, q) in enumerate(groups)]
    for i, (j, q) in enumerate(groups):
        sl = slice(GW * q, GW * (q + 1))
        wg = jnp.where(same_head, gwt[i], 0.0) + jnp.where(eye_g, g_end[j][:, sl], 0.0)
        dk = jnp.where(head_of_lane[0], dkf[i][0:N], 0.0)
        for h in range(1, G):
            dk = dk + jnp.where(head_of_lane[h], dkf[i][N * h:N * (h + 1)], 0.0)
        s_old = s_scr[j, :, sl].astype(bf16)
        y_scr[j * C:(j + 1) * C, sl] = _dot_nt(ry[i].astype(bf16), bd(s_old, head_of_lane)) + y0[i]
        s_scr[j, :, sl] = _dot(s_old, wg.astype(bf16)) + dk

    out = _rwkv_output(y_scr[...], (r * kp * rk_ref[...], v), g, lng_ref, lnb_ref, ones_bd)
    o_ref[...] = out.reshape(nb, C, RWKV_W).astype(o_ref.dtype)

    @pl.when(c == pl.num_programs(1) - 1)
    def _():
        for j in range(nb):
            for h in range(RWKV_HEADS):
                h_out_ref[j, h] = s_scr[j, :, N * h:N * (h + 1)]
        shift_ref[...] = prev_scr[...]


def _rwkv(p, prev0, h0, params, ones_bd, *, nb, rows, valid):
    B, T, _ = p.shape

    def maybe_bcast(arr, tail):
        nd = len(tail) + 1
        if arr.shape[0] == 1:
            return pl.BlockSpec((1,) + tail, lambda b, c: (0,) * nd)
        return pl.BlockSpec((nb,) + tail, lambda b, c: (b,) + (0,) * (nd - 1))

    seq_in = pl.BlockSpec((nb, rows, SHIFT_W), lambda b, c: (b, c, 0))
    seq_out = pl.BlockSpec((nb, rows, RWKV_W), lambda b, c: (b, c, 0))
    st_tail = (RWKV_HEADS, RWKV_HD, RWKV_HD)
    sh_tail = (1, SHIFT_W)
    param_specs = [_resident(x.shape) for x in params]
    return pl.pallas_call(
        functools.partial(_rwkv_kernel, nb=nb, rows=rows, valid=valid),
        grid=(B // nb, T // rows),
        in_specs=[seq_in, maybe_bcast(prev0, sh_tail), maybe_bcast(h0, st_tail)] + param_specs
                 + [_resident(ones_bd.shape)],
        out_specs=[seq_out, pl.BlockSpec((nb,) + st_tail, lambda b, c: (b, 0, 0, 0)),
                   pl.BlockSpec((nb,) + sh_tail, lambda b, c: (b, 0, 0))],
        out_shape=[jax.ShapeDtypeStruct((B, T, RWKV_W), bf16),
                   jax.ShapeDtypeStruct((B,) + st_tail, f32),
                   jax.ShapeDtypeStruct((B,) + sh_tail, f32)],
        scratch_shapes=[pltpu.VMEM((nb, RWKV_HD, RWKV_W), f32), pltpu.VMEM((nb,) + sh_tail, f32),
                        pltpu.VMEM((nb * rows, RWKV_W), f32)],
        compiler_params=pltpu.CompilerParams(dimension_semantics=("parallel", "arbitrary"),
                                             vmem_limit_bytes=V7X_VMEM_LIMIT_BYTES),
        name="rwkv7",
    )(p, prev0, h0, *params, ones_bd)


def _rwkv_step_kernel(p_ref, shift0_ref, s_ref, mu_ref, w0_ref, w2_ref, a0_ref, a2_ref, g2_ref, kk_ref, ka_ref,
                      rk_ref, lng_ref, lnb_ref, ones_ref, o_ref, s_out_ref, shift_ref,
                      feat_scr, y_scr, g_scr, rkr_scr, v_scr, *, steps, batch):
    h = pl.program_id(0)
    T, B, N = steps, batch, RWKV_HD
    key_tiles = N // 8
    f_r, f_w, f_k, f_v, f_a, f_b = range(6)

    @pl.when(h == 0)
    def _():
        p = p_ref[...]
        p_prev = jnp.concatenate([shift0_ref[...], p[:(T - 1) * B]], axis=0)
        r, kp, v, kk, a, ld, g = _rwkv_features(p, p_prev, mu_ref, w0_ref, w2_ref, a0_ref, a2_ref, g2_ref,
                                                kk_ref, ka_ref, ones_ref[...])
        for idx, x in enumerate((r, jnp.exp(ld), kp, v, -kk, kk * a)):
            for t in range(T):
                feat_scr[idx, t] = x[t * B:(t + 1) * B, :].T
        g_scr[...] = g
        rkr_scr[...] = r * kp * rk_ref[...]
        v_scr[...] = v
        shift_ref[...] = p[(T - 1) * B:]

    base = pl.multiple_of(h * N, N)

    def tile(idx, t, kt):
        return feat_scr[idx, t, pl.ds(pl.multiple_of(base + 8 * kt, 8), 8), :]

    def keysum(s, idx, t):
        acc = s[0] * tile(idx, t, 0)
        for kt in range(1, key_tiles):
            acc = acc + s[kt] * tile(idx, t, kt)
        return jnp.sum(acc, axis=0, keepdims=True)

    def value_group(vg, carry):
        v_rows = pl.ds(pl.multiple_of(base + 8 * vg, 8), 8)
        v_tiles = [feat_scr[f_v, t, v_rows, :] for t in range(T)]
        y_rows = [[] for _ in range(T)]
        for i in range(8):
            row0 = pl.multiple_of((8 * vg + i) * N, N)
            s = [s_ref[0, pl.ds(row0 + 8 * kt, 8), :] for kt in range(key_tiles)]
            for t in range(T):
                sa = jnp.broadcast_to(keysum(s, f_a, t), (8, B))
                vv = jnp.broadcast_to(v_tiles[t][i:i + 1, :], (8, B))
                s = [s[kt] * tile(f_w, t, kt) + sa * tile(f_b, t, kt) + vv * tile(f_k, t, kt)
                     for kt in range(key_tiles)]
                y_rows[t].append(keysum(s, f_r, t))
            for kt in range(key_tiles):
                s_out_ref[0, pl.ds(row0 + 8 * kt, 8), :] = s[kt]
        for t in range(T):
            y_scr[t, v_rows, :] = jnp.concatenate(y_rows[t], axis=0)
        return carry

    lax.fori_loop(0, N // 8, value_group, 0)

    @pl.when(h == pl.num_programs(0) - 1)
    def _():
        y = jnp.concatenate([y_scr[t].T for t in range(T)], axis=0)
        out = _rwkv_output(y, (rkr_scr[...], v_scr[...]), g_scr[...], lng_ref, lnb_ref, ones_ref[...])
        o_ref[...] = out.astype(o_ref.dtype)


def _rwkv_step(p, shift0, state, params, ones_bd, *, steps):
    B = state.shape[0]
    rows = steps * B
    n_state = RWKV_HD * RWKV_HD
    s_in = jnp.transpose(state, (1, 2, 3, 0)).reshape(RWKV_HEADS, n_state, B)
    s_spec = pl.BlockSpec((1, n_state, B), lambda h: (h, 0, 0))
    o, s_out, shift = pl.pallas_call(
        functools.partial(_rwkv_step_kernel, steps=steps, batch=B),
        grid=(RWKV_HEADS,),
        in_specs=[_resident((rows, SHIFT_W)), _resident((B, SHIFT_W)), s_spec]
                 + [_resident(x.shape) for x in params] + [_resident(ones_bd.shape)],
        out_specs=[pl.BlockSpec((rows, RWKV_W), lambda h: (0, 0)), s_spec,
                   pl.BlockSpec((B, SHIFT_W), lambda h: (0, 0))],
        out_shape=[jax.ShapeDtypeStruct((rows, RWKV_W), bf16),
                   jax.ShapeDtypeStruct((RWKV_HEADS, n_state, B), f32),
                   jax.ShapeDtypeStruct((B, SHIFT_W), f32)],
        scratch_shapes=[pltpu.VMEM((6, steps, RWKV_W, B), f32), pltpu.VMEM((steps, RWKV_W, B), f32),
                        pltpu.VMEM((rows, RWKV_W), f32), pltpu.VMEM((rows, RWKV_W), f32),
                        pltpu.VMEM((rows, RWKV_W), f32)],
        compiler_params=pltpu.CompilerParams(dimension_semantics=("arbitrary",),
                                             vmem_limit_bytes=V7X_VMEM_LIMIT_BYTES),
        name="rwkv7_step",
    )(p, shift0, s_in, *params, ones_bd)
    s_out = jnp.transpose(s_out.reshape(RWKV_HEADS, RWKV_HD, RWKV_HD, B), (3, 0, 1, 2))
    return o, s_out, shift


def _merge_ffn_kernel(h_ref, oret_ref, orwkv_ref, gate_ref, wr_ref, ww_ref, wo_ref,
                      g_ref, wg_ref, wu_ref, wd_ref, fin_ref, o_ref):
    a = _dot(oret_ref[...], wr_ref[...])
    b = _dot(orwkv_ref[...], ww_ref[...])
    merged = gate_ref[:, :D_MODEL] * a + gate_ref[:, D_MODEL:] * b
    h = h_ref[...] + _dot(merged.astype(bf16), wo_ref[...])
    h = _swiglu_half_step(h, g_ref, _column_chunks(wg_ref), _column_chunks(wu_ref), _row_chunks(wd_ref))
    o_ref[...] = _rms(h, fin_ref[...])


def _merge_ffn(h, o_ret, o_rwkv, gates, w_out_ret, w_out_rwkv, w_out, norm_g, wg, wu, wd, fin_g, *, tm):
    rows = h.shape[0]

    def rowspec(width):
        return pl.BlockSpec((tm, width), lambda i: (i, 0))

    return pl.pallas_call(
        _merge_ffn_kernel,
        grid=(rows // tm,),
        in_specs=[rowspec(D_MODEL), rowspec(RET_V), rowspec(RWKV_W), rowspec(GATE_W),
                  _resident((RET_V, D_MODEL)), _resident((RWKV_W, D_MODEL)), _resident((D_MODEL, D_MODEL))]
                 + [_resident(s) for s in _FFN_WEIGHT_SPECS] + [_resident((1, D_MODEL))],
        out_specs=rowspec(D_MODEL),
        out_shape=jax.ShapeDtypeStruct((rows, D_MODEL), f32),
        compiler_params=pltpu.CompilerParams(dimension_semantics=("parallel",),
                                             vmem_limit_bytes=V7X_VMEM_LIMIT_BYTES),
        name="merge_ffn",
    )(h, o_ret, o_rwkv, gates, w_out_ret, w_out_rwkv, w_out, norm_g, wg, wu, wd, fin_g)


def _rotary_tables(pos):
    half = RET_DK // 2
    inv_freq = ROPE_BASE ** (-jnp.arange(half, dtype=f32) / half)
    ang = pos.astype(f32)[:, None] * inv_freq[None, :]
    cos = jnp.cos(ang)
    sin = jnp.sin(ang)
    cos_t = jnp.tile(jnp.concatenate([cos, cos], axis=1), (1, RET_HEADS))
    sin_t = jnp.tile(jnp.concatenate([-sin, sin], axis=1), (1, RET_HEADS))
    return cos_t, sin_t


def _row_tile(rows, target):
    tm = min(rows, target)
    while rows % tm:
        tm -= 8
    return tm


def kernel(x_prompt, x_sample, state_ret, state_wkv, state_shift, meta_tokens, ffn1_norm, ffn1_w_gate, ffn1_w_up, ffn1_w_down, mix_norm, w_in, ret_gn_g, mu_shift, w0, w2, a0, a2, g2, k_k, k_a, r_k, lnx_g, lnx_b, w_out_ret, w_out_rwkv, w_out, ffn2_norm, ffn2_w_gate, ffn2_w_up, ffn2_w_down, final_norm):
    Bp, Tp, _ = x_prompt.shape
    Bs, Ts, _ = x_sample.shape
    st_dtype = state_ret.dtype

    def row(x):
        return x.reshape(1, -1).astype(f32)

    fin = row(final_norm)
    mix_g = row(mix_norm)
    wr_b, ww_b, wo_b = w_out_ret.astype(bf16), w_out_rwkv.astype(bf16), w_out.astype(bf16)
    rwkv_params = (row(mu_shift), row(w0), w2.astype(bf16), row(a0), a2.astype(bf16), g2.astype(bf16),
                   row(k_k), row(k_a), row(r_k), row(lnx_g), row(lnx_b))
    head_id = jnp.arange(RWKV_W, dtype=jnp.int32) // RWKV_HD
    ones_bd = (head_id[:, None] == head_id[None, :]).astype(bf16)
    gn = row(ret_gn_g)

    n_p = Bp * Tp
    h_p, *ffn1_b, w_in_b = _ffn_cast(x_prompt.reshape(n_p, D_MODEL), row(ffn1_norm), ffn1_w_gate, ffn1_w_up,
                                     ffn1_w_down, w_in, tm=_row_tile(n_p, 512))

    n_s = Bs * Ts
    x_small = jnp.concatenate([jnp.swapaxes(x_sample, 0, 1).reshape(n_s, D_MODEL),
                               meta_tokens.astype(x_sample.dtype)], axis=0)
    cos_s, sin_s = _rotary_tables(PAST_LEN + jnp.arange(Ts, dtype=jnp.int32))
    cos_m, sin_m = _rotary_tables(jnp.arange(N_META, dtype=jnp.int32))
    cos_small = jnp.concatenate([jnp.repeat(cos_s, Bs, axis=0), cos_m], axis=0)
    sin_small = jnp.concatenate([jnp.repeat(sin_s, Bs, axis=0), sin_m], axis=0)
    n_small = n_s + N_META
    h_small = _ffn(x_small, row(ffn1_norm), *ffn1_b, tm=n_small)
    q_s, k_s, v_s, sg_s, p_s, gate_s = _inproj(h_small, mix_g, w_in_b, cos_small, sin_small, tm=n_small)

    def meta(x):
        return x[n_s:].reshape(1, N_META, -1)

    zeros_ret = jnp.zeros((1, RET_HEADS, RET_DK, RET_DV), f32)
    _, s_ret_m = _retention(meta(q_s), meta(k_s), meta(v_s), meta(sg_s), zeros_ret, gn,
                            nb=1, rows=N_META, valid=N_META)
    _, h_wkv_m, shift_m = _rwkv(meta(p_s), jnp.zeros((1, 1, SHIFT_W), f32),
                                jnp.zeros((1, RWKV_HEADS, RWKV_HD, RWKV_HD), f32), rwkv_params, ones_bd,
                                nb=1, rows=N_META, valid=N_META)

    pad_t = -(-Ts // 8) * 8

    def samp(x):
        x = jnp.swapaxes(x[:n_s].reshape(Ts, Bs, -1), 0, 1)
        return jnp.pad(x, ((0, 0), (0, pad_t - Ts), (0, 0)))

    o_ret_s, s_ret_s = _retention(samp(q_s), samp(k_s), samp(v_s), samp(sg_s), state_ret.astype(f32), gn,
                                  nb=RET_SAMPLE_SEQS, rows=pad_t, valid=Ts)
    o_ret_s = jnp.swapaxes(o_ret_s[:, :Ts], 0, 1).reshape(n_s, RET_V)
    o_wkv_s, h_wkv_s, shift_s = _rwkv_step(p_s[:n_s], state_shift.astype(f32), state_wkv.astype(f32),
                                           rwkv_params, ones_bd, steps=Ts)

    cos_p, sin_p = _rotary_tables(N_META + jnp.arange(Tp, dtype=jnp.int32))
    q_p, k_p, v_p, sg_p, p_p, gate_p, *ffn2_b = _inproj(h_p, mix_g, w_in_b, cos_p, sin_p, tm=_row_tile(Tp, 512),
                                                        cast_ffn=(ffn2_w_gate, ffn2_w_up, ffn2_w_down))

    def post(h, o_ret, o_rwkv, gates, tm):
        return _merge_ffn(h, o_ret, o_rwkv, gates, wr_b, ww_b, wo_b, row(ffn2_norm), *ffn2_b, fin, tm=tm)

    y_sample = post(h_small[:n_s], o_ret_s, o_wkv_s, gate_s[:n_s], _row_tile(n_s, 512))
    y_sample = jnp.swapaxes(y_sample.reshape(Ts, Bs, D_MODEL), 0, 1)

    def seqs(x):
        return x.reshape(Bp, Tp, -1)

    o_ret_p, s_ret_p = _retention(seqs(q_p), seqs(k_p), seqs(v_p), seqs(sg_p), s_ret_m, gn,
                                  nb=RET_PROMPT_SEQS, rows=RET_CHUNK, valid=RET_CHUNK)
    o_wkv_p, h_wkv_p, shift_p = _rwkv(seqs(p_p), shift_m, h_wkv_m, rwkv_params, ones_bd,
                                      nb=RWKV_PROMPT_SEQS, rows=RWKV_CHUNK, valid=RWKV_CHUNK)
    y_prompt = post(h_p, o_ret_p.reshape(n_p, RET_V), o_wkv_p.reshape(n_p, RWKV_W), gate_p, _row_tile(n_p, 512))

    return (y_prompt.reshape(Bp, Tp, D_MODEL).astype(x_prompt.dtype),
            y_sample.reshape(Bs, Ts, D_MODEL).astype(x_sample.dtype),
            s_ret_p.astype(st_dtype), h_wkv_p.astype(st_dtype),
            shift_p.reshape(Bp, SHIFT_W).astype(st_dtype),
            s_ret_s.astype(st_dtype), h_wkv_s.astype(st_dtype),
            shift_s.reshape(Bs, SHIFT_W).astype(st_dtype))
```

```python
import functools

import numpy as np
import jax
import jax.numpy as jnp
from jax import lax
from jax.experimental import pallas as pl
from jax.experimental.pallas import tpu as pltpu

D_MODEL = 1024
N_META = 16
PAST_LEN = 16384
RET_HEADS = 4
RET_DK = 64
RET_DV = 128
RET_CHUNK = 128
RWKV_HEADS = 8
RWKV_HD = 64
RWKV_W = RWKV_HEADS * RWKV_HD
DECAY_LORA = 64
AAA_LORA = 64
GATE_LORA = 128
D_FF = 2816
ROPE_BASE = 10000.0
NORM_EPS = 1e-6
RET_GN_EPS = 1e-6
RWKV_GN_EPS = 64e-5
RET_QK = RET_HEADS * RET_DK
RET_V = RET_HEADS * RET_DV
SHIFT_W = 3 * RWKV_W + DECAY_LORA + AAA_LORA + GATE_LORA
GATE_W = 2 * D_MODEL
PROJ_W = 2 * RET_QK + 2 * RET_V + SHIFT_W + GATE_W

_C_Q, _C_K, _C_V, _C_G = 0, RET_QK, 2 * RET_QK, 2 * RET_QK + RET_V
_C_P = 2 * RET_QK + 2 * RET_V
_C_GATE = _C_P + SHIFT_W

V7X_VMEM_LIMIT_BYTES = 56 * 1024 * 1024
FF_CHUNK = 256
RWKV_CHUNK = 64
RWKV_LANE_HEADS = 2
RWKV_PROMPT_SEQS = 8
RET_PROMPT_SEQS = 8
RET_SAMPLE_SEQS = 32
LOG_GAMMA = tuple(float(np.log1p(-2.0 ** (-5.0 - h))) for h in range(RET_HEADS))

f32 = jnp.float32
bf16 = jnp.bfloat16


def _resident(shape):
    zeros = (0,) * len(shape)
    return pl.BlockSpec(shape, lambda *_: zeros, pipeline_mode=pl.Buffered(1))


def _rms(x, g):
    return x * lax.rsqrt(jnp.mean(x * x, axis=-1, keepdims=True) + NORM_EPS) * g


def _dot(a, b):
    return jnp.dot(a, b, preferred_element_type=f32)


def _dot_nt(a, b):
    return lax.dot_general(a, b, (((1,), (1,)), ((), ())), preferred_element_type=f32)


def _dot_tn(a, b):
    return lax.dot_general(a, b, (((0,), (0,)), ((), ())), preferred_element_type=f32)


FF_CHUNKS = D_FF // FF_CHUNK


def _swiglu_half_step(x, g_ref, gate, up, down):
    xn = _rms(x, g_ref[...]).astype(bf16)
    acc = jnp.zeros(x.shape, f32)
    for c in range(FF_CHUNKS):
        gt = _dot(xn, gate(c))
        ut = _dot(xn, up(c))
        act = (gt * jax.nn.sigmoid(gt) * ut).astype(bf16)
        acc = acc + _dot(act, down(c))
    return x + 0.5 * acc


def _column_chunks(w_ref):
    return lambda c: w_ref[:, c * FF_CHUNK:(c + 1) * FF_CHUNK]


def _row_chunks(w_ref):
    return lambda c: w_ref[c * FF_CHUNK:(c + 1) * FF_CHUNK, :]


def _ffn_kernel(x_ref, g_ref, wg_ref, wu_ref, wd_ref, o_ref):
    o_ref[...] = _swiglu_half_step(x_ref[...], g_ref, _column_chunks(wg_ref), _column_chunks(wu_ref),
                                   _row_chunks(wd_ref))


_FFN_WEIGHT_SPECS = ((1, D_MODEL), (D_MODEL, D_FF), (D_MODEL, D_FF), (D_FF, D_MODEL))


def _ffn_chunk_specs(step_of):
    col = pl.BlockSpec((D_MODEL, FF_CHUNK), lambda i: (0, step_of(i)))
    return [col, col, pl.BlockSpec((FF_CHUNK, D_MODEL), lambda i: (step_of(i), 0))]


def _ffn_bf16_shapes():
    return [jax.ShapeDtypeStruct(s, bf16) for s in _FFN_WEIGHT_SPECS[1:]]


def _ffn_cast_kernel(x_ref, g_ref, wg_f, wu_f, wd_f, win_f, o_ref, wg_o, wu_o, wd_o, win_o, wg_s, wu_s, wd_s):
    i = pl.program_id(0)

    @pl.when(i < FF_CHUNKS)
    def _():
        for w_f, w_s, w_o in ((wg_f, wg_s, wg_o), (wu_f, wu_s, wu_o), (wd_f, wd_s, wd_o)):
            w = w_f[...].astype(bf16)
            w_s[i] = w
            w_o[...] = w

    @pl.when(i >= FF_CHUNKS)
    def _():
        o_ref[...] = _swiglu_half_step(x_ref[...], g_ref, lambda c: wg_s[c], lambda c: wu_s[c], lambda c: wd_s[c])
        win_o[...] = win_f[...].astype(bf16)


def _ffn_cast(x, norm_g, wg, wu, wd, w_in, *, tm):
    rows = x.shape[0]
    tiles = rows // tm
    in_chunks = PROJ_W // FF_CHUNK
    assert in_chunks <= tiles
    row = pl.BlockSpec((tm, D_MODEL), lambda i: (jnp.maximum(i - FF_CHUNKS, 0), 0))
    win = pl.BlockSpec((D_MODEL, FF_CHUNK), lambda i: (0, jnp.clip(i - FF_CHUNKS, 0, in_chunks - 1)))
    weight_chunks = _ffn_chunk_specs(lambda i: jnp.minimum(i, FF_CHUNKS - 1))
    return pl.pallas_call(
        _ffn_cast_kernel,
        grid=(FF_CHUNKS + tiles,),
        in_specs=[row, _resident((1, D_MODEL))] + weight_chunks + [win],
        out_specs=[row] + weight_chunks + [win],
        out_shape=[jax.ShapeDtypeStruct((rows, D_MODEL), f32)] + _ffn_bf16_shapes()
                  + [jax.ShapeDtypeStruct((D_MODEL, PROJ_W), bf16)],
        scratch_shapes=[pltpu.VMEM((FF_CHUNKS, D_MODEL, FF_CHUNK), bf16), pltpu.VMEM((FF_CHUNKS, D_MODEL, FF_CHUNK), bf16),
                        pltpu.VMEM((FF_CHUNKS, FF_CHUNK, D_MODEL), bf16)],
        compiler_params=pltpu.CompilerParams(dimension_semantics=("arbitrary",),
                                             vmem_limit_bytes=V7X_VMEM_LIMIT_BYTES),
        name="ffn_cast",
    )(x, norm_g, wg, wu, wd, w_in)


def _ffn(x, norm_g, wg, wu, wd, *, tm):
    rows = x.shape[0]
    row = pl.BlockSpec((tm, D_MODEL), lambda i: (i, 0))
    return pl.pallas_call(
        _ffn_kernel,
        grid=(rows // tm,),
        in_specs=[row] + [_resident(s) for s in _FFN_WEIGHT_SPECS],
        out_specs=row,
        out_shape=jax.ShapeDtypeStruct((rows, D_MODEL), f32),
        compiler_params=pltpu.CompilerParams(dimension_semantics=("parallel",),
                                             vmem_limit_bytes=V7X_VMEM_LIMIT_BYTES),
        name="ffn",
    )(x, norm_g, wg, wu, wd)


def _swap_halves(x):
    parts = []
    for j in range(x.shape[1] // 128):
        xs = x[:, 128 * j:128 * (j + 1)]
        fwd = pltpu.roll(xs, 32, 1)
        bwd = pltpu.roll(xs, 96, 1)
        lane = lax.broadcasted_iota(jnp.int32, xs.shape, 1)
        parts.append(jnp.where((lane % RET_DK) < RET_DK // 2, bwd, fwd))
    return jnp.concatenate(parts, axis=1)


def _inproj_cast_kernel(h_ref, g_ref, w_ref, cos_ref, sin_ref, wg_f, wu_f, wd_f,
                        q_ref, k_ref, v_ref, sg_ref, p_ref, gate_ref, wg_o, wu_o, wd_o):
    _inproj_kernel(h_ref, g_ref, w_ref, cos_ref, sin_ref, q_ref, k_ref, v_ref, sg_ref, p_ref, gate_ref)

    @pl.when(pl.program_id(0) < FF_CHUNKS)
    def _():
        for w_f, w_o in ((wg_f, wg_o), (wu_f, wu_o), (wd_f, wd_o)):
            w_o[...] = w_f[...].astype(bf16)


def _inproj_kernel(h_ref, g_ref, w_ref, cos_ref, sin_ref, q_ref, k_ref, v_ref, sg_ref, p_ref, gate_ref):
    un = _rms(h_ref[...], g_ref[...]).astype(bf16)
    gate_ref[...] = jax.nn.sigmoid(_dot(un, w_ref[:, _C_GATE:PROJ_W])).astype(gate_ref.dtype)
    gr = _dot(un, w_ref[:, _C_G:_C_P])
    sg_ref[...] = (gr * jax.nn.sigmoid(gr)).astype(sg_ref.dtype)
    cos = cos_ref[...]
    sin = sin_ref[...]
    q = _dot(un, w_ref[:, _C_Q:_C_K])
    q_ref[...] = q * cos + _swap_halves(q) * sin
    k = _dot(un, w_ref[:, _C_K:_C_V])
    k_ref[...] = (k * cos + _swap_halves(k) * sin) * (RET_DK ** -0.5)
    v_ref[...] = _dot(un, w_ref[:, _C_V:_C_G]).astype(v_ref.dtype)
    p_ref[...] = _dot(un, w_ref[:, _C_P:_C_GATE])


def _inproj(h, norm_g, w_in, cos, sin, *, tm, cast_ffn=None):
    rows = h.shape[0]
    tiles = rows // tm
    tab_blocks = cos.shape[0] // tm

    def rowspec(width):
        return pl.BlockSpec((tm, width), lambda i: (i, 0))

    tab = pl.BlockSpec((tm, RET_QK), lambda i: (i % tab_blocks, 0))
    widths = (RET_QK, RET_QK, RET_V, RET_V, SHIFT_W, GATE_W)
    dtypes = (f32, f32, bf16, bf16, f32, bf16)
    in_specs = [rowspec(D_MODEL), _resident((1, D_MODEL)), _resident((D_MODEL, PROJ_W)), tab, tab]
    out_specs = [rowspec(w) for w in widths]
    out_shape = [jax.ShapeDtypeStruct((rows, w), d) for w, d in zip(widths, dtypes)]
    operands = (h, norm_g, w_in, cos, sin)
    if cast_ffn is not None:
        assert FF_CHUNKS <= tiles
        weight_chunks = _ffn_chunk_specs(lambda i: jnp.minimum(i, FF_CHUNKS - 1))
        in_specs += weight_chunks
        out_specs += weight_chunks
        out_shape += _ffn_bf16_shapes()
        operands += tuple(cast_ffn)
    return pl.pallas_call(
        _inproj_kernel if cast_ffn is None else _inproj_cast_kernel,
        grid=(tiles,),
        in_specs=in_specs,
        out_specs=out_specs,
        out_shape=out_shape,
        compiler_params=pltpu.CompilerParams(dimension_semantics=("arbitrary",),
                                             vmem_limit_bytes=V7X_VMEM_LIMIT_BYTES),
        name="inproj",
    )(*operands)


def _ret_kernel(q_ref, k_ref, v_ref, sg_ref, s0_ref, gn_ref, o_ref, s_out_ref, s_scr, o_scr, *, nb, rows, valid):
    c = pl.program_id(1)
    L = rows

    @pl.when(c == 0)
    def _():
        s_scr[...] = jnp.broadcast_to(s0_ref[...], s_scr.shape)

    ii = lax.broadcasted_iota(jnp.int32, (L, L), 0)
    jj = lax.broadcasted_iota(jnp.int32, (L, L), 1)
    diff = (ii - jj).astype(f32)
    row = lax.broadcasted_iota(jnp.int32, (L, 1), 0).astype(f32)
    mask = [jnp.where(diff >= 0, jnp.exp(lg * jnp.maximum(diff, 0.0)), 0.0) for lg in LOG_GAMMA]
    q_decay = [jnp.exp(lg * (row + 1.0)) for lg in LOG_GAMMA]
    k_decay = [jnp.exp(lg * (valid - 1.0 - row)) for lg in LOG_GAMMA]
    s_decay = [float(np.exp(lg * valid)) for lg in LOG_GAMMA]

    chains = [(j, h) for j in range(nb) for h in range(RET_HEADS)]
    qh = [q_ref[j, :, RET_DK * h:RET_DK * (h + 1)] for j, h in chains]
    kh = [k_ref[j, :, RET_DK * h:RET_DK * (h + 1)] for j, h in chains]
    vh = [v_ref[j, :, RET_DV * h:RET_DV * (h + 1)] for j, h in chains]
    scores = [(_dot_nt(qh[i].astype(bf16), kh[i].astype(bf16)) * mask[h]).astype(bf16)
              for i, (j, h) in enumerate(chains)]
    qd = [(qh[i] * q_decay[h]).astype(bf16) for i, (j, h) in enumerate(chains)]
    kd = [(kh[i] * k_decay[h]).astype(bf16) for i, (j, h) in enumerate(chains)]
    s_old = [s_scr[j, h] for j, h in chains]
    for i, (j, h) in enumerate(chains):
        o_scr[j * L:(j + 1) * L, RET_DV * h:RET_DV * (h + 1)] = (
            _dot(scores[i], vh[i]) + _dot(qd[i], s_old[i].astype(bf16)))
    for i, (j, h) in enumerate(chains):
        s_scr[j, h] = s_decay[h] * s_old[i] + _dot_tn(kd[i], vh[i])

    gn = gn_ref[...]
    sg = sg_ref[...].astype(f32).reshape(nb * L, RET_V)
    for h in range(RET_HEADS):
        sl = slice(RET_DV * h, RET_DV * (h + 1))
        o = o_scr[:, sl]
        mu = jnp.mean(o, axis=-1, keepdims=True)
        oc = o - mu
        var = jnp.mean(oc * oc, axis=-1, keepdims=True)
        out = oc * lax.rsqrt(var + RET_GN_EPS) * gn[:, sl] * sg[:, sl]
        o_ref[:, :, sl] = out.reshape(nb, L, RET_DV).astype(o_ref.dtype)

    @pl.when(c == pl.num_programs(1) - 1)
    def _():
        s_out_ref[...] = s_scr[...]


def _retention(q, k, v, sg, s0, gn, *, nb, rows, valid):
    B, T, _ = q.shape
    bcast = s0.shape[0] == 1

    def seq(width):
        return pl.BlockSpec((nb, rows, width), lambda b, c: (b, c, 0))

    state = pl.BlockSpec((nb, RET_HEADS, RET_DK, RET_DV), lambda b, c: (b, 0, 0, 0))
    state_in = pl.BlockSpec((1, RET_HEADS, RET_DK, RET_DV), lambda b, c: (0, 0, 0, 0)) if bcast else state
    return pl.pallas_call(
        functools.partial(_ret_kernel, nb=nb, rows=rows, valid=valid),
        grid=(B // nb, T // rows),
        in_specs=[seq(RET_QK), seq(RET_QK), seq(RET_V), seq(RET_V), state_in, _resident((1, RET_V))],
        out_specs=[seq(RET_V), state],
        out_shape=[jax.ShapeDtypeStruct((B, T, RET_V), bf16),
                   jax.ShapeDtypeStruct((B, RET_HEADS, RET_DK, RET_DV), f32)],
        scratch_shapes=[pltpu.VMEM((nb, RET_HEADS, RET_DK, RET_DV), f32), pltpu.VMEM((nb * rows, RET_V), f32)],
        compiler_params=pltpu.CompilerParams(dimension_semantics=("parallel", "arbitrary"),
                                             vmem_limit_bytes=V7X_VMEM_LIMIT_BYTES),
        name="retention",
    )(q, k, v, sg, s0, gn)


def _split3(x):
    hi = x.astype(bf16)
    r1 = x - hi.astype(f32)
    mid = r1.astype(bf16)
    lo = (r1 - mid.astype(f32)).astype(bf16)
    return hi, mid, lo


def _rwkv_features(p, p_prev, mu_ref, w0_ref, w2_ref, a0_ref, a2_ref, g2_ref, kk_ref, ka_ref, ones_bd):
    pm = p + (p_prev - p) * mu_ref[...]
    r = pm[:, 0:RWKV_W]
    k = pm[:, RWKV_W:2 * RWKV_W]
    v = pm[:, 2 * RWKV_W:3 * RWKV_W]
    o_w = 3 * RWKV_W
    xw = pm[:, o_w:o_w + DECAY_LORA]
    xa = pm[:, o_w + DECAY_LORA:o_w + DECAY_LORA + AAA_LORA]
    xg = pm[:, o_w + DECAY_LORA + AAA_LORA:SHIFT_W]
    z = w0_ref[...] + _dot(jnp.tanh(xw).astype(bf16), w2_ref[...])
    nz = -z
    softplus = jnp.maximum(nz, 0.0) + jnp.log(1.0 + jnp.exp(-jnp.abs(nz)))
    ld = -jnp.exp(-softplus - 0.5)
    a = jax.nn.sigmoid(a0_ref[...] + _dot(xa.astype(bf16), a2_ref[...]))
    g = _dot(jax.nn.sigmoid(xg).astype(bf16), g2_ref[...])
    kk = k * kk_ref[...]
    kk = kk * lax.rsqrt(jnp.maximum(_dot((kk * kk).astype(bf16), ones_bd), 1e-24))
    kp = k * (1.0 + (a - 1.0) * ka_ref[...])
    return r, kp, v, kk, a, ld, g


def _rwkv_output(y, bonus_rkv, g, lng_ref, lnb_ref, ones_bd):
    inv_n = 1.0 / RWKV_HD
    mean = _dot(y.astype(bf16), ones_bd) * inv_n
    yc = y - mean
    var = _dot((yc * yc).astype(bf16), ones_bd) * inv_n
    out = yc * lax.rsqrt(var + RWKV_GN_EPS) * lng_ref[...] + lnb_ref[...]
    rkr, v = bonus_rkv
    bonus = _dot(rkr.astype(bf16), ones_bd) * v
    return (out + bonus) * g


def _rwkv_kernel(p_ref, prev0_ref, h0_ref, mu_ref, w0_ref, w2_ref, a0_ref, a2_ref, g2_ref, kk_ref, ka_ref,
                 rk_ref, lng_ref, lnb_ref, ones_ref, o_ref, h_out_ref, shift_ref, s_scr, prev_scr, y_scr,
                 *, nb, rows, valid):
    c = pl.program_id(1)
    C = rows
    N = RWKV_HD
    R = nb * C
    assert C & (C - 1) == 0
    log2c = C.bit_length() - 1

    @pl.when(c == 0)
    def _():
        for j in range(nb):
            for h in range(RWKV_HEADS):
                s_scr[j, :, N * h:N * (h + 1)] = h0_ref[j if h0_ref.shape[0] > 1 else 0, h]
        prev_scr[...] = jnp.broadcast_to(prev0_ref[...], prev_scr.shape)

    p = p_ref[...].reshape(R, SHIFT_W)
    rowid = lax.broadcasted_iota(jnp.int32, (R, 1), 0)
    step = rowid & (C - 1)
    p_prev = pltpu.roll(p, 1, 0)
    for j in range(nb):
        p_prev = jnp.where(rowid == j * C, prev_scr[j], p_prev)
    for j in range(nb):
        prev_scr[j] = p[j * C + valid - 1:j * C + valid, :]
    ones_bd = ones_ref[...]
    r, kp, v, kk, a, ld, g = _rwkv_features(p, p_prev, mu_ref, w0_ref, w2_ref, a0_ref, a2_ref, g2_ref, kk_ref,
                                            ka_ref, ones_bd)
    if valid < C:
        live = (step < valid).astype(f32)
        ld = ld * live
        kk = kk * live
        kp = kp * live
        v = v * live
    b = kk * a

    ri = lax.broadcasted_iota(jnp.int32, (R, R), 0)
    rj = lax.broadcasted_iota(jnp.int32, (R, R), 1)
    tri = (((ri >> log2c) == (rj >> log2c)) & (ri >= rj)).astype(bf16)
    ld_hi, ld_mid, ld_lo = _split3(ld)
    cum = _dot(tri, ld_hi) + _dot(tri, ld_mid) + _dot(tri, ld_lo)
    last_rows = [cum[j * C + C - 1:j * C + C, :] for j in range(nb)]
    cum_last = jnp.concatenate([jnp.broadcast_to(x, (C, RWKV_W)) for x in last_rows], axis=0)
    e_in = jnp.exp(cum)
    e_ex = jnp.exp(cum - ld)
    e_neg = jnp.exp(-cum)
    e_end = jnp.exp(cum_last - cum)
    g_end = [jnp.exp(x) for x in last_rows]
    at = -kk * e_ex
    rt = r * e_in
    bt = (b * e_neg).astype(bf16)
    kt = (kp * e_neg).astype(bf16)
    bh = (b * e_end).astype(bf16)
    kh = (kp * e_end).astype(bf16)
    vb = v.astype(bf16)

    G = RWKV_LANE_HEADS
    GW = G * N
    log2n = N.bit_length() - 1
    def head_masks(width, log2_block):
        lane = lax.broadcasted_iota(jnp.int32, (1, width), 1)
        return [((lane >> log2_block) & (G - 1)) == h for h in range(G)]

    head_of_lane = head_masks(GW, log2n)
    head_of_lane2 = head_masks(2 * GW, log2n)
    head_of_col = head_masks(G * C, log2c)
    head_of_col2 = head_masks(2 * G * C, log2c)
    ti = lax.broadcasted_iota(jnp.int32, (C, G * C), 0)
    tj = lax.broadcasted_iota(jnp.int32, (C, G * C), 1) & (C - 1)
    incl = ti >= tj
    strict = ti > tj
    unit = ti == tj
    gi = lax.broadcasted_iota(jnp.int32, (GW, GW), 0)
    gj = lax.broadcasted_iota(jnp.int32, (GW, GW), 1)
    eye_g = gi == gj
    same_head = (gi >> log2n) == (gj >> log2n)

    def bd(a, masks):
        return jnp.concatenate([jnp.where(m, a, jnp.zeros_like(a)) for m in masks], axis=0)

    groups = [(j, q) for j in range(nb) for q in range(RWKV_HEADS // G)]
    n_gr = len(groups)

    def grp(x, j, q):
        return x[j * C:(j + 1) * C, GW * q:GW * (q + 1)]

    ar = [jnp.concatenate([grp(at, j, q), grp(rt, j, q)], axis=0).astype(bf16) for j, q in groups]
    gb = [_dot_nt(ar[i], bd(grp(bt, j, q), head_of_lane)) for i, (j, q) in enumerate(groups)]
    gk = [_dot_nt(ar[i], bd(grp(kt, j, q), head_of_lane)) for i, (j, q) in enumerate(groups)]
    lab = [jnp.where(strict, x[:C], 0.0) for x in gb]
    mrb = [jnp.where(incl, x[C:], 0.0).astype(bf16) for x in gb]
    lmk = [jnp.concatenate([jnp.where(strict, x[:C], 0.0), jnp.where(incl, x[C:], 0.0)], axis=0).astype(bf16)
           for x in gk]
    lmv = [_dot(lmk[i], bd(grp(vb, j, q), head_of_lane)) for i, (j, q) in enumerate(groups)]
    m = [_dot(x.astype(bf16), bd(x.astype(bf16), head_of_col)) for x in lab]
    t_inv = [jnp.where(unit, 1.0, x) for x in lab]
    for level in range(1, log2c):
        mb = [x.astype(bf16) for x in m]
        if level < log2c - 1:
            out = [_dot(mb[i], bd(jnp.concatenate([t_inv[i].astype(bf16), mb[i]], axis=1), head_of_col2))
                   for i in range(n_gr)]
            t_inv = [t_inv[i] + out[i][:, :G * C] for i in range(n_gr)]
            m = [x[:, G * C:] for x in out]
        else:
            t_inv = [t_inv[i] + _dot(mb[i], bd(t_inv[i].astype(bf16), head_of_col)) for i in range(n_gr)]
    wu = [_dot(t_inv[i].astype(bf16),
               bd(jnp.concatenate([grp(at, j, q).astype(bf16), lmv[i][:C].astype(bf16)], axis=1), head_of_lane2)
               ).astype(bf16) for i, (j, q) in enumerate(groups)]
    wa = [x[:, :GW] for x in wu]
    uv = [x[:, GW:] for x in wu]
    pq = [_dot(mrb[i], bd(wu[i], head_of_lane2)) for i in range(n_gr)]
    ry = [grp(rt, j, q) + pq[i][:, :GW] for i, (j, q) in enumerate(groups)]
    y0 = [lmv[i][C:] + pq[i][:, GW:] for i in range(n_gr)]
    gwt = [_dot_tn(wa[i], grp(bh, j, q)) for i, (j, q) in enumerate(groups)]
    dkf = [_dot_tn(jnp.concatenate([uv[i], grp(vb, j, q)], axis=0),
                   jnp.concatenate([grp(bh, j, q), grp(kh, j, q)], axis=0))
           for i, (j, q) in enumerate(groups)]
    for i, (j, q) in enumerate(groups):
        sl = slice(GW * q, GW * (q + 1))
        wg = jnp.where(same_head, gwt[i], 0.0) + jnp.where(eye_g, g_end[j][:, sl], 0.0)
        dk = jnp.where(head_of_lane[0], dkf[i][0:N], 0.0)
        for h in range(1, G):
            dk = dk + jnp.where(head_of_lane[h], dkf[i][N * h:N * (h + 1)], 0.0)
        s_old = s_scr[j, :, sl].astype(bf16)
        y_scr[j * C:(j + 1) * C, sl] = _dot_nt(ry[i].astype(bf16), bd(s_old, head_of_lane)) + y0[i]
        s_scr[j, :, sl] = _dot(s_old, wg.astype(bf16)) + dk

    out = _rwkv_output(y_scr[...], (r * kp * rk_ref[...], v), g, lng_ref, lnb_ref, ones_bd)
    o_ref[...] = out.reshape(nb, C, RWKV_W).astype(o_ref.dtype)

    @pl.when(c == pl.num_programs(1) - 1)
    def _():
        for j in range(nb):
            for h in range(RWKV_HEADS):
                h_out_ref[j, h] = s_scr[j, :, N * h:N * (h + 1)]
        shift_ref[...] = prev_scr[...]


def _rwkv(p, prev0, h0, params, ones_bd, *, nb, rows, valid):
    B, T, _ = p.shape

    def maybe_bcast(arr, tail):
        nd = len(tail) + 1
        if arr.shape[0] == 1:
            return pl.BlockSpec((1,) + tail, lambda b, c: (0,) * nd)
        return pl.BlockSpec((nb,) + tail, lambda b, c: (b,) + (0,) * (nd - 1))

    seq_in = pl.BlockSpec((nb, rows, SHIFT_W), lambda b, c: (b, c, 0))
    seq_out = pl.BlockSpec((nb, rows, RWKV_W), lambda b, c: (b, c, 0))
    st_tail = (RWKV_HEADS, RWKV_HD, RWKV_HD)
    sh_tail = (1, SHIFT_W)
    param_specs = [_resident(x.shape) for x in params]
    return pl.pallas_call(
        functools.partial(_rwkv_kernel, nb=nb, rows=rows, valid=valid),
        grid=(B // nb, T // rows),
        in_specs=[seq_in, maybe_bcast(prev0, sh_tail), maybe_bcast(h0, st_tail)] + param_specs
                 + [_resident(ones_bd.shape)],
        out_specs=[seq_out, pl.BlockSpec((nb,) + st_tail, lambda b, c: (b, 0, 0, 0)),
                   pl.BlockSpec((nb,) + sh_tail, lambda b, c: (b, 0, 0))],
        out_shape=[jax.ShapeDtypeStruct((B, T, RWKV_W), bf16),
                   jax.ShapeDtypeStruct((B,) + st_tail, f32),
                   jax.ShapeDtypeStruct((B,) + sh_tail, f32)],
        scratch_shapes=[pltpu.VMEM((nb, RWKV_HD, RWKV_W), f32), pltpu.VMEM((nb,) + sh_tail, f32),
                        pltpu.VMEM((nb * rows, RWKV_W), f32)],
        compiler_params=pltpu.CompilerParams(dimension_semantics=("parallel", "arbitrary"),
                                             vmem_limit_bytes=V7X_VMEM_LIMIT_BYTES),
        name="rwkv7",
    )(p, prev0, h0, *params, ones_bd)


def _rwkv_step_kernel(p_ref, shift0_ref, s_ref, mu_ref, w0_ref, w2_ref, a0_ref, a2_ref, g2_ref, kk_ref, ka_ref,
                      rk_ref, lng_ref, lnb_ref, ones_ref, o_ref, s_out_ref, shift_ref,
                      feat_scr, y_scr, g_scr, rkr_scr, v_scr, *, steps, batch):
    h = pl.program_id(0)
    T, B, N = steps, batch, RWKV_HD
    key_tiles = N // 8
    f_r, f_w, f_k, f_v, f_a, f_b = range(6)

    @pl.when(h == 0)
    def _():
        p = p_ref[...]
        p_prev = jnp.concatenate([shift0_ref[...], p[:(T - 1) * B]], axis=0)
        r, kp, v, kk, a, ld, g = _rwkv_features(p, p_prev, mu_ref, w0_ref, w2_ref, a0_ref, a2_ref, g2_ref,
                                                kk_ref, ka_ref, ones_ref[...])
        for idx, x in enumerate((r, jnp.exp(ld), kp, v, -kk, kk * a)):
            for t in range(T):
                feat_scr[idx, t] = x[t * B:(t + 1) * B, :].T
        g_scr[...] = g
        rkr_scr[...] = r * kp * rk_ref[...]
        v_scr[...] = v
        shift_ref[...] = p[(T - 1) * B:]

    base = pl.multiple_of(h * N, N)

    def tile(idx, t, kt):
        return feat_scr[idx, t, pl.ds(pl.multiple_of(base + 8 * kt, 8), 8), :]

    def keysum(s, idx, t):
        acc = s[0] * tile(idx, t, 0)
        for kt in range(1, key_tiles):
            acc = acc + s[kt] * tile(idx, t, kt)
        return jnp.sum(acc, axis=0, keepdims=True)

    def value_group(vg, carry):
        v_rows = pl.ds(pl.multiple_of(base + 8 * vg, 8), 8)
        v_tiles = [feat_scr[f_v, t, v_rows, :] for t in range(T)]
        y_rows = [[] for _ in range(T)]
        for i in range(8):
            row0 = pl.multiple_of((8 * vg + i) * N, N)
            s = [s_ref[0, pl.ds(row0 + 8 * kt, 8), :] for kt in range(key_tiles)]
            for t in range(T):
                sa = jnp.broadcast_to(keysum(s, f_a, t), (8, B))
                vv = jnp.broadcast_to(v_tiles[t][i:i + 1, :], (8, B))
                s = [s[kt] * tile(f_w, t, kt) + sa * tile(f_b, t, kt) + vv * tile(f_k, t, kt)
                     for kt in range(key_tiles)]
                y_rows[t].append(keysum(s, f_r, t))
            for kt in range(key_tiles):
                s_out_ref[0, pl.ds(row0 + 8 * kt, 8), :] = s[kt]
        for t in range(T):
            y_scr[t, v_rows, :] = jnp.concatenate(y_rows[t], axis=0)
        return carry

    lax.fori_loop(0, N // 8, value_group, 0)

    @pl.when(h == pl.num_programs(0) - 1)
    def _():
        y = jnp.concatenate([y_scr[t].T for t in range(T)], axis=0)
        out = _rwkv_output(y, (rkr_scr[...], v_scr[...]), g_scr[...], lng_ref, lnb_ref, ones_ref[...])
        o_ref[...] = out.astype(o_ref.dtype)


def _rwkv_step(p, shift0, state, params, ones_bd, *, steps):
    B = state.shape[0]
    rows = steps * B
    n_state = RWKV_HD * RWKV_HD
    s_in = jnp.transpose(state, (1, 2, 3, 0)).reshape(RWKV_HEADS, n_state, B)
    s_spec = pl.BlockSpec((1, n_state, B), lambda h: (h, 0, 0))
    o, s_out, shift = pl.pallas_call(
        functools.partial(_rwkv_step_kernel, steps=steps, batch=B),
        grid=(RWKV_HEADS,),
        in_specs=[_resident((rows, SHIFT_W)), _resident((B, SHIFT_W)), s_spec]
                 + [_resident(x.shape) for x in params] + [_resident(ones_bd.shape)],
        out_specs=[pl.BlockSpec((rows, RWKV_W), lambda h: (0, 0)), s_spec,
                   pl.BlockSpec((B, SHIFT_W), lambda h: (0, 0))],
        out_shape=[jax.ShapeDtypeStruct((rows, RWKV_W), bf16),
                   jax.ShapeDtypeStruct((RWKV_HEADS, n_state, B), f32),
                   jax.ShapeDtypeStruct((B, SHIFT_W), f32)],
        scratch_shapes=[pltpu.VMEM((6, steps, RWKV_W, B), f32), pltpu.VMEM((steps, RWKV_W, B), f32),
                        pltpu.VMEM((rows, RWKV_W), f32), pltpu.VMEM((rows, RWKV_W), f32),
                        pltpu.VMEM((rows, RWKV_W), f32)],
        compiler_params=pltpu.CompilerParams(dimension_semantics=("arbitrary",),
                                             vmem_limit_bytes=V7X_VMEM_LIMIT_BYTES),
        name="rwkv7_step",
    )(p, shift0, s_in, *params, ones_bd)
    s_out = jnp.transpose(s_out.reshape(RWKV_HEADS, RWKV_HD, RWKV_HD, B), (3, 0, 1, 2))
    return o, s_out, shift


def _merge_ffn_kernel(h_ref, oret_ref, orwkv_ref, gate_ref, wr_ref, ww_ref, wo_ref,
                      g_ref, wg_ref, wu_ref, wd_ref, fin_ref, o_ref):
    a = _dot(oret_ref[...], wr_ref[...])
    b = _dot(orwkv_ref[...], ww_ref[...])
    merged = gate_ref[:, :D_MODEL] * a + gate_ref[:, D_MODEL:] * b
    h = h_ref[...] + _dot(merged.astype(bf16), wo_ref[...])
    h = _swiglu_half_step(h, g_ref, _column_chunks(wg_ref), _column_chunks(wu_ref), _row_chunks(wd_ref))
    o_ref[...] = _rms(h, fin_ref[...])


def _merge_ffn(h, o_ret, o_rwkv, gates, w_out_ret, w_out_rwkv, w_out, norm_g, wg, wu, wd, fin_g, *, tm):
    rows = h.shape[0]

    def rowspec(width):
        return pl.BlockSpec((tm, width), lambda i: (i, 0))

    return pl.pallas_call(
        _merge_ffn_kernel,
        grid=(rows // tm,),
        in_specs=[rowspec(D_MODEL), rowspec(RET_V), rowspec(RWKV_W), rowspec(GATE_W),
                  _resident((RET_V, D_MODEL)), _resident((RWKV_W, D_MODEL)), _resident((D_MODEL, D_MODEL))]
                 + [_resident(s) for s in _FFN_WEIGHT_SPECS] + [_resident((1, D_MODEL))],
        out_specs=rowspec(D_MODEL),
        out_shape=jax.ShapeDtypeStruct((rows, D_MODEL), f32),
        compiler_params=pltpu.CompilerParams(dimension_semantics=("parallel",),
                                             vmem_limit_bytes=V7X_VMEM_LIMIT_BYTES),
        name="merge_ffn",
    )(h, o_ret, o_rwkv, gates, w_out_ret, w_out_rwkv, w_out, norm_g, wg, wu, wd, fin_g)


def _rotary_tables(pos):
    half = RET_DK // 2
    inv_freq = ROPE_BASE ** (-jnp.arange(half, dtype=f32) / half)
    ang = pos.astype(f32)[:, None] * inv_freq[None, :]
    cos = jnp.cos(ang)
    sin = jnp.sin(ang)
    cos_t = jnp.tile(jnp.concatenate([cos, cos], axis=1), (1, RET_HEADS))
    sin_t = jnp.tile(jnp.concatenate([-sin, sin], axis=1), (1, RET_HEADS))
    return cos_t, sin_t


def _row_tile(rows, target):
    tm = min(rows, target)
    while rows % tm:
        tm -= 8
    return tm


def kernel(x_prompt, x_sample, state_ret, state_wkv, state_shift, meta_tokens, ffn1_norm, ffn1_w_gate, ffn1_w_up, ffn1_w_down, mix_norm, w_in, ret_gn_g, mu_shift, w0, w2, a0, a2, g2, k_k, k_a, r_k, lnx_g, lnx_b, w_out_ret, w_out_rwkv, w_out, ffn2_norm, ffn2_w_gate, ffn2_w_up, ffn2_w_down, final_norm):
    Bp, Tp, _ = x_prompt.shape
    Bs, Ts, _ = x_sample.shape
    st_dtype = state_ret.dtype

    def row(x):
        return x.reshape(1, -1).astype(f32)

    fin = row(final_norm)
    mix_g = row(mix_norm)
    wr_b, ww_b, wo_b = w_out_ret.astype(bf16), w_out_rwkv.astype(bf16), w_out.astype(bf16)
    rwkv_params = (row(mu_shift), row(w0), w2.astype(bf16), row(a0), a2.astype(bf16), g2.astype(bf16),
                   row(k_k), row(k_a), row(r_k), row(lnx_g), row(lnx_b))
    head_id = jnp.arange(RWKV_W, dtype=jnp.int32) // RWKV_HD
    ones_bd = (head_id[:, None] == head_id[None, :]).astype(bf16)
    gn = row(ret_gn_g)

    n_p = Bp * Tp
    h_p, *ffn1_b, w_in_b = _ffn_cast(x_prompt.reshape(n_p, D_MODEL), row(ffn1_norm), ffn1_w_gate, ffn1_w_up,
                                     ffn1_w_down, w_in, tm=_row_tile(n_p, 512))

    n_s = Bs * Ts
    x_small = jnp.concatenate([jnp.swapaxes(x_sample, 0, 1).reshape(n_s, D_MODEL),
                               meta_tokens.astype(x_sample.dtype)], axis=0)
    cos_s, sin_s = _rotary_tables(PAST_LEN + jnp.arange(Ts, dtype=jnp.int32))
    cos_m, sin_m = _rotary_tables(jnp.arange(N_META, dtype=jnp.int32))
    cos_small = jnp.concatenate([jnp.repeat(cos_s, Bs, axis=0), cos_m], axis=0)
    sin_small = jnp.concatenate([jnp.repeat(sin_s, Bs, axis=0), sin_m], axis=0)
    n_small = n_s + N_META
    h_small = _ffn(x_small, row(ffn1_norm), *ffn1_b, tm=n_small)
    q_s, k_s, v_s, sg_s, p_s, gate_s = _inproj(h_small, mix_g, w_in_b, cos_small, sin_small, tm=n_small)

    def meta(x):
        return x[n_s:].reshape(1, N_META, -1)

    zeros_ret = jnp.zeros((1, RET_HEADS, RET_DK, RET_DV), f32)
    _, s_ret_m = _retention(meta(q_s), meta(k_s), meta(v_s), meta(sg_s), zeros_ret, gn,
                            nb=1, rows=N_META, valid=N_META)
    _, h_wkv_m, shift_m = _rwkv(meta(p_s), jnp.zeros((1, 1, SHIFT_W), f32),
                                jnp.zeros((1, RWKV_HEADS, RWKV_HD, RWKV_HD), f32), rwkv_params, ones_bd,
                                nb=1, rows=N_META, valid=N_META)

    pad_t = -(-Ts // 8) * 8

    def samp(x):
        x = jnp.swapaxes(x[:n_s].reshape(Ts, Bs, -1), 0, 1)
        return jnp.pad(x, ((0, 0), (0, pad_t - Ts), (0, 0)))

    o_ret_s, s_ret_s = _retention(samp(q_s), samp(k_s), samp(v_s), samp(sg_s), state_ret.astype(f32), gn,
                                  nb=RET_SAMPLE_SEQS, rows=pad_t, valid=Ts)
    o_ret_s = jnp.swapaxes(o_ret_s[:, :Ts], 0, 1).reshape(n_s, RET_V)
    o_wkv_s, h_wkv_s, shift_s = _rwkv_step(p_s[:n_s], state_shift.astype(f32), state_wkv.astype(f32),
                                           rwkv_params, ones_bd, steps=Ts)

    cos_p, sin_p = _rotary_tables(N_META + jnp.arange(Tp, dtype=jnp.int32))
    q_p, k_p, v_p, sg_p, p_p, gate_p, *ffn2_b = _inproj(h_p, mix_g, w_in_b, cos_p, sin_p, tm=_row_tile(Tp, 512),
                                                        cast_ffn=(ffn2_w_gate, ffn2_w_up, ffn2_w_down))

    def post(h, o_ret, o_rwkv, gates, tm):
        return _merge_ffn(h, o_ret, o_rwkv, gates, wr_b, ww_b, wo_b, row(ffn2_norm), *ffn2_b, fin, tm=tm)

    y_sample = post(h_small[:n_s], o_ret_s, o_wkv_s, gate_s[:n_s], _row_tile(n_s, 512))
    y_sample = jnp.swapaxes(y_sample.reshape(Ts, Bs, D_MODEL), 0, 1)

    def seqs(x):
        return x.reshape(Bp, Tp, -1)

    o_ret_p, s_ret_p = _retention(seqs(q_p), seqs(k_p), seqs(v_p), seqs(sg_p), s_ret_m, gn,
                                  nb=RET_PROMPT_SEQS, rows=RET_CHUNK, valid=RET_CHUNK)
    o_wkv_p, h_wkv_p, shift_p = _rwkv(seqs(p_p), shift_m, h_wkv_m, rwkv_params, ones_bd,
                                      nb=RWKV_PROMPT_SEQS, rows=RWKV_CHUNK, valid=RWKV_CHUNK)
    y_prompt = post(h_p, o_ret_p.reshape(n_p, RET_V), o_wkv_p.reshape(n_p, RWKV_W), gate_p, _row_tile(n_p, 512))

    return (y_prompt.reshape(Bp, Tp, D_MODEL).astype(x_prompt.dtype),
            y_sample.reshape(Bs, Ts, D_MODEL).astype(x_sample.dtype),
            s_ret_p.astype(st_dtype), h_wkv_p.astype(st_dtype),
            shift_p.reshape(Bp, SHIFT_W).astype(st_dtype),
            s_ret_s.astype(st_dtype), h_wkv_s.astype(st_dtype),
            shift_s.reshape(Bs, SHIFT_W).astype(st_dtype))
```

```python
import functools

import numpy as np
import jax
import jax.numpy as jnp
from jax import lax
from jax.experimental import pallas as pl
from jax.experimental.pallas import tpu as pltpu

D_MODEL = 1024
N_META = 16
PAST_LEN = 16384
RET_HEADS = 4
RET_DK = 64
RET_DV = 128
RET_CHUNK = 128
RWKV_HEADS = 8
RWKV_HD = 64
RWKV_W = RWKV_HEADS * RWKV_HD
DECAY_LORA = 64
AAA_LORA = 64
GATE_LORA = 128
D_FF = 2816
ROPE_BASE = 10000.0
NORM_EPS = 1e-6
RET_GN_EPS = 1e-6
RWKV_GN_EPS = 64e-5
RET_QK = RET_HEADS * RET_DK
RET_V = RET_HEADS * RET_DV
SHIFT_W = 3 * RWKV_W + DECAY_LORA + AAA_LORA + GATE_LORA
GATE_W = 2 * D_MODEL
PROJ_W = 2 * RET_QK + 2 * RET_V + SHIFT_W + GATE_W

_C_Q, _C_K, _C_V, _C_G = 0, RET_QK, 2 * RET_QK, 2 * RET_QK + RET_V
_C_P = 2 * RET_QK + 2 * RET_V
_C_GATE = _C_P + SHIFT_W

V7X_VMEM_LIMIT_BYTES = 56 * 1024 * 1024
FF_CHUNK = 256
RWKV_CHUNK = 64
RWKV_LANE_HEADS = 2
RWKV_PROMPT_SEQS = 8
RET_PROMPT_SEQS = 8
RET_SAMPLE_SEQS = 32
LOG_GAMMA = tuple(float(np.log1p(-2.0 ** (-5.0 - h))) for h in range(RET_HEADS))

f32 = jnp.float32
bf16 = jnp.bfloat16


def _resident(shape):
    zeros = (0,) * len(shape)
    return pl.BlockSpec(shape, lambda *_: zeros, pipeline_mode=pl.Buffered(1))


def _rms(x, g):
    return x * lax.rsqrt(jnp.mean(x * x, axis=-1, keepdims=True) + NORM_EPS) * g


def _dot(a, b):
    return jnp.dot(a, b, preferred_element_type=f32)


def _dot_nt(a, b):
    return lax.dot_general(a, b, (((1,), (1,)), ((), ())), preferred_element_type=f32)


def _dot_tn(a, b):
    return lax.dot_general(a, b, (((0,), (0,)), ((), ())), preferred_element_type=f32)


FF_CHUNKS = D_FF // FF_CHUNK


def _swiglu_half_step(x, g_ref, gate, up, down):
    xn = _rms(x, g_ref[...]).astype(bf16)
    acc = jnp.zeros(x.shape, f32)
    for c in range(FF_CHUNKS):
        gt = _dot(xn, gate(c))
        ut = _dot(xn, up(c))
        act = (gt * jax.nn.sigmoid(gt) * ut).astype(bf16)
        acc = acc + _dot(act, down(c))
    return x + 0.5 * acc


def _column_chunks(w_ref):
    return lambda c: w_ref[:, c * FF_CHUNK:(c + 1) * FF_CHUNK]


def _row_chunks(w_ref):
    return lambda c: w_ref[c * FF_CHUNK:(c + 1) * FF_CHUNK, :]


def _ffn_kernel(x_ref, g_ref, wg_ref, wu_ref, wd_ref, o_ref):
    o_ref[...] = _swiglu_half_step(x_ref[...], g_ref, _column_chunks(wg_ref), _column_chunks(wu_ref),
                                   _row_chunks(wd_ref))


_FFN_WEIGHT_SPECS = ((1, D_MODEL), (D_MODEL, D_FF), (D_MODEL, D_FF), (D_FF, D_MODEL))


def _ffn_chunk_specs(step_of):
    col = pl.BlockSpec((D_MODEL, FF_CHUNK), lambda i: (0, step_of(i)))
    return [col, col, pl.BlockSpec((FF_CHUNK, D_MODEL), lambda i: (step_of(i), 0))]


def _ffn_bf16_shapes():
    return [jax.ShapeDtypeStruct(s, bf16) for s in _FFN_WEIGHT_SPECS[1:]]


def _ffn_cast_kernel(x_ref, g_ref, wg_f, wu_f, wd_f, win_f, o_ref, wg_o, wu_o, wd_o, win_o, wg_s, wu_s, wd_s):
    i = pl.program_id(0)

    @pl.when(i < FF_CHUNKS)
    def _():
        for w_f, w_s, w_o in ((wg_f, wg_s, wg_o), (wu_f, wu_s, wu_o), (wd_f, wd_s, wd_o)):
            w = w_f[...].astype(bf16)
            w_s[i] = w
            w_o[...] = w

    @pl.when(i >= FF_CHUNKS)
    def _():
        o_ref[...] = _swiglu_half_step(x_ref[...], g_ref, lambda c: wg_s[c], lambda c: wu_s[c], lambda c: wd_s[c])
        win_o[...] = win_f[...].astype(bf16)


def _ffn_cast(x, norm_g, wg, wu, wd, w_in, *, tm):
    rows = x.shape[0]
    tiles = rows // tm
    in_chunks = PROJ_W // FF_CHUNK
    assert in_chunks <= tiles
    row = pl.BlockSpec((tm, D_MODEL), lambda i: (jnp.maximum(i - FF_CHUNKS, 0), 0))
    win = pl.BlockSpec((D_MODEL, FF_CHUNK), lambda i: (0, jnp.clip(i - FF_CHUNKS, 0, in_chunks - 1)))
    weight_chunks = _ffn_chunk_specs(lambda i: jnp.minimum(i, FF_CHUNKS - 1))
    return pl.pallas_call(
        _ffn_cast_kernel,
        grid=(FF_CHUNKS + tiles,),
        in_specs=[row, _resident((1, D_MODEL))] + weight_chunks + [win],
        out_specs=[row] + weight_chunks + [win],
        out_shape=[jax.ShapeDtypeStruct((rows, D_MODEL), f32)] + _ffn_bf16_shapes()
                  + [jax.ShapeDtypeStruct((D_MODEL, PROJ_W), bf16)],
        scratch_shapes=[pltpu.VMEM((FF_CHUNKS, D_MODEL, FF_CHUNK), bf16), pltpu.VMEM((FF_CHUNKS, D_MODEL, FF_CHUNK), bf16),
                        pltpu.VMEM((FF_CHUNKS, FF_CHUNK, D_MODEL), bf16)],
        compiler_params=pltpu.CompilerParams(dimension_semantics=("arbitrary",),
                                             vmem_limit_bytes=V7X_VMEM_LIMIT_BYTES),
        name="ffn_cast",
    )(x, norm_g, wg, wu, wd, w_in)


def _ffn(x, norm_g, wg, wu, wd, *, tm):
    rows = x.shape[0]
    row = pl.BlockSpec((tm, D_MODEL), lambda i: (i, 0))
    return pl.pallas_call(
        _ffn_kernel,
        grid=(rows // tm,),
        in_specs=[row] + [_resident(s) for s in _FFN_WEIGHT_SPECS],
        out_specs=row,
        out_shape=jax.ShapeDtypeStruct((rows, D_MODEL), f32),
        compiler_params=pltpu.CompilerParams(dimension_semantics=("parallel",),
                                             vmem_limit_bytes=V7X_VMEM_LIMIT_BYTES),
        name="ffn",
    )(x, norm_g, wg, wu, wd)


def _swap_halves(x):
    parts = []
    for j in range(x.shape[1] // 128):
        xs = x[:, 128 * j:128 * (j + 1)]
        fwd = pltpu.roll(xs, 32, 1)
        bwd = pltpu.roll(xs, 96, 1)
        lane = lax.broadcasted_iota(jnp.int32, xs.shape, 1)
        parts.append(jnp.where((lane % RET_DK) < RET_DK // 2, bwd, fwd))
    return jnp.concatenate(parts, axis=1)


def _inproj_cast_kernel(h_ref, g_ref, w_ref, cos_ref, sin_ref, wg_f, wu_f, wd_f,
                        q_ref, k_ref, v_ref, sg_ref, p_ref, gate_ref, wg_o, wu_o, wd_o):
    _inproj_kernel(h_ref, g_ref, w_ref, cos_ref, sin_ref, q_ref, k_ref, v_ref, sg_ref, p_ref, gate_ref)

    @pl.when(pl.program_id(0) < FF_CHUNKS)
    def _():
        for w_f, w_o in ((wg_f, wg_o), (wu_f, wu_o), (wd_f, wd_o)):
            w_o[...] = w_f[...].astype(bf16)


def _inproj_kernel(h_ref, g_ref, w_ref, cos_ref, sin_ref, q_ref, k_ref, v_ref, sg_ref, p_ref, gate_ref):
    un = _rms(h_ref[...], g_ref[...]).astype(bf16)
    gate_ref[...] = jax.nn.sigmoid(_dot(un, w_ref[:, _C_GATE:PROJ_W]))
    gr = _dot(un, w_ref[:, _C_G:_C_P])
    sg_ref[...] = gr * jax.nn.sigmoid(gr)
    cos = cos_ref[...]
    sin = sin_ref[...]
    q = _dot(un, w_ref[:, _C_Q:_C_K])
    q_ref[...] = q * cos + _swap_halves(q) * sin
    k = _dot(un, w_ref[:, _C_K:_C_V])
    k_ref[...] = (k * cos + _swap_halves(k) * sin) * (RET_DK ** -0.5)
    v_ref[...] = _dot(un, w_ref[:, _C_V:_C_G]).astype(v_ref.dtype)
    p_ref[...] = _dot(un, w_ref[:, _C_P:_C_GATE])


def _inproj(h, norm_g, w_in, cos, sin, *, tm, cast_ffn=None):
    rows = h.shape[0]
    tiles = rows // tm
    tab_blocks = cos.shape[0] // tm

    def rowspec(width):
        return pl.BlockSpec((tm, width), lambda i: (i, 0))

    tab = pl.BlockSpec((tm, RET_QK), lambda i: (i % tab_blocks, 0))
    widths = (RET_QK, RET_QK, RET_V, RET_V, SHIFT_W, GATE_W)
    dtypes = (f32, f32, bf16, f32, f32, f32)
    in_specs = [rowspec(D_MODEL), _resident((1, D_MODEL)), _resident((D_MODEL, PROJ_W)), tab, tab]
    out_specs = [rowspec(w) for w in widths]
    out_shape = [jax.ShapeDtypeStruct((rows, w), d) for w, d in zip(widths, dtypes)]
    operands = (h, norm_g, w_in, cos, sin)
    if cast_ffn is not None:
        assert FF_CHUNKS <= tiles
        weight_chunks = _ffn_chunk_specs(lambda i: jnp.minimum(i, FF_CHUNKS - 1))
        in_specs += weight_chunks
        out_specs += weight_chunks
        out_shape += _ffn_bf16_shapes()
        operands += tuple(cast_ffn)
    return pl.pallas_call(
        _inproj_kernel if cast_ffn is None else _inproj_cast_kernel,
        grid=(tiles,),
        in_specs=in_specs,
        out_specs=out_specs,
        out_shape=out_shape,
        compiler_params=pltpu.CompilerParams(dimension_semantics=("arbitrary",),
                                             vmem_limit_bytes=V7X_VMEM_LIMIT_BYTES),
        name="inproj",
    )(*operands)


def _ret_kernel(q_ref, k_ref, v_ref, sg_ref, s0_ref, gn_ref, o_ref, s_out_ref, s_scr, o_scr, *, nb, rows, valid):
    c = pl.program_id(1)
    L = rows

    @pl.when(c == 0)
    def _():
        s_scr[...] = jnp.broadcast_to(s0_ref[...], s_scr.shape)

    ii = lax.broadcasted_iota(jnp.int32, (L, L), 0)
    jj = lax.broadcasted_iota(jnp.int32, (L, L), 1)
    diff = (ii - jj).astype(f32)
    row = lax.broadcasted_iota(jnp.int32, (L, 1), 0).astype(f32)
    mask = [jnp.where(diff >= 0, jnp.exp(lg * jnp.maximum(diff, 0.0)), 0.0) for lg in LOG_GAMMA]
    q_decay = [jnp.exp(lg * (row + 1.0)) for lg in LOG_GAMMA]
    k_decay = [jnp.exp(lg * (valid - 1.0 - row)) for lg in LOG_GAMMA]
    s_decay = [float(np.exp(lg * valid)) for lg in LOG_GAMMA]

    chains = [(j, h) for j in range(nb) for h in range(RET_HEADS)]
    qh = [q_ref[j, :, RET_DK * h:RET_DK * (h + 1)] for j, h in chains]
    kh = [k_ref[j, :, RET_DK * h:RET_DK * (h + 1)] for j, h in chains]
    vh = [v_ref[j, :, RET_DV * h:RET_DV * (h + 1)] for j, h in chains]
    scores = [(_dot_nt(qh[i].astype(bf16), kh[i].astype(bf16)) * mask[h]).astype(bf16)
              for i, (j, h) in enumerate(chains)]
    qd = [(qh[i] * q_decay[h]).astype(bf16) for i, (j, h) in enumerate(chains)]
    kd = [(kh[i] * k_decay[h]).astype(bf16) for i, (j, h) in enumerate(chains)]
    s_old = [s_scr[j, h] for j, h in chains]
    for i, (j, h) in enumerate(chains):
        o_scr[j * L:(j + 1) * L, RET_DV * h:RET_DV * (h + 1)] = (
            _dot(scores[i], vh[i]) + _dot(qd[i], s_old[i].astype(bf16)))
    for i, (j, h) in enumerate(chains):
        s_scr[j, h] = s_decay[h] * s_old[i] + _dot_tn(kd[i], vh[i])

    gn = gn_ref[...]
    sg = sg_ref[...].reshape(nb * L, RET_V)
    for h in range(RET_HEADS):
        sl = slice(RET_DV * h, RET_DV * (h + 1))
        o = o_scr[:, sl]
        mu = jnp.mean(o, axis=-1, keepdims=True)
        oc = o - mu
        var = jnp.mean(oc * oc, axis=-1, keepdims=True)
        out = oc * lax.rsqrt(var + RET_GN_EPS) * gn[:, sl] * sg[:, sl]
        o_ref[:, :, sl] = out.reshape(nb, L, RET_DV).astype(o_ref.dtype)

    @pl.when(c == pl.num_programs(1) - 1)
    def _():
        s_out_ref[...] = s_scr[...]


def _retention(q, k, v, sg, s0, gn, *, nb, rows, valid):
    B, T, _ = q.shape
    bcast = s0.shape[0] == 1

    def seq(width):
        return pl.BlockSpec((nb, rows, width), lambda b, c: (b, c, 0))

    state = pl.BlockSpec((nb, RET_HEADS, RET_DK, RET_DV), lambda b, c: (b, 0, 0, 0))
    state_in = pl.BlockSpec((1, RET_HEADS, RET_DK, RET_DV), lambda b, c: (0, 0, 0, 0)) if bcast else state
    return pl.pallas_call(
        functools.partial(_ret_kernel, nb=nb, rows=rows, valid=valid),
        grid=(B // nb, T // rows),
        in_specs=[seq(RET_QK), seq(RET_QK), seq(RET_V), seq(RET_V), state_in, _resident((1, RET_V))],
        out_specs=[seq(RET_V), state],
        out_shape=[jax.ShapeDtypeStruct((B, T, RET_V), bf16),
                   jax.ShapeDtypeStruct((B, RET_HEADS, RET_DK, RET_DV), f32)],
        scratch_shapes=[pltpu.VMEM((nb, RET_HEADS, RET_DK, RET_DV), f32), pltpu.VMEM((nb * rows, RET_V), f32)],
        compiler_params=pltpu.CompilerParams(dimension_semantics=("parallel", "arbitrary"),
                                             vmem_limit_bytes=V7X_VMEM_LIMIT_BYTES),
        name="retention",
    )(q, k, v, sg, s0, gn)


def _rwkv_features(p, p_prev, mu_ref, w0_ref, w2_ref, a0_ref, a2_ref, g2_ref, kk_ref, ka_ref, ones_bd):
    pm = p + (p_prev - p) * mu_ref[...]
    r = pm[:, 0:RWKV_W]
    k = pm[:, RWKV_W:2 * RWKV_W]
    v = pm[:, 2 * RWKV_W:3 * RWKV_W]
    o_w = 3 * RWKV_W
    xw = pm[:, o_w:o_w + DECAY_LORA]
    xa = pm[:, o_w + DECAY_LORA:o_w + DECAY_LORA + AAA_LORA]
    xg = pm[:, o_w + DECAY_LORA + AAA_LORA:SHIFT_W]
    z = w0_ref[...] + _dot(jnp.tanh(xw).astype(bf16), w2_ref[...])
    ld = -float(np.exp(-0.5)) * jax.nn.sigmoid(z)
    a = jax.nn.sigmoid(a0_ref[...] + _dot(xa.astype(bf16), a2_ref[...]))
    g = _dot(jax.nn.sigmoid(xg).astype(bf16), g2_ref[...])
    kk = k * kk_ref[...]
    kk = kk * lax.rsqrt(jnp.maximum(_dot((kk * kk).astype(bf16), ones_bd), 1e-24))
    kp = k * (1.0 + (a - 1.0) * ka_ref[...])
    return r, kp, v, kk, a, ld, g


def _rwkv_output(y, bonus_rkv, g, lng_ref, lnb_ref, ones_bd):
    inv_n = 1.0 / RWKV_HD
    mean = _dot(y.astype(bf16), ones_bd) * inv_n
    yc = y - mean
    var = _dot((yc * yc).astype(bf16), ones_bd) * inv_n
    out = yc * lax.rsqrt(var + RWKV_GN_EPS) * lng_ref[...] + lnb_ref[...]
    rkr, v = bonus_rkv
    bonus = _dot(rkr.astype(bf16), ones_bd) * v
    return (out + bonus) * g


def _rwkv_kernel(p_ref, prev0_ref, h0_ref, mu_ref, w0_ref, w2_ref, a0_ref, a2_ref, g2_ref, kk_ref, ka_ref,
                 rk_ref, lng_ref, lnb_ref, ones_ref, o_ref, h_out_ref, shift_ref, s_scr, prev_scr, y_scr,
                 *, nb, rows, valid):
    c = pl.program_id(1)
    C = rows
    N = RWKV_HD
    R = nb * C
    assert C & (C - 1) == 0
    log2c = C.bit_length() - 1

    @pl.when(c == 0)
    def _():
        for j in range(nb):
            for h in range(RWKV_HEADS):
                s_scr[j, :, N * h:N * (h + 1)] = h0_ref[j if h0_ref.shape[0] > 1 else 0, h]
        prev_scr[...] = jnp.broadcast_to(prev0_ref[...], prev_scr.shape)

    p = p_ref[...].reshape(R, SHIFT_W)
    rowid = lax.broadcasted_iota(jnp.int32, (R, 1), 0)
    step = rowid & (C - 1)
    p_prev = pltpu.roll(p, 1, 0)
    for j in range(nb):
        p_prev = jnp.where(rowid == j * C, prev_scr[j], p_prev)
    for j in range(nb):
        prev_scr[j] = p[j * C + valid - 1:j * C + valid, :]
    ones_bd = ones_ref[...]
    r, kp, v, kk, a, ld, g = _rwkv_features(p, p_prev, mu_ref, w0_ref, w2_ref, a0_ref, a2_ref, g2_ref, kk_ref,
                                            ka_ref, ones_bd)
    if valid < C:
        live = (step < valid).astype(f32)
        ld = ld * live
        kk = kk * live
        kp = kp * live
        v = v * live
    b = kk * a

    cum = ld
    for level in range(log2c):
        reach = 1 << level
        cum = cum + jnp.where(step >= reach, pltpu.roll(cum, reach, 0), 0.0)
    last_rows = [cum[j * C + C - 1:j * C + C, :] for j in range(nb)]
    cum_last = jnp.concatenate([jnp.broadcast_to(x, (C, RWKV_W)) for x in last_rows], axis=0)
    e_in = jnp.exp(cum)
    e_ex = jnp.exp(cum - ld)
    e_neg = jnp.exp(-cum)
    e_end = jnp.exp(cum_last - cum)
    g_end = [jnp.exp(x) for x in last_rows]
    at = -kk * e_ex
    rt = r * e_in
    bt = (b * e_neg).astype(bf16)
    kt = (kp * e_neg).astype(bf16)
    bh = (b * e_end).astype(bf16)
    kh = (kp * e_end).astype(bf16)
    vb = v.astype(bf16)

    G = RWKV_LANE_HEADS
    GW = G * N
    log2n = N.bit_length() - 1
    def head_masks(width, log2_block):
        lane = lax.broadcasted_iota(jnp.int32, (1, width), 1)
        return [((lane >> log2_block) & (G - 1)) == h for h in range(G)]

    head_of_lane = head_masks(GW, log2n)
    head_of_lane2 = head_masks(2 * GW, log2n)
    head_of_col = head_masks(G * C, log2c)
    head_of_col2 = head_masks(2 * G * C, log2c)
    ti = lax.broadcasted_iota(jnp.int32, (C, G * C), 0)
    tj = lax.broadcasted_iota(jnp.int32, (C, G * C), 1) & (C - 1)
    incl = ti >= tj
    strict = ti > tj
    unit = ti == tj
    gi = lax.broadcasted_iota(jnp.int32, (GW, GW), 0)
    gj = lax.broadcasted_iota(jnp.int32, (GW, GW), 1)
    eye_g = gi == gj
    same_head = (gi >> log2n) == (gj >> log2n)

    def bd(a, masks):
        return jnp.concatenate([jnp.where(m, a, jnp.zeros_like(a)) for m in masks], axis=0)

    groups = [(j, q) for j in range(nb) for q in range(RWKV_HEADS // G)]
    n_gr = len(groups)

    def grp(x, j, q):
        return x[j * C:(j + 1) * C, GW * q:GW * (q + 1)]

    ar = [jnp.concatenate([grp(at, j, q), grp(rt, j, q)], axis=0).astype(bf16) for j, q in groups]
    gb = [_dot_nt(ar[i], bd(grp(bt, j, q), head_of_lane)) for i, (j, q) in enumerate(groups)]
    gk = [_dot_nt(ar[i], bd(grp(kt, j, q), head_of_lane)) for i, (j, q) in enumerate(groups)]
    lab = [jnp.where(strict, x[:C], 0.0) for x in gb]
    mrb = [jnp.where(incl, x[C:], 0.0).astype(bf16) for x in gb]
    lmk = [jnp.concatenate([jnp.where(strict, x[:C], 0.0), jnp.where(incl, x[C:], 0.0)], axis=0).astype(bf16)
           for x in gk]
    lmv = [_dot(lmk[i], bd(grp(vb, j, q), head_of_lane)) for i, (j, q) in enumerate(groups)]
    m = [_dot(x.astype(bf16), bd(x.astype(bf16), head_of_col)) for x in lab]
    t_inv = [jnp.where(unit, 1.0, x) for x in lab]
    for level in range(1, log2c):
        mb = [x.astype(bf16) for x in m]
        if level < log2c - 1:
            out = [_dot(mb[i], bd(jnp.concatenate([t_inv[i].astype(bf16), mb[i]], axis=1), head_of_col2))
                   for i in range(n_gr)]
            t_inv = [t_inv[i] + out[i][:, :G * C] for i in range(n_gr)]
            m = [x[:, G * C:] for x in out]
        else:
            t_inv = [t_inv[i] + _dot(mb[i], bd(t_inv[i].astype(bf16), head_of_col)) for i in range(n_gr)]
    wu = [_dot(t_inv[i].astype(bf16),
               bd(jnp.concatenate([grp(at, j, q).astype(bf16), lmv[i][:C].astype(bf16)], axis=1), head_of_lane2)
               ).astype(bf16) for i, (j, q) in enumerate(groups)]
    wa = [x[:, :GW] for x in wu]
    uv = [x[:, GW:] for x in wu]
    pq = [_dot(mrb[i], bd(wu[i], head_of_lane2)) for i in range(n_gr)]
    ry = [grp(rt, j, q) + pq[i][:, :GW] for i, (j, q) in enumerate(groups)]
    y0 = [lmv[i][C:] + pq[i][:, GW:] for i in range(n_gr)]
    gwt = [_dot_tn(wa[i], grp(bh, j, q)) for i, (j, q) in enumerate(groups)]
    dkf = [_dot_tn(jnp.concatenate([uv[i], grp(vb, j, q)], axis=0),
                   jnp.concatenate([grp(bh, j, q), grp(kh, j, q)], axis=0))
           for i, (j, q) in enumerate(groups)]
    for i, (j, q) in enumerate(groups):
        sl = slice(GW * q, GW * (q + 1))
        wg = jnp.where(same_head, gwt[i], 0.0) + jnp.where(eye_g, g_end[j][:, sl], 0.0)
        dk = jnp.where(head_of_lane[0], dkf[i][0:N], 0.0)
        for h in range(1, G):
            dk = dk + jnp.where(head_of_lane[h], dkf[i][N * h:N * (h + 1)], 0.0)
        s_old = s_scr[j, :, sl].astype(bf16)
        y_scr[j * C:(j + 1) * C, sl] = _dot_nt(ry[i].astype(bf16), bd(s_old, head_of_lane)) + y0[i]
        s_scr[j, :, sl] = _dot(s_old, wg.astype(bf16)) + dk

    out = _rwkv_output(y_scr[...], (r * kp * rk_ref[...], v), g, lng_ref, lnb_ref, ones_bd)
    o_ref[...] = out.reshape(nb, C, RWKV_W).astype(o_ref.dtype)

    @pl.when(c == pl.num_programs(1) - 1)
    def _():
        for j in range(nb):
            for h in range(RWKV_HEADS):
                h_out_ref[j, h] = s_scr[j, :, N * h:N * (h + 1)]
        shift_ref[...] = prev_scr[...]


def _rwkv(p, prev0, h0, params, ones_bd, *, nb, rows, valid):
    B, T, _ = p.shape

    def maybe_bcast(arr, tail):
        nd = len(tail) + 1
        if arr.shape[0] == 1:
            return pl.BlockSpec((1,) + tail, lambda b, c: (0,) * nd)
        return pl.BlockSpec((nb,) + tail, lambda b, c: (b,) + (0,) * (nd - 1))

    seq_in = pl.BlockSpec((nb, rows, SHIFT_W), lambda b, c: (b, c, 0))
    seq_out = pl.BlockSpec((nb, rows, RWKV_W), lambda b, c: (b, c, 0))
    st_tail = (RWKV_HEADS, RWKV_HD, RWKV_HD)
    sh_tail = (1, SHIFT_W)
    param_specs = [_resident(x.shape) for x in params]
    return pl.pallas_call(
        functools.partial(_rwkv_kernel, nb=nb, rows=rows, valid=valid),
        grid=(B // nb, T // rows),
        in_specs=[seq_in, maybe_bcast(prev0, sh_tail), maybe_bcast(h0, st_tail)] + param_specs
                 + [_resident(ones_bd.shape)],
        out_specs=[seq_out, pl.BlockSpec((nb,) + st_tail, lambda b, c: (b, 0, 0, 0)),
                   pl.BlockSpec((nb,) + sh_tail, lambda b, c: (b, 0, 0))],
        out_shape=[jax.ShapeDtypeStruct((B, T, RWKV_W), bf16),
                   jax.ShapeDtypeStruct((B,) + st_tail, f32),
                   jax.ShapeDtypeStruct((B,) + sh_tail, f32)],
        scratch_shapes=[pltpu.VMEM((nb, RWKV_HD, RWKV_W), f32), pltpu.VMEM((nb,) + sh_tail, f32),
                        pltpu.VMEM((nb * rows, RWKV_W), f32)],
        compiler_params=pltpu.CompilerParams(dimension_semantics=("parallel", "arbitrary"),
                                             vmem_limit_bytes=V7X_VMEM_LIMIT_BYTES),
        name="rwkv7",
    )(p, prev0, h0, *params, ones_bd)


def _rwkv_step_kernel(p_ref, shift0_ref, s_ref, mu_ref, w0_ref, w2_ref, a0_ref, a2_ref, g2_ref, kk_ref, ka_ref,
                      rk_ref, lng_ref, lnb_ref, ones_ref, o_ref, s_out_ref, shift_ref,
                      feat_scr, y_scr, g_scr, rkr_scr, v_scr, *, steps, batch):
    h = pl.program_id(0)
    T, B, N = steps, batch, RWKV_HD
    key_tiles = N // 8
    f_r, f_w, f_k, f_v, f_a, f_b = range(6)

    @pl.when(h == 0)
    def _():
        p = p_ref[...]
        p_prev = jnp.concatenate([shift0_ref[...], p[:(T - 1) * B]], axis=0)
        r, kp, v, kk, a, ld, g = _rwkv_features(p, p_prev, mu_ref, w0_ref, w2_ref, a0_ref, a2_ref, g2_ref,
                                                kk_ref, ka_ref, ones_ref[...])
        for idx, x in enumerate((r, jnp.exp(ld), kp, v, -kk, kk * a)):
            for t in range(T):
                feat_scr[idx, t] = x[t * B:(t + 1) * B, :].T
        g_scr[...] = g
        rkr_scr[...] = r * kp * rk_ref[...]
        v_scr[...] = v
        shift_ref[...] = p[(T - 1) * B:]

    base = pl.multiple_of(h * N, N)

    def tile(idx, t, kt):
        return feat_scr[idx, t, pl.ds(pl.multiple_of(base + 8 * kt, 8), 8), :]

    def keysum(s, idx, t):
        acc = s[0] * tile(idx, t, 0)
        for kt in range(1, key_tiles):
            acc = acc + s[kt] * tile(idx, t, kt)
        return jnp.sum(acc, axis=0, keepdims=True)

    def value_group(vg, carry):
        v_rows = pl.ds(pl.multiple_of(base + 8 * vg, 8), 8)
        v_tiles = [feat_scr[f_v, t, v_rows, :] for t in range(T)]
        y_rows = [[] for _ in range(T)]
        for i in range(8):
            row0 = pl.multiple_of((8 * vg + i) * N, N)
            s = [s_ref[0, pl.ds(row0 + 8 * kt, 8), :] for kt in range(key_tiles)]
            for t in range(T):
                sa = jnp.broadcast_to(keysum(s, f_a, t), (8, B))
                vv = jnp.broadcast_to(v_tiles[t][i:i + 1, :], (8, B))
                s = [s[kt] * tile(f_w, t, kt) + sa * tile(f_b, t, kt) + vv * tile(f_k, t, kt)
                     for kt in range(key_tiles)]
                y_rows[t].append(keysum(s, f_r, t))
            for kt in range(key_tiles):
                s_out_ref[0, pl.ds(row0 + 8 * kt, 8), :] = s[kt]
        for t in range(T):
            y_scr[t, v_rows, :] = jnp.concatenate(y_rows[t], axis=0)
        return carry

    lax.fori_loop(0, N // 8, value_group, 0)

    @pl.when(h == pl.num_programs(0) - 1)
    def _():
        y = jnp.concatenate([y_scr[t].T for t in range(T)], axis=0)
        out = _rwkv_output(y, (rkr_scr[...], v_scr[...]), g_scr[...], lng_ref, lnb_ref, ones_ref[...])
        o_ref[...] = out.astype(o_ref.dtype)


def _rwkv_step(p, shift0, state, params, ones_bd, *, steps):
    B = state.shape[0]
    rows = steps * B
    n_state = RWKV_HD * RWKV_HD
    s_in = jnp.transpose(state, (1, 2, 3, 0)).reshape(RWKV_HEADS, n_state, B)
    s_spec = pl.BlockSpec((1, n_state, B), lambda h: (h, 0, 0))
    o, s_out, shift = pl.pallas_call(
        functools.partial(_rwkv_step_kernel, steps=steps, batch=B),
        grid=(RWKV_HEADS,),
        in_specs=[_resident((rows, SHIFT_W)), _resident((B, SHIFT_W)), s_spec]
                 + [_resident(x.shape) for x in params] + [_resident(ones_bd.shape)],
        out_specs=[pl.BlockSpec((rows, RWKV_W), lambda h: (0, 0)), s_spec,
                   pl.BlockSpec((B, SHIFT_W), lambda h: (0, 0))],
        out_shape=[jax.ShapeDtypeStruct((rows, RWKV_W), bf16),
                   jax.ShapeDtypeStruct((RWKV_HEADS, n_state, B), f32),
                   jax.ShapeDtypeStruct((B, SHIFT_W), f32)],
        scratch_shapes=[pltpu.VMEM((6, steps, RWKV_W, B), f32), pltpu.VMEM((steps, RWKV_W, B), f32),
                        pltpu.VMEM((rows, RWKV_W), f32), pltpu.VMEM((rows, RWKV_W), f32),
                        pltpu.VMEM((rows, RWKV_W), f32)],
        compiler_params=pltpu.CompilerParams(dimension_semantics=("arbitrary",),
                                             vmem_limit_bytes=V7X_VMEM_LIMIT_BYTES),
        name="rwkv7_step",
    )(p, shift0, s_in, *params, ones_bd)
    s_out = jnp.transpose(s_out.reshape(RWKV_HEADS, RWKV_HD, RWKV_HD, B), (3, 0, 1, 2))
    return o, s_out, shift


def _merge_ffn_kernel(h_ref, oret_ref, orwkv_ref, gate_ref, wr_ref, ww_ref, wo_ref,
                      g_ref, wg_ref, wu_ref, wd_ref, fin_ref, o_ref):
    a = _dot(oret_ref[...], wr_ref[...])
    b = _dot(orwkv_ref[...], ww_ref[...])
    merged = gate_ref[:, :D_MODEL] * a + gate_ref[:, D_MODEL:] * b
    h = h_ref[...] + _dot(merged.astype(bf16), wo_ref[...])
    h = _swiglu_half_step(h, g_ref, _column_chunks(wg_ref), _column_chunks(wu_ref), _row_chunks(wd_ref))
    o_ref[...] = _rms(h, fin_ref[...])


def _merge_ffn(h, o_ret, o_rwkv, gates, w_out_ret, w_out_rwkv, w_out, norm_g, wg, wu, wd, fin_g, *, tm):
    rows = h.shape[0]

    def rowspec(width):
        return pl.BlockSpec((tm, width), lambda i: (i, 0))

    return pl.pallas_call(
        _merge_ffn_kernel,
        grid=(rows // tm,),
        in_specs=[rowspec(D_MODEL), rowspec(RET_V), rowspec(RWKV_W), rowspec(GATE_W),
                  _resident((RET_V, D_MODEL)), _resident((RWKV_W, D_MODEL)), _resident((D_MODEL, D_MODEL))]
                 + [_resident(s) for s in _FFN_WEIGHT_SPECS] + [_resident((1, D_MODEL))],
        out_specs=rowspec(D_MODEL),
        out_shape=jax.ShapeDtypeStruct((rows, D_MODEL), f32),
        compiler_params=pltpu.CompilerParams(dimension_semantics=("parallel",),
                                             vmem_limit_bytes=V7X_VMEM_LIMIT_BYTES),
        name="merge_ffn",
    )(h, o_ret, o_rwkv, gates, w_out_ret, w_out_rwkv, w_out, norm_g, wg, wu, wd, fin_g)


def _rotary_tables(pos):
    half = RET_DK // 2
    inv_freq = ROPE_BASE ** (-jnp.arange(half, dtype=f32) / half)
    ang = pos.astype(f32)[:, None] * inv_freq[None, :]
    cos = jnp.cos(ang)
    sin = jnp.sin(ang)
    cos_t = jnp.tile(jnp.concatenate([cos, cos], axis=1), (1, RET_HEADS))
    sin_t = jnp.tile(jnp.concatenate([-sin, sin], axis=1), (1, RET_HEADS))
    return cos_t, sin_t


def _row_tile(rows, target):
    tm = min(rows, target)
    while rows % tm:
        tm -= 8
    return tm


def kernel(x_prompt, x_sample, state_ret, state_wkv, state_shift, meta_tokens, ffn1_norm, ffn1_w_gate, ffn1_w_up, ffn1_w_down, mix_norm, w_in, ret_gn_g, mu_shift, w0, w2, a0, a2, g2, k_k, k_a, r_k, lnx_g, lnx_b, w_out_ret, w_out_rwkv, w_out, ffn2_norm, ffn2_w_gate, ffn2_w_up, ffn2_w_down, final_norm):
    Bp, Tp, _ = x_prompt.shape
    Bs, Ts, _ = x_sample.shape
    st_dtype = state_ret.dtype

    def row(x):
        return x.reshape(1, -1).astype(f32)

    fin = row(final_norm)
    mix_g = row(mix_norm)
    wr_b, ww_b, wo_b = w_out_ret.astype(bf16), w_out_rwkv.astype(bf16), w_out.astype(bf16)
    rwkv_params = (row(mu_shift), row(w0), w2.astype(bf16), row(a0), a2.astype(bf16), g2.astype(bf16),
                   row(k_k), row(k_a), row(r_k), row(lnx_g), row(lnx_b))
    head_id = jnp.arange(RWKV_W, dtype=jnp.int32) // RWKV_HD
    ones_bd = (head_id[:, None] == head_id[None, :]).astype(bf16)
    gn = row(ret_gn_g)

    n_p = Bp * Tp
    h_p, *ffn1_b, w_in_b = _ffn_cast(x_prompt.reshape(n_p, D_MODEL), row(ffn1_norm), ffn1_w_gate, ffn1_w_up,
                                     ffn1_w_down, w_in, tm=_row_tile(n_p, 512))

    n_s = Bs * Ts
    x_small = jnp.concatenate([jnp.swapaxes(x_sample, 0, 1).reshape(n_s, D_MODEL),
                               meta_tokens.astype(x_sample.dtype)], axis=0)
    cos_s, sin_s = _rotary_tables(PAST_LEN + jnp.arange(Ts, dtype=jnp.int32))
    cos_m, sin_m = _rotary_tables(jnp.arange(N_META, dtype=jnp.int32))
    cos_small = jnp.concatenate([jnp.repeat(cos_s, Bs, axis=0), cos_m], axis=0)
    sin_small = jnp.concatenate([jnp.repeat(sin_s, Bs, axis=0), sin_m], axis=0)
    n_small = n_s + N_META
    h_small = _ffn(x_small, row(ffn1_norm), *ffn1_b, tm=n_small)
    q_s, k_s, v_s, sg_s, p_s, gate_s = _inproj(h_small, mix_g, w_in_b, cos_small, sin_small, tm=n_small)

    def meta(x):
        return x[n_s:].reshape(1, N_META, -1)

    zeros_ret = jnp.zeros((1, RET_HEADS, RET_DK, RET_DV), f32)
    _, s_ret_m = _retention(meta(q_s), meta(k_s), meta(v_s), meta(sg_s), zeros_ret, gn,
                            nb=1, rows=N_META, valid=N_META)
    _, h_wkv_m, shift_m = _rwkv(meta(p_s), jnp.zeros((1, 1, SHIFT_W), f32),
                                jnp.zeros((1, RWKV_HEADS, RWKV_HD, RWKV_HD), f32), rwkv_params, ones_bd,
                                nb=1, rows=N_META, valid=N_META)

    pad_t = -(-Ts // 8) * 8

    def samp(x):
        x = jnp.swapaxes(x[:n_s].reshape(Ts, Bs, -1), 0, 1)
        return jnp.pad(x, ((0, 0), (0, pad_t - Ts), (0, 0)))

    o_ret_s, s_ret_s = _retention(samp(q_s), samp(k_s), samp(v_s), samp(sg_s), state_ret.astype(f32), gn,
                                  nb=RET_SAMPLE_SEQS, rows=pad_t, valid=Ts)
    o_ret_s = jnp.swapaxes(o_ret_s[:, :Ts], 0, 1).reshape(n_s, RET_V)
    o_wkv_s, h_wkv_s, shift_s = _rwkv_step(p_s[:n_s], state_shift.astype(f32), state_wkv.astype(f32),
                                           rwkv_params, ones_bd, steps=Ts)

    cos_p, sin_p = _rotary_tables(N_META + jnp.arange(Tp, dtype=jnp.int32))
    q_p, k_p, v_p, sg_p, p_p, gate_p, *ffn2_b = _inproj(h_p, mix_g, w_in_b, cos_p, sin_p, tm=_row_tile(Tp, 512),
                                                        cast_ffn=(ffn2_w_gate, ffn2_w_up, ffn2_w_down))

    def post(h, o_ret, o_rwkv, gates, tm):
        return _merge_ffn(h, o_ret, o_rwkv, gates, wr_b, ww_b, wo_b, row(ffn2_norm), *ffn2_b, fin, tm=tm)

    y_sample = post(h_small[:n_s], o_ret_s, o_wkv_s, gate_s[:n_s], _row_tile(n_s, 512))
    y_sample = jnp.swapaxes(y_sample.reshape(Ts, Bs, D_MODEL), 0, 1)

    def seqs(x):
        return x.reshape(Bp, Tp, -1)

    o_ret_p, s_ret_p = _retention(seqs(q_p), seqs(k_p), seqs(v_p), seqs(sg_p), s_ret_m, gn,
                                  nb=RET_PROMPT_SEQS, rows=RET_CHUNK, valid=RET_CHUNK)
    o_wkv_p, h_wkv_p, shift_p = _rwkv(seqs(p_p), shift_m, h_wkv_m, rwkv_params, ones_bd,
                                      nb=RWKV_PROMPT_SEQS, rows=RWKV_CHUNK, valid=RWKV_CHUNK)
    y_prompt = post(h_p, o_ret_p.reshape(n_p, RET_V), o_wkv_p.reshape(n_p, RWKV_W), gate_p, _row_tile(n_p, 512))

    return (y_prompt.reshape(Bp, Tp, D_MODEL).astype(x_prompt.dtype),
            y_sample.reshape(Bs, Ts, D_MODEL).astype(x_sample.dtype),
            s_ret_p.astype(st_dtype), h_wkv_p.astype(st_dtype),
            shift_p.reshape(Bp, SHIFT_W).astype(st_dtype),
            s_ret_s.astype(st_dtype), h_wkv_s.astype(st_dtype),
            shift_s.reshape(Bs, SHIFT_W).astype(st_dtype))
```

```python
import functools

import numpy as np
import jax
import jax.numpy as jnp
from jax import lax
from jax.experimental import pallas as pl
from jax.experimental.pallas import tpu as pltpu

D_MODEL = 1024
N_META = 16
PAST_LEN = 16384
RET_HEADS = 4
RET_DK = 64
RET_DV = 128
RET_CHUNK = 128
RWKV_HEADS = 8
RWKV_HD = 64
RWKV_W = RWKV_HEADS * RWKV_HD
DECAY_LORA = 64
AAA_LORA = 64
GATE_LORA = 128
D_FF = 2816
ROPE_BASE = 10000.0
NORM_EPS = 1e-6
RET_GN_EPS = 1e-6
RWKV_GN_EPS = 64e-5
RET_QK = RET_HEADS * RET_DK
RET_V = RET_HEADS * RET_DV
SHIFT_W = 3 * RWKV_W + DECAY_LORA + AAA_LORA + GATE_LORA
GATE_W = 2 * D_MODEL
PROJ_W = 2 * RET_QK + 2 * RET_V + SHIFT_W + GATE_W

_C_Q, _C_K, _C_V, _C_G = 0, RET_QK, 2 * RET_QK, 2 * RET_QK + RET_V
_C_P = 2 * RET_QK + 2 * RET_V
_C_GATE = _C_P + SHIFT_W

V7X_VMEM_LIMIT_BYTES = 56 * 1024 * 1024
FF_CHUNK = 256
RWKV_CHUNK = 64
RWKV_LANE_HEADS = 2
RWKV_PROMPT_SEQS = 8
RET_PROMPT_SEQS = 8
RET_SAMPLE_SEQS = 32
LOG_GAMMA = tuple(float(np.log1p(-2.0 ** (-5.0 - h))) for h in range(RET_HEADS))

f32 = jnp.float32
bf16 = jnp.bfloat16


def _resident(shape):
    zeros = (0,) * len(shape)
    return pl.BlockSpec(shape, lambda *_: zeros, pipeline_mode=pl.Buffered(1))


def _rms(x, g):
    return x * lax.rsqrt(jnp.mean(x * x, axis=-1, keepdims=True) + NORM_EPS) * g


def _dot(a, b):
    return jnp.dot(a, b, preferred_element_type=f32)


def _dot_nt(a, b):
    return lax.dot_general(a, b, (((1,), (1,)), ((), ())), preferred_element_type=f32)


def _dot_tn(a, b):
    return lax.dot_general(a, b, (((0,), (0,)), ((), ())), preferred_element_type=f32)


FF_CHUNKS = D_FF // FF_CHUNK


def _swiglu_half_step(x, g_ref, gate, up, down):
    xn = _rms(x, g_ref[...]).astype(bf16)
    acc = jnp.zeros(x.shape, f32)
    for c in range(FF_CHUNKS):
        gt = _dot(xn, gate(c))
        ut = _dot(xn, up(c))
        act = (gt * jax.nn.sigmoid(gt) * ut).astype(bf16)
        acc = acc + _dot(act, down(c))
    return x + 0.5 * acc


def _column_chunks(w_ref):
    return lambda c: w_ref[:, c * FF_CHUNK:(c + 1) * FF_CHUNK]


def _row_chunks(w_ref):
    return lambda c: w_ref[c * FF_CHUNK:(c + 1) * FF_CHUNK, :]


def _ffn_kernel(x_ref, g_ref, wg_ref, wu_ref, wd_ref, o_ref):
    o_ref[...] = _swiglu_half_step(x_ref[...], g_ref, _column_chunks(wg_ref), _column_chunks(wu_ref),
                                   _row_chunks(wd_ref))


_FFN_WEIGHT_SPECS = ((1, D_MODEL), (D_MODEL, D_FF), (D_MODEL, D_FF), (D_FF, D_MODEL))


def _ffn_chunk_specs(step_of):
    col = pl.BlockSpec((D_MODEL, FF_CHUNK), lambda i: (0, step_of(i)))
    return [col, col, pl.BlockSpec((FF_CHUNK, D_MODEL), lambda i: (step_of(i), 0))]


def _ffn_bf16_shapes():
    return [jax.ShapeDtypeStruct(s, bf16) for s in _FFN_WEIGHT_SPECS[1:]]


def _ffn_stream_kernel(x_ref, g_ref, wg_f, wu_f, wd_f, o_ref, wg_o, wu_o, wd_o, xn_s, acc_s):
    c = pl.program_id(0)

    @pl.when(c == 0)
    def _():
        xn_s[...] = _rms(x_ref[...], g_ref[...]).astype(bf16)
        acc_s[...] = jnp.zeros(acc_s.shape, f32)

    wg = wg_f[...].astype(bf16)
    wu = wu_f[...].astype(bf16)
    wd = wd_f[...].astype(bf16)
    wg_o[...] = wg
    wu_o[...] = wu
    wd_o[...] = wd
    xn = xn_s[...]
    gt = _dot(xn, wg)
    ut = _dot(xn, wu)
    act = (gt * jax.nn.sigmoid(gt) * ut).astype(bf16)
    acc_s[...] = acc_s[...] + _dot(act, wd)

    @pl.when(c == pl.num_programs(0) - 1)
    def _():
        o_ref[...] = x_ref[...] + 0.5 * acc_s[...]


def _ffn_stream(x, norm_g, wg, wu, wd):
    rows = x.shape[0]
    weight_chunks = _ffn_chunk_specs(lambda c: c)
    whole = pl.BlockSpec((rows, D_MODEL), lambda c: (0, 0))
    return pl.pallas_call(
        _ffn_stream_kernel,
        grid=(FF_CHUNKS,),
        in_specs=[_resident((rows, D_MODEL)), _resident((1, D_MODEL))] + weight_chunks,
        out_specs=[whole] + weight_chunks,
        out_shape=[jax.ShapeDtypeStruct((rows, D_MODEL), f32)] + _ffn_bf16_shapes(),
        scratch_shapes=[pltpu.VMEM((rows, D_MODEL), bf16), pltpu.VMEM((rows, D_MODEL), f32)],
        compiler_params=pltpu.CompilerParams(dimension_semantics=("arbitrary",),
                                             vmem_limit_bytes=V7X_VMEM_LIMIT_BYTES),
        name="ffn_stream",
    )(x, norm_g, wg, wu, wd)


def _ffn_side_kernel(x_ref, g_ref, wg_ref, wu_ref, wd_ref, win_f, o_ref, win_o, *, in_chunks):
    _ffn_kernel(x_ref, g_ref, wg_ref, wu_ref, wd_ref, o_ref)

    @pl.when(pl.program_id(0) < in_chunks)
    def _():
        win_o[...] = win_f[...].astype(bf16)


def _ffn(x, norm_g, wg, wu, wd, *, tm, cast_w_in=None):
    rows = x.shape[0]
    tiles = rows // tm
    row = pl.BlockSpec((tm, D_MODEL), lambda i: (i, 0))
    in_specs = [row] + [_resident(s) for s in _FFN_WEIGHT_SPECS]
    if cast_w_in is None:
        return pl.pallas_call(
            _ffn_kernel,
            grid=(tiles,),
            in_specs=in_specs,
            out_specs=row,
            out_shape=jax.ShapeDtypeStruct((rows, D_MODEL), f32),
            compiler_params=pltpu.CompilerParams(dimension_semantics=("parallel",),
                                                 vmem_limit_bytes=V7X_VMEM_LIMIT_BYTES),
            name="ffn",
        )(x, norm_g, wg, wu, wd)
    in_chunks = PROJ_W // FF_CHUNK
    assert in_chunks <= tiles
    win = pl.BlockSpec((D_MODEL, FF_CHUNK), lambda i: (0, jnp.minimum(i, in_chunks - 1)))
    return pl.pallas_call(
        functools.partial(_ffn_side_kernel, in_chunks=in_chunks),
        grid=(tiles,),
        in_specs=in_specs + [win],
        out_specs=[row, win],
        out_shape=[jax.ShapeDtypeStruct((rows, D_MODEL), f32), jax.ShapeDtypeStruct((D_MODEL, PROJ_W), bf16)],
        compiler_params=pltpu.CompilerParams(dimension_semantics=("arbitrary",),
                                             vmem_limit_bytes=V7X_VMEM_LIMIT_BYTES),
        name="ffn",
    )(x, norm_g, wg, wu, wd, cast_w_in)


def _swap_halves(x):
    parts = []
    for j in range(x.shape[1] // 128):
        xs = x[:, 128 * j:128 * (j + 1)]
        fwd = pltpu.roll(xs, 32, 1)
        bwd = pltpu.roll(xs, 96, 1)
        lane = lax.broadcasted_iota(jnp.int32, xs.shape, 1)
        parts.append(jnp.where((lane % RET_DK) < RET_DK // 2, bwd, fwd))
    return jnp.concatenate(parts, axis=1)


def _inproj_cast_kernel(h_ref, g_ref, w_ref, cos_ref, sin_ref, wg_f, wu_f, wd_f,
                        q_ref, k_ref, v_ref, sg_ref, p_ref, gate_ref, wg_o, wu_o, wd_o):
    _inproj_kernel(h_ref, g_ref, w_ref, cos_ref, sin_ref, q_ref, k_ref, v_ref, sg_ref, p_ref, gate_ref)

    @pl.when(pl.program_id(0) < FF_CHUNKS)
    def _():
        for w_f, w_o in ((wg_f, wg_o), (wu_f, wu_o), (wd_f, wd_o)):
            w_o[...] = w_f[...].astype(bf16)


def _inproj_kernel(h_ref, g_ref, w_ref, cos_ref, sin_ref, q_ref, k_ref, v_ref, sg_ref, p_ref, gate_ref):
    un = _rms(h_ref[...], g_ref[...]).astype(bf16)
    gate_ref[...] = jax.nn.sigmoid(_dot(un, w_ref[:, _C_GATE:PROJ_W]))
    gr = _dot(un, w_ref[:, _C_G:_C_P])
    sg_ref[...] = gr * jax.nn.sigmoid(gr)
    cos = cos_ref[...]
    sin = sin_ref[...]
    q = _dot(un, w_ref[:, _C_Q:_C_K])
    q_ref[...] = q * cos + _swap_halves(q) * sin
    k = _dot(un, w_ref[:, _C_K:_C_V])
    k_ref[...] = (k * cos + _swap_halves(k) * sin) * (RET_DK ** -0.5)
    v_ref[...] = _dot(un, w_ref[:, _C_V:_C_G]).astype(v_ref.dtype)
    p_ref[...] = _dot(un, w_ref[:, _C_P:_C_GATE])


def _inproj(h, norm_g, w_in, cos, sin, *, tm, cast_ffn=None):
    rows = h.shape[0]
    tiles = rows // tm
    tab_blocks = cos.shape[0] // tm

    def rowspec(width):
        return pl.BlockSpec((tm, width), lambda i: (i, 0))

    tab = pl.BlockSpec((tm, RET_QK), lambda i: (i % tab_blocks, 0))
    widths = (RET_QK, RET_QK, RET_V, RET_V, SHIFT_W, GATE_W)
    dtypes = (f32, f32, bf16, f32, f32, f32)
    in_specs = [rowspec(D_MODEL), _resident((1, D_MODEL)), _resident((D_MODEL, PROJ_W)), tab, tab]
    out_specs = [rowspec(w) for w in widths]
    out_shape = [jax.ShapeDtypeStruct((rows, w), d) for w, d in zip(widths, dtypes)]
    operands = (h, norm_g, w_in, cos, sin)
    if cast_ffn is not None:
        assert FF_CHUNKS <= tiles
        weight_chunks = _ffn_chunk_specs(lambda i: jnp.minimum(i, FF_CHUNKS - 1))
        in_specs += weight_chunks
        out_specs += weight_chunks
        out_shape += _ffn_bf16_shapes()
        operands += tuple(cast_ffn)
    return pl.pallas_call(
        _inproj_kernel if cast_ffn is None else _inproj_cast_kernel,
        grid=(tiles,),
        in_specs=in_specs,
        out_specs=out_specs,
        out_shape=out_shape,
        compiler_params=pltpu.CompilerParams(dimension_semantics=("arbitrary",),
                                             vmem_limit_bytes=V7X_VMEM_LIMIT_BYTES),
        name="inproj",
    )(*operands)


def _ret_kernel(q_ref, k_ref, v_ref, sg_ref, s0_ref, gn_ref, o_ref, s_out_ref, s_scr, o_scr, *, nb, rows, valid):
    c = pl.program_id(1)
    L = rows

    @pl.when(c == 0)
    def _():
        s_scr[...] = jnp.broadcast_to(s0_ref[...], s_scr.shape)

    ii = lax.broadcasted_iota(jnp.int32, (L, L), 0)
    jj = lax.broadcasted_iota(jnp.int32, (L, L), 1)
    diff = (ii - jj).astype(f32)
    row = lax.broadcasted_iota(jnp.int32, (L, 1), 0).astype(f32)
    mask = [jnp.where(diff >= 0, jnp.exp(lg * jnp.maximum(diff, 0.0)), 0.0) for lg in LOG_GAMMA]
    q_decay = [jnp.exp(lg * (row + 1.0)) for lg in LOG_GAMMA]
    k_decay = [jnp.exp(lg * (valid - 1.0 - row)) for lg in LOG_GAMMA]
    s_decay = [float(np.exp(lg * valid)) for lg in LOG_GAMMA]

    chains = [(j, h) for j in range(nb) for h in range(RET_HEADS)]
    qh = [q_ref[j, :, RET_DK * h:RET_DK * (h + 1)] for j, h in chains]
    kh = [k_ref[j, :, RET_DK * h:RET_DK * (h + 1)] for j, h in chains]
    vh = [v_ref[j, :, RET_DV * h:RET_DV * (h + 1)] for j, h in chains]
    scores = [(_dot_nt(qh[i].astype(bf16), kh[i].astype(bf16)) * mask[h]).astype(bf16)
              for i, (j, h) in enumerate(chains)]
    qd = [(qh[i] * q_decay[h]).astype(bf16) for i, (j, h) in enumerate(chains)]
    kd = [(kh[i] * k_decay[h]).astype(bf16) for i, (j, h) in enumerate(chains)]
    s_old = [s_scr[j, h] for j, h in chains]
    for i, (j, h) in enumerate(chains):
        o_scr[j * L:(j + 1) * L, RET_DV * h:RET_DV * (h + 1)] = (
            _dot(scores[i], vh[i]) + _dot(qd[i], s_old[i].astype(bf16)))
    for i, (j, h) in enumerate(chains):
        s_scr[j, h] = s_decay[h] * s_old[i] + _dot_tn(kd[i], vh[i])

    gn = gn_ref[...]
    sg = sg_ref[...].reshape(nb * L, RET_V)
    for h in range(RET_HEADS):
        sl = slice(RET_DV * h, RET_DV * (h + 1))
        o = o_scr[:, sl]
        mu = jnp.mean(o, axis=-1, keepdims=True)
        oc = o - mu
        var = jnp.mean(oc * oc, axis=-1, keepdims=True)
        out = oc * lax.rsqrt(var + RET_GN_EPS) * gn[:, sl] * sg[:, sl]
        o_ref[:, :, sl] = out.reshape(nb, L, RET_DV).astype(o_ref.dtype)

    @pl.when(c == pl.num_programs(1) - 1)
    def _():
        s_out_ref[...] = s_scr[...]


def _retention(q, k, v, sg, s0, gn, *, nb, rows, valid):
    B, T, _ = q.shape
    bcast = s0.shape[0] == 1

    def seq(width):
        return pl.BlockSpec((nb, rows, width), lambda b, c: (b, c, 0))

    state = pl.BlockSpec((nb, RET_HEADS, RET_DK, RET_DV), lambda b, c: (b, 0, 0, 0))
    state_in = pl.BlockSpec((1, RET_HEADS, RET_DK, RET_DV), lambda b, c: (0, 0, 0, 0)) if bcast else state
    return pl.pallas_call(
        functools.partial(_ret_kernel, nb=nb, rows=rows, valid=valid),
        grid=(B // nb, T // rows),
        in_specs=[seq(RET_QK), seq(RET_QK), seq(RET_V), seq(RET_V), state_in, _resident((1, RET_V))],
        out_specs=[seq(RET_V), state],
        out_shape=[jax.ShapeDtypeStruct((B, T, RET_V), bf16),
                   jax.ShapeDtypeStruct((B, RET_HEADS, RET_DK, RET_DV), f32)],
        scratch_shapes=[pltpu.VMEM((nb, RET_HEADS, RET_DK, RET_DV), f32), pltpu.VMEM((nb * rows, RET_V), f32)],
        compiler_params=pltpu.CompilerParams(dimension_semantics=("parallel", "arbitrary"),
                                             vmem_limit_bytes=V7X_VMEM_LIMIT_BYTES),
        name="retention",
    )(q, k, v, sg, s0, gn)


def _rwkv_features(p, p_prev, mu_ref, w0_ref, w2_ref, a0_ref, a2_ref, g2_ref, kk_ref, ka_ref, ones_bd):
    pm = p + (p_prev - p) * mu_ref[...]
    r = pm[:, 0:RWKV_W]
    k = pm[:, RWKV_W:2 * RWKV_W]
    v = pm[:, 2 * RWKV_W:3 * RWKV_W]
    o_w = 3 * RWKV_W
    xw = pm[:, o_w:o_w + DECAY_LORA]
    xa = pm[:, o_w + DECAY_LORA:o_w + DECAY_LORA + AAA_LORA]
    xg = pm[:, o_w + DECAY_LORA + AAA_LORA:SHIFT_W]
    z = w0_ref[...] + _dot(jnp.tanh(xw).astype(bf16), w2_ref[...])
    ld = -float(np.exp(-0.5)) * jax.nn.sigmoid(z)
    a = jax.nn.sigmoid(a0_ref[...] + _dot(xa.astype(bf16), a2_ref[...]))
    g = _dot(jax.nn.sigmoid(xg).astype(bf16), g2_ref[...])
    kk = k * kk_ref[...]
    kk = kk * lax.rsqrt(jnp.maximum(_dot((kk * kk).astype(bf16), ones_bd), 1e-24))
    kp = k * (1.0 + (a - 1.0) * ka_ref[...])
    return r, kp, v, kk, a, ld, g


def _rwkv_output(y, bonus_rkv, g, lng_ref, lnb_ref, ones_bd):
    inv_n = 1.0 / RWKV_HD
    mean = _dot(y.astype(bf16), ones_bd) * inv_n
    yc = y - mean
    var = _dot((yc * yc).astype(bf16), ones_bd) * inv_n
    out = yc * lax.rsqrt(var + RWKV_GN_EPS) * lng_ref[...] + lnb_ref[...]
    rkr, v = bonus_rkv
    bonus = _dot(rkr.astype(bf16), ones_bd) * v
    return (out + bonus) * g


def _rwkv_kernel(p_ref, prev0_ref, h0_ref, mu_ref, w0_ref, w2_ref, a0_ref, a2_ref, g2_ref, kk_ref, ka_ref,
                 rk_ref, lng_ref, lnb_ref, ones_ref, o_ref, h_out_ref, shift_ref, s_scr, prev_scr, y_scr,
                 *, nb, rows, valid):
    c = pl.program_id(1)
    C = rows
    N = RWKV_HD
    R = nb * C
    assert C & (C - 1) == 0
    log2c = C.bit_length() - 1

    @pl.when(c == 0)
    def _():
        for j in range(nb):
            for h in range(RWKV_HEADS):
                s_scr[j, :, N * h:N * (h + 1)] = h0_ref[j if h0_ref.shape[0] > 1 else 0, h]
        prev_scr[...] = jnp.broadcast_to(prev0_ref[...], prev_scr.shape)

    p = p_ref[...].reshape(R, SHIFT_W)
    rowid = lax.broadcasted_iota(jnp.int32, (R, 1), 0)
    step = rowid & (C - 1)
    p_prev = pltpu.roll(p, 1, 0)
    for j in range(nb):
        p_prev = jnp.where(rowid == j * C, prev_scr[j], p_prev)
    for j in range(nb):
        prev_scr[j] = p[j * C + valid - 1:j * C + valid, :]
    ones_bd = ones_ref[...]
    r, kp, v, kk, a, ld, g = _rwkv_features(p, p_prev, mu_ref, w0_ref, w2_ref, a0_ref, a2_ref, g2_ref, kk_ref,
                                            ka_ref, ones_bd)
    if valid < C:
        live = (step < valid).astype(f32)
        ld = ld * live
        kk = kk * live
        kp = kp * live
        v = v * live
    b = kk * a

    cum = ld
    for level in range(log2c):
        reach = 1 << level
        cum = cum + jnp.where(step >= reach, pltpu.roll(cum, reach, 0), 0.0)
    last_rows = [cum[j * C + C - 1:j * C + C, :] for j in range(nb)]
    cum_last = jnp.concatenate([jnp.broadcast_to(x, (C, RWKV_W)) for x in last_rows], axis=0)
    e_in = jnp.exp(cum)
    e_ex = jnp.exp(cum - ld)
    e_neg = jnp.exp(-cum)
    e_end = jnp.exp(cum_last - cum)
    g_end = [jnp.exp(x) for x in last_rows]
    at = -kk * e_ex
    rt = r * e_in
    bt = (b * e_neg).astype(bf16)
    kt = (kp * e_neg).astype(bf16)
    bh = (b * e_end).astype(bf16)
    kh = (kp * e_end).astype(bf16)
    vb = v.astype(bf16)

    G = RWKV_LANE_HEADS
    GW = G * N
    log2n = N.bit_length() - 1
    def head_masks(width, log2_block):
        lane = lax.broadcasted_iota(jnp.int32, (1, width), 1)
        return [((lane >> log2_block) & (G - 1)) == h for h in range(G)]

    head_of_lane = head_masks(GW, log2n)
    head_of_lane2 = head_masks(2 * GW, log2n)
    head_of_col = head_masks(G * C, log2c)
    head_of_col2 = head_masks(2 * G * C, log2c)
    ti = lax.broadcasted_iota(jnp.int32, (C, G * C), 0)
    tj = lax.broadcasted_iota(jnp.int32, (C, G * C), 1) & (C - 1)
    incl = ti >= tj
    strict = ti > tj
    unit = ti == tj
    gi = lax.broadcasted_iota(jnp.int32, (GW, GW), 0)
    gj = lax.broadcasted_iota(jnp.int32, (GW, GW), 1)
    eye_g = gi == gj
    same_head = (gi >> log2n) == (gj >> log2n)

    def bd(a, masks):
        return jnp.concatenate([jnp.where(m, a, jnp.zeros_like(a)) for m in masks], axis=0)

    groups = [(j, q) for j in range(nb) for q in range(RWKV_HEADS // G)]
    n_gr = len(groups)

    def grp(x, j, q):
        return x[j * C:(j + 1) * C, GW * q:GW * (q + 1)]

    ar = [jnp.concatenate([grp(at, j, q), grp(rt, j, q)], axis=0).astype(bf16) for j, q in groups]
    gb = [_dot_nt(ar[i], bd(grp(bt, j, q), head_of_lane)) for i, (j, q) in enumerate(groups)]
    gk = [_dot_nt(ar[i], bd(grp(kt, j, q), head_of_lane)) for i, (j, q) in enumerate(groups)]
    lab = [jnp.where(strict, x[:C], 0.0) for x in gb]
    mrb = [jnp.where(incl, x[C:], 0.0).astype(bf16) for x in gb]
    lmk = [jnp.concatenate([jnp.where(strict, x[:C], 0.0), jnp.where(incl, x[C:], 0.0)], axis=0).astype(bf16)
           for x in gk]
    lmv = [_dot(lmk[i], bd(grp(vb, j, q), head_of_lane)) for i, (j, q) in enumerate(groups)]
    m = [_dot(x.astype(bf16), bd(x.astype(bf16), head_of_col)) for x in lab]
    t_inv = [jnp.where(unit, 1.0, x) for x in lab]
    for level in range(1, log2c):
        mb = [x.astype(bf16) for x in m]
        if level < log2c - 1:
            out = [_dot(mb[i], bd(jnp.concatenate([t_inv[i].astype(bf16), mb[i]], axis=1), head_of_col2))
                   for i in range(n_gr)]
            t_inv = [t_inv[i] + out[i][:, :G * C] for i in range(n_gr)]
            m = [x[:, G * C:] for x in out]
        else:
            t_inv = [t_inv[i] + _dot(mb[i], bd(t_inv[i].astype(bf16), head_of_col)) for i in range(n_gr)]
    wu = [_dot(t_inv[i].astype(bf16),
               bd(jnp.concatenate([grp(at, j, q).astype(bf16), lmv[i][:C].astype(bf16)], axis=1), head_of_lane2)
               ).astype(bf16) for i, (j, q) in enumerate(groups)]
    wa = [x[:, :GW] for x in wu]
    uv = [x[:, GW:] for x in wu]
    pq = [_dot(mrb[i], bd(wu[i], head_of_lane2)) for i in range(n_gr)]
    ry = [grp(rt, j, q) + pq[i][:, :GW] for i, (j, q) in enumerate(groups)]
    y0 = [lmv[i][C:] + pq[i][:, GW:] for i in range(n_gr)]
    gwt = [_dot_tn(wa[i], grp(bh, j, q)) for i, (j, q) in enumerate(groups)]
    dkf = [_dot_tn(jnp.concatenate([uv[i], grp(vb, j, q)], axis=0),
                   jnp.concatenate([grp(bh, j, q), grp(kh, j, q)], axis=0))
           for i, (j, q) in enumerate(groups)]
    for i, (j, q) in enumerate(groups):
        sl = slice(GW * q, GW * (q + 1))
        wg = jnp.where(same_head, gwt[i], 0.0) + jnp.where(eye_g, g_end[j][:, sl], 0.0)
        dk = jnp.where(head_of_lane[0], dkf[i][0:N], 0.0)
        for h in range(1, G):
            dk = dk + jnp.where(head_of_lane[h], dkf[i][N * h:N * (h + 1)], 0.0)
        s_old = s_scr[j, :, sl].astype(bf16)
        y_scr[j * C:(j + 1) * C, sl] = _dot_nt(ry[i].astype(bf16), bd(s_old, head_of_lane)) + y0[i]
        s_scr[j, :, sl] = _dot(s_old, wg.astype(bf16)) + dk

    out = _rwkv_output(y_scr[...], (r * kp * rk_ref[...], v), g, lng_ref, lnb_ref, ones_bd)
    o_ref[...] = out.reshape(nb, C, RWKV_W).astype(o_ref.dtype)

    @pl.when(c == pl.num_programs(1) - 1)
    def _():
        for j in range(nb):
            for h in range(RWKV_HEADS):
                h_out_ref[j, h] = s_scr[j, :, N * h:N * (h + 1)]
        shift_ref[...] = prev_scr[...]


def _rwkv(p, prev0, h0, params, ones_bd, *, nb, rows, valid):
    B, T, _ = p.shape

    def maybe_bcast(arr, tail):
        nd = len(tail) + 1
        if arr.shape[0] == 1:
            return pl.BlockSpec((1,) + tail, lambda b, c: (0,) * nd)
        return pl.BlockSpec((nb,) + tail, lambda b, c: (b,) + (0,) * (nd - 1))

    seq_in = pl.BlockSpec((nb, rows, SHIFT_W), lambda b, c: (b, c, 0))
    seq_out = pl.BlockSpec((nb, rows, RWKV_W), lambda b, c: (b, c, 0))
    st_tail = (RWKV_HEADS, RWKV_HD, RWKV_HD)
    sh_tail = (1, SHIFT_W)
    param_specs = [_resident(x.shape) for x in params]
    return pl.pallas_call(
        functools.partial(_rwkv_kernel, nb=nb, rows=rows, valid=valid),
        grid=(B // nb, T // rows),
        in_specs=[seq_in, maybe_bcast(prev0, sh_tail), maybe_bcast(h0, st_tail)] + param_specs
                 + [_resident(ones_bd.shape)],
        out_specs=[seq_out, pl.BlockSpec((nb,) + st_tail, lambda b, c: (b, 0, 0, 0)),
                   pl.BlockSpec((nb,) + sh_tail, lambda b, c: (b, 0, 0))],
        out_shape=[jax.ShapeDtypeStruct((B, T, RWKV_W), bf16),
                   jax.ShapeDtypeStruct((B,) + st_tail, f32),
                   jax.ShapeDtypeStruct((B,) + sh_tail, f32)],
        scratch_shapes=[pltpu.VMEM((nb, RWKV_HD, RWKV_W), f32), pltpu.VMEM((nb,) + sh_tail, f32),
                        pltpu.VMEM((nb * rows, RWKV_W), f32)],
        compiler_params=pltpu.CompilerParams(dimension_semantics=("parallel", "arbitrary"),
                                             vmem_limit_bytes=V7X_VMEM_LIMIT_BYTES),
        name="rwkv7",
    )(p, prev0, h0, *params, ones_bd)


def _rwkv_step_kernel(p_ref, shift0_ref, s_ref, mu_ref, w0_ref, w2_ref, a0_ref, a2_ref, g2_ref, kk_ref, ka_ref,
                      rk_ref, lng_ref, lnb_ref, ones_ref, o_ref, s_out_ref, shift_ref,
                      feat_scr, y_scr, g_scr, rkr_scr, v_scr, *, steps, batch):
    h = pl.program_id(0)
    T, B, N = steps, batch, RWKV_HD
    key_tiles = N // 8
    f_r, f_w, f_k, f_v, f_a, f_b = range(6)

    @pl.when(h == 0)
    def _():
        p = p_ref[...]
        p_prev = jnp.concatenate([shift0_ref[...], p[:(T - 1) * B]], axis=0)
        r, kp, v, kk, a, ld, g = _rwkv_features(p, p_prev, mu_ref, w0_ref, w2_ref, a0_ref, a2_ref, g2_ref,
                                                kk_ref, ka_ref, ones_ref[...])
        for idx, x in enumerate((r, jnp.exp(ld), kp, v, -kk, kk * a)):
            for t in range(T):
                feat_scr[idx, t] = x[t * B:(t + 1) * B, :].T
        g_scr[...] = g
        rkr_scr[...] = r * kp * rk_ref[...]
        v_scr[...] = v
        shift_ref[...] = p[(T - 1) * B:]

    base = pl.multiple_of(h * N, N)

    def tile(idx, t, kt):
        return feat_scr[idx, t, pl.ds(pl.multiple_of(base + 8 * kt, 8), 8), :]

    def keysum(s, idx, t):
        acc = s[0] * tile(idx, t, 0)
        for kt in range(1, key_tiles):
            acc = acc + s[kt] * tile(idx, t, kt)
        return jnp.sum(acc, axis=0, keepdims=True)

    def value_group(vg, carry):
        v_rows = pl.ds(pl.multiple_of(base + 8 * vg, 8), 8)
        v_tiles = [feat_scr[f_v, t, v_rows, :] for t in range(T)]
        y_rows = [[] for _ in range(T)]
        for i in range(8):
            row0 = pl.multiple_of((8 * vg + i) * N, N)
            s = [s_ref[0, pl.ds(row0 + 8 * kt, 8), :] for kt in range(key_tiles)]
            for t in range(T):
                sa = jnp.broadcast_to(keysum(s, f_a, t), (8, B))
                vv = jnp.broadcast_to(v_tiles[t][i:i + 1, :], (8, B))
                s = [s[kt] * tile(f_w, t, kt) + sa * tile(f_b, t, kt) + vv * tile(f_k, t, kt)
                     for kt in range(key_tiles)]
                y_rows[t].append(keysum(s, f_r, t))
            for kt in range(key_tiles):
                s_out_ref[0, pl.ds(row0 + 8 * kt, 8), :] = s[kt]
        for t in range(T):
            y_scr[t, v_rows, :] = jnp.concatenate(y_rows[t], axis=0)
        return carry

    lax.fori_loop(0, N // 8, value_group, 0)

    @pl.when(h == pl.num_programs(0) - 1)
    def _():
        y = jnp.concatenate([y_scr[t].T for t in range(T)], axis=0)
        out = _rwkv_output(y, (rkr_scr[...], v_scr[...]), g_scr[...], lng_ref, lnb_ref, ones_ref[...])
        o_ref[...] = out.astype(o_ref.dtype)


def _rwkv_step(p, shift0, state, params, ones_bd, *, steps):
    B = state.shape[0]
    rows = steps * B
    n_state = RWKV_HD * RWKV_HD
    s_in = jnp.transpose(state, (1, 2, 3, 0)).reshape(RWKV_HEADS, n_state, B)
    s_spec = pl.BlockSpec((1, n_state, B), lambda h: (h, 0, 0))
    o, s_out, shift = pl.pallas_call(
        functools.partial(_rwkv_step_kernel, steps=steps, batch=B),
        grid=(RWKV_HEADS,),
        in_specs=[_resident((rows, SHIFT_W)), _resident((B, SHIFT_W)), s_spec]
                 + [_resident(x.shape) for x in params] + [_resident(ones_bd.shape)],
        out_specs=[pl.BlockSpec((rows, RWKV_W), lambda h: (0, 0)), s_spec,
                   pl.BlockSpec((B, SHIFT_W), lambda h: (0, 0))],
        out_shape=[jax.ShapeDtypeStruct((rows, RWKV_W), bf16),
                   jax.ShapeDtypeStruct((RWKV_HEADS, n_state, B), f32),
                   jax.ShapeDtypeStruct((B, SHIFT_W), f32)],
        scratch_shapes=[pltpu.VMEM((6, steps, RWKV_W, B), f32), pltpu.VMEM((steps, RWKV_W, B), f32),
                        pltpu.VMEM((rows, RWKV_W), f32), pltpu.VMEM((rows, RWKV_W), f32),
                        pltpu.VMEM((rows, RWKV_W), f32)],
        compiler_params=pltpu.CompilerParams(dimension_semantics=("arbitrary",),
                                             vmem_limit_bytes=V7X_VMEM_LIMIT_BYTES),
        name="rwkv7_step",
    )(p, shift0, s_in, *params, ones_bd)
    s_out = jnp.transpose(s_out.reshape(RWKV_HEADS, RWKV_HD, RWKV_HD, B), (3, 0, 1, 2))
    return o, s_out, shift


def _merge_ffn_kernel(h_ref, oret_ref, orwkv_ref, gate_ref, wr_ref, ww_ref, wo_ref,
                      g_ref, wg_ref, wu_ref, wd_ref, fin_ref, o_ref):
    a = _dot(oret_ref[...], wr_ref[...])
    b = _dot(orwkv_ref[...], ww_ref[...])
    merged = gate_ref[:, :D_MODEL] * a + gate_ref[:, D_MODEL:] * b
    h = h_ref[...] + _dot(merged.astype(bf16), wo_ref[...])
    h = _swiglu_half_step(h, g_ref, _column_chunks(wg_ref), _column_chunks(wu_ref), _row_chunks(wd_ref))
    o_ref[...] = _rms(h, fin_ref[...])


def _merge_ffn(h, o_ret, o_rwkv, gates, w_out_ret, w_out_rwkv, w_out, norm_g, wg, wu, wd, fin_g, *, tm):
    rows = h.shape[0]

    def rowspec(width):
        return pl.BlockSpec((tm, width), lambda i: (i, 0))

    return pl.pallas_call(
        _merge_ffn_kernel,
        grid=(rows // tm,),
        in_specs=[rowspec(D_MODEL), rowspec(RET_V), rowspec(RWKV_W), rowspec(GATE_W),
                  _resident((RET_V, D_MODEL)), _resident((RWKV_W, D_MODEL)), _resident((D_MODEL, D_MODEL))]
                 + [_resident(s) for s in _FFN_WEIGHT_SPECS] + [_resident((1, D_MODEL))],
        out_specs=rowspec(D_MODEL),
        out_shape=jax.ShapeDtypeStruct((rows, D_MODEL), f32),
        compiler_params=pltpu.CompilerParams(dimension_semantics=("parallel",),
                                             vmem_limit_bytes=V7X_VMEM_LIMIT_BYTES),
        name="merge_ffn",
    )(h, o_ret, o_rwkv, gates, w_out_ret, w_out_rwkv, w_out, norm_g, wg, wu, wd, fin_g)


def _rotary_tables(pos):
    half = RET_DK // 2
    inv_freq = ROPE_BASE ** (-jnp.arange(half, dtype=f32) / half)
    ang = pos.astype(f32)[:, None] * inv_freq[None, :]
    cos = jnp.cos(ang)
    sin = jnp.sin(ang)
    cos_t = jnp.tile(jnp.concatenate([cos, cos], axis=1), (1, RET_HEADS))
    sin_t = jnp.tile(jnp.concatenate([-sin, sin], axis=1), (1, RET_HEADS))
    return cos_t, sin_t


def _row_tile(rows, target):
    tm = min(rows, target)
    while rows % tm:
        tm -= 8
    return tm


def kernel(x_prompt, x_sample, state_ret, state_wkv, state_shift, meta_tokens, ffn1_norm, ffn1_w_gate, ffn1_w_up, ffn1_w_down, mix_norm, w_in, ret_gn_g, mu_shift, w0, w2, a0, a2, g2, k_k, k_a, r_k, lnx_g, lnx_b, w_out_ret, w_out_rwkv, w_out, ffn2_norm, ffn2_w_gate, ffn2_w_up, ffn2_w_down, final_norm):
    Bp, Tp, _ = x_prompt.shape
    Bs, Ts, _ = x_sample.shape
    st_dtype = state_ret.dtype

    def row(x):
        return x.reshape(1, -1).astype(f32)

    fin = row(final_norm)
    mix_g = row(mix_norm)
    wr_b, ww_b, wo_b = w_out_ret.astype(bf16), w_out_rwkv.astype(bf16), w_out.astype(bf16)
    rwkv_params = (row(mu_shift), row(w0), w2.astype(bf16), row(a0), a2.astype(bf16), g2.astype(bf16),
                   row(k_k), row(k_a), row(r_k), row(lnx_g), row(lnx_b))
    head_id = jnp.arange(RWKV_W, dtype=jnp.int32) // RWKV_HD
    ones_bd = (head_id[:, None] == head_id[None, :]).astype(bf16)
    gn = row(ret_gn_g)

    n_s = Bs * Ts
    x_small = jnp.concatenate([jnp.swapaxes(x_sample, 0, 1).reshape(n_s, D_MODEL),
                               meta_tokens.astype(x_sample.dtype)], axis=0)
    cos_s, sin_s = _rotary_tables(PAST_LEN + jnp.arange(Ts, dtype=jnp.int32))
    cos_m, sin_m = _rotary_tables(jnp.arange(N_META, dtype=jnp.int32))
    cos_small = jnp.concatenate([jnp.repeat(cos_s, Bs, axis=0), cos_m], axis=0)
    sin_small = jnp.concatenate([jnp.repeat(sin_s, Bs, axis=0), sin_m], axis=0)
    n_small = n_s + N_META
    h_small, *ffn1_b = _ffn_stream(x_small, row(ffn1_norm), ffn1_w_gate, ffn1_w_up, ffn1_w_down)

    n_p = Bp * Tp
    h_p, w_in_b = _ffn(x_prompt.reshape(n_p, D_MODEL), row(ffn1_norm), *ffn1_b, tm=_row_tile(n_p, 512),
                       cast_w_in=w_in)
    q_s, k_s, v_s, sg_s, p_s, gate_s = _inproj(h_small, mix_g, w_in_b, cos_small, sin_small, tm=n_small)

    def meta(x):
        return x[n_s:].reshape(1, N_META, -1)

    zeros_ret = jnp.zeros((1, RET_HEADS, RET_DK, RET_DV), f32)
    _, s_ret_m = _retention(meta(q_s), meta(k_s), meta(v_s), meta(sg_s), zeros_ret, gn,
                            nb=1, rows=N_META, valid=N_META)
    _, h_wkv_m, shift_m = _rwkv(meta(p_s), jnp.zeros((1, 1, SHIFT_W), f32),
                                jnp.zeros((1, RWKV_HEADS, RWKV_HD, RWKV_HD), f32), rwkv_params, ones_bd,
                                nb=1, rows=N_META, valid=N_META)

    pad_t = -(-Ts // 8) * 8

    def samp(x):
        x = jnp.swapaxes(x[:n_s].reshape(Ts, Bs, -1), 0, 1)
        return jnp.pad(x, ((0, 0), (0, pad_t - Ts), (0, 0)))

    o_ret_s, s_ret_s = _retention(samp(q_s), samp(k_s), samp(v_s), samp(sg_s), state_ret.astype(f32), gn,
                                  nb=RET_SAMPLE_SEQS, rows=pad_t, valid=Ts)
    o_ret_s = jnp.swapaxes(o_ret_s[:, :Ts], 0, 1).reshape(n_s, RET_V)
    o_wkv_s, h_wkv_s, shift_s = _rwkv_step(p_s[:n_s], state_shift.astype(f32), state_wkv.astype(f32),
                                           rwkv_params, ones_bd, steps=Ts)

    cos_p, sin_p = _rotary_tables(N_META + jnp.arange(Tp, dtype=jnp.int32))
    q_p, k_p, v_p, sg_p, p_p, gate_p, *ffn2_b = _inproj(h_p, mix_g, w_in_b, cos_p, sin_p, tm=_row_tile(Tp, 512),
                                                        cast_ffn=(ffn2_w_gate, ffn2_w_up, ffn2_w_down))

    def post(h, o_ret, o_rwkv, gates, tm):
        return _merge_ffn(h, o_ret, o_rwkv, gates, wr_b, ww_b, wo_b, row(ffn2_norm), *ffn2_b, fin, tm=tm)

    y_sample = post(h_small[:n_s], o_ret_s, o_wkv_s, gate_s[:n_s], _row_tile(n_s, 512))
    y_sample = jnp.swapaxes(y_sample.reshape(Ts, Bs, D_MODEL), 0, 1)

    def seqs(x):
        return x.reshape(Bp, Tp, -1)

    o_ret_p, s_ret_p = _retention(seqs(q_p), seqs(k_p), seqs(v_p), seqs(sg_p), s_ret_m, gn,
                                  nb=RET_PROMPT_SEQS, rows=RET_CHUNK, valid=RET_CHUNK)
    o_wkv_p, h_wkv_p, shift_p = _rwkv(seqs(p_p), shift_m, h_wkv_m, rwkv_params, ones_bd,
                                      nb=RWKV_PROMPT_SEQS, rows=RWKV_CHUNK, valid=RWKV_CHUNK)
    y_prompt = post(h_p, o_ret_p.reshape(n_p, RET_V), o_wkv_p.reshape(n_p, RWKV_W), gate_p, _row_tile(n_p, 512))

    return (y_prompt.reshape(Bp, Tp, D_MODEL).astype(x_prompt.dtype),
            y_sample.reshape(Bs, Ts, D_MODEL).astype(x_sample.dtype),
            s_ret_p.astype(st_dtype), h_wkv_p.astype(st_dtype),
            shift_p.reshape(Bp, SHIFT_W).astype(st_dtype),
            s_ret_s.astype(st_dtype), h_wkv_s.astype(st_dtype),
            shift_s.reshape(Bs, SHIFT_W).astype(st_dtype))
```

```python
import functools

import numpy as np
import jax
import jax.numpy as jnp
from jax import lax
from jax.experimental import pallas as pl
from jax.experimental.pallas import tpu as pltpu

D_MODEL = 1024
N_META = 16
PAST_LEN = 16384
RET_HEADS = 4
RET_DK = 64
RET_DV = 128
RET_CHUNK = 128
RWKV_HEADS = 8
RWKV_HD = 64
RWKV_W = RWKV_HEADS * RWKV_HD
DECAY_LORA = 64
AAA_LORA = 64
GATE_LORA = 128
D_FF = 2816
ROPE_BASE = 10000.0
NORM_EPS = 1e-6
RET_GN_EPS = 1e-6
RWKV_GN_EPS = 64e-5
RET_QK = RET_HEADS * RET_DK
RET_V = RET_HEADS * RET_DV
SHIFT_W = 3 * RWKV_W + DECAY_LORA + AAA_LORA + GATE_LORA
GATE_W = 2 * D_MODEL
PROJ_W = 2 * RET_QK + 2 * RET_V + SHIFT_W + GATE_W

_C_Q, _C_K, _C_V, _C_G = 0, RET_QK, 2 * RET_QK, 2 * RET_QK + RET_V
_C_P = 2 * RET_QK + 2 * RET_V
_C_GATE = _C_P + SHIFT_W

V7X_VMEM_LIMIT_BYTES = 56 * 1024 * 1024
FF_CHUNK = 256
RWKV_CHUNK = 64
RWKV_LANE_HEADS = 2
RWKV_PROMPT_SEQS = 8
RET_PROMPT_SEQS = 8
RET_SAMPLE_SEQS = 32
LOG_GAMMA = tuple(float(np.log1p(-2.0 ** (-5.0 - h))) for h in range(RET_HEADS))

f32 = jnp.float32
bf16 = jnp.bfloat16


def _resident(shape):
    zeros = (0,) * len(shape)
    return pl.BlockSpec(shape, lambda *_: zeros, pipeline_mode=pl.Buffered(1))


def _rms(x, g):
    return x * lax.rsqrt(jnp.mean(x * x, axis=-1, keepdims=True) + NORM_EPS) * g


def _dot(a, b):
    return jnp.dot(a, b, preferred_element_type=f32)


def _dot_nt(a, b):
    return lax.dot_general(a, b, (((1,), (1,)), ((), ())), preferred_element_type=f32)


def _dot_tn(a, b):
    return lax.dot_general(a, b, (((0,), (0,)), ((), ())), preferred_element_type=f32)


FF_CHUNKS = D_FF // FF_CHUNK


def _swiglu_half_step(x, g_ref, gate, up, down):
    xn = _rms(x, g_ref[...]).astype(bf16)
    acc = jnp.zeros(x.shape, f32)
    for c in range(FF_CHUNKS):
        gt = _dot(xn, gate(c))
        ut = _dot(xn, up(c))
        act = (gt * jax.nn.sigmoid(gt) * ut).astype(bf16)
        acc = acc + _dot(act, down(c))
    return x + 0.5 * acc


def _column_chunks(w_ref):
    return lambda c: w_ref[:, c * FF_CHUNK:(c + 1) * FF_CHUNK]


def _row_chunks(w_ref):
    return lambda c: w_ref[c * FF_CHUNK:(c + 1) * FF_CHUNK, :]


def _ffn_kernel(x_ref, g_ref, wg_ref, wu_ref, wd_ref, o_ref):
    o_ref[...] = _swiglu_half_step(x_ref[...], g_ref, _column_chunks(wg_ref), _column_chunks(wu_ref),
                                   _row_chunks(wd_ref))


_FFN_WEIGHT_SPECS = ((1, D_MODEL), (D_MODEL, D_FF), (D_MODEL, D_FF), (D_FF, D_MODEL))


def _ffn_chunk_specs(step_of):
    col = pl.BlockSpec((D_MODEL, FF_CHUNK), lambda i: (0, step_of(i)))
    return [col, col, pl.BlockSpec((FF_CHUNK, D_MODEL), lambda i: (step_of(i), 0))]


def _ffn_bf16_shapes():
    return [jax.ShapeDtypeStruct(s, bf16) for s in _FFN_WEIGHT_SPECS[1:]]


def _ffn_stream_kernel(x_ref, g_ref, wg_f, wu_f, wd_f, o_ref, wg_o, wu_o, wd_o, xn_s, acc_s):
    c = pl.program_id(0)

    @pl.when(c == 0)
    def _():
        xn_s[...] = _rms(x_ref[...], g_ref[...]).astype(bf16)
        acc_s[...] = jnp.zeros(acc_s.shape, f32)

    wg = wg_f[...].astype(bf16)
    wu = wu_f[...].astype(bf16)
    wd = wd_f[...].astype(bf16)
    wg_o[...] = wg
    wu_o[...] = wu
    wd_o[...] = wd
    xn = xn_s[...]
    gt = _dot(xn, wg)
    ut = _dot(xn, wu)
    act = (gt * jax.nn.sigmoid(gt) * ut).astype(bf16)
    acc_s[...] = acc_s[...] + _dot(act, wd)

    @pl.when(c == pl.num_programs(0) - 1)
    def _():
        o_ref[...] = x_ref[...] + 0.5 * acc_s[...]


def _ffn_stream(x, norm_g, wg, wu, wd):
    rows = x.shape[0]
    weight_chunks = _ffn_chunk_specs(lambda c: c)
    whole = pl.BlockSpec((rows, D_MODEL), lambda c: (0, 0))
    return pl.pallas_call(
        _ffn_stream_kernel,
        grid=(FF_CHUNKS,),
        in_specs=[_resident((rows, D_MODEL)), _resident((1, D_MODEL))] + weight_chunks,
        out_specs=[whole] + weight_chunks,
        out_shape=[jax.ShapeDtypeStruct((rows, D_MODEL), f32)] + _ffn_bf16_shapes(),
        scratch_shapes=[pltpu.VMEM((rows, D_MODEL), bf16), pltpu.VMEM((rows, D_MODEL), f32)],
        compiler_params=pltpu.CompilerParams(dimension_semantics=("arbitrary",),
                                             vmem_limit_bytes=V7X_VMEM_LIMIT_BYTES),
        name="ffn_stream",
    )(x, norm_g, wg, wu, wd)


def _ffn_side_kernel(x_ref, g_ref, wg_ref, wu_ref, wd_ref, win_f, o_ref, win_o, *, in_chunks):
    _ffn_kernel(x_ref, g_ref, wg_ref, wu_ref, wd_ref, o_ref)

    @pl.when(pl.program_id(0) < in_chunks)
    def _():
        win_o[...] = win_f[...].astype(bf16)


def _ffn(x, norm_g, wg, wu, wd, *, tm, cast_w_in=None):
    rows = x.shape[0]
    tiles = rows // tm
    row = pl.BlockSpec((tm, D_MODEL), lambda i: (i, 0))
    in_specs = [row] + [_resident(s) for s in _FFN_WEIGHT_SPECS]
    if cast_w_in is None:
        return pl.pallas_call(
            _ffn_kernel,
            grid=(tiles,),
            in_specs=in_specs,
            out_specs=row,
            out_shape=jax.ShapeDtypeStruct((rows, D_MODEL), f32),
            compiler_params=pltpu.CompilerParams(dimension_semantics=("parallel",),
                                                 vmem_limit_bytes=V7X_VMEM_LIMIT_BYTES),
            name="ffn",
        )(x, norm_g, wg, wu, wd)
    in_chunks = PROJ_W // FF_CHUNK
    assert in_chunks <= tiles
    win = pl.BlockSpec((D_MODEL, FF_CHUNK), lambda i: (0, jnp.minimum(i, in_chunks - 1)))
    return pl.pallas_call(
        functools.partial(_ffn_side_kernel, in_chunks=in_chunks),
        grid=(tiles,),
        in_specs=in_specs + [win],
        out_specs=[row, win],
        out_shape=[jax.ShapeDtypeStruct((rows, D_MODEL), f32), jax.ShapeDtypeStruct((D_MODEL, PROJ_W), bf16)],
        compiler_params=pltpu.CompilerParams(dimension_semantics=("arbitrary",),
                                             vmem_limit_bytes=V7X_VMEM_LIMIT_BYTES),
        name="ffn",
    )(x, norm_g, wg, wu, wd, cast_w_in)


def _swap_halves(x):
    parts = []
    for j in range(x.shape[1] // 128):
        xs = x[:, 128 * j:128 * (j + 1)]
        fwd = pltpu.roll(xs, 32, 1)
        bwd = pltpu.roll(xs, 96, 1)
        lane = lax.broadcasted_iota(jnp.int32, xs.shape, 1)
        parts.append(jnp.where((lane % RET_DK) < RET_DK // 2, bwd, fwd))
    return jnp.concatenate(parts, axis=1)


def _inproj_cast_kernel(h_ref, g_ref, w_ref, cos_ref, sin_ref, wg_f, wu_f, wd_f,
                        q_ref, k_ref, v_ref, sg_ref, p_ref, gate_ref, wg_o, wu_o, wd_o):
    _inproj_kernel(h_ref, g_ref, w_ref, cos_ref, sin_ref, q_ref, k_ref, v_ref, sg_ref, p_ref, gate_ref)

    @pl.when(pl.program_id(0) < FF_CHUNKS)
    def _():
        for w_f, w_o in ((wg_f, wg_o), (wu_f, wu_o), (wd_f, wd_o)):
            w_o[...] = w_f[...].astype(bf16)


def _inproj_kernel(h_ref, g_ref, w_ref, cos_ref, sin_ref, q_ref, k_ref, v_ref, sg_ref, p_ref, gate_ref):
    un = _rms(h_ref[...], g_ref[...]).astype(bf16)
    gate_ref[...] = jax.nn.sigmoid(_dot(un, w_ref[:, _C_GATE:PROJ_W]))
    gr = _dot(un, w_ref[:, _C_G:_C_P])
    sg_ref[...] = gr * jax.nn.sigmoid(gr)
    cos = cos_ref[...]
    sin = sin_ref[...]
    q = _dot(un, w_ref[:, _C_Q:_C_K])
    q_ref[...] = q * cos + _swap_halves(q) * sin
    k = _dot(un, w_ref[:, _C_K:_C_V])
    k_ref[...] = (k * cos + _swap_halves(k) * sin) * (RET_DK ** -0.5)
    v_ref[...] = _dot(un, w_ref[:, _C_V:_C_G]).astype(v_ref.dtype)
    p_ref[...] = _dot(un, w_ref[:, _C_P:_C_GATE])


def _inproj(h, norm_g, w_in, cos, sin, *, tm, cast_ffn=None):
    rows = h.shape[0]
    tiles = rows // tm
    tab_blocks = cos.shape[0] // tm

    def rowspec(width):
        return pl.BlockSpec((tm, width), lambda i: (i, 0))

    tab = pl.BlockSpec((tm, RET_QK), lambda i: (i % tab_blocks, 0))
    widths = (RET_QK, RET_QK, RET_V, RET_V, SHIFT_W, GATE_W)
    dtypes = (f32, f32, bf16, f32, f32, f32)
    in_specs = [rowspec(D_MODEL), _resident((1, D_MODEL)), _resident((D_MODEL, PROJ_W)), tab, tab]
    out_specs = [rowspec(w) for w in widths]
    out_shape = [jax.ShapeDtypeStruct((rows, w), d) for w, d in zip(widths, dtypes)]
    operands = (h, norm_g, w_in, cos, sin)
    if cast_ffn is not None:
        assert FF_CHUNKS <= tiles
        weight_chunks = _ffn_chunk_specs(lambda i: jnp.minimum(i, FF_CHUNKS - 1))
        in_specs += weight_chunks
        out_specs += weight_chunks
        out_shape += _ffn_bf16_shapes()
        operands += tuple(cast_ffn)
    return pl.pallas_call(
        _inproj_kernel if cast_ffn is None else _inproj_cast_kernel,
        grid=(tiles,),
        in_specs=in_specs,
        out_specs=out_specs,
        out_shape=out_shape,
        compiler_params=pltpu.CompilerParams(dimension_semantics=("arbitrary",),
                                             vmem_limit_bytes=V7X_VMEM_LIMIT_BYTES),
        name="inproj",
    )(*operands)


def _ret_kernel(q_ref, k_ref, v_ref, sg_ref, s0_ref, gn_ref, o_ref, s_out_ref, s_scr, o_scr, *, nb, rows, valid):
    c = pl.program_id(1)
    L = rows

    @pl.when(c == 0)
    def _():
        s_scr[...] = jnp.broadcast_to(s0_ref[...], s_scr.shape)

    ii = lax.broadcasted_iota(jnp.int32, (L, L), 0)
    jj = lax.broadcasted_iota(jnp.int32, (L, L), 1)
    diff = (ii - jj).astype(f32)
    row = lax.broadcasted_iota(jnp.int32, (L, 1), 0).astype(f32)
    mask = [jnp.where(diff >= 0, jnp.exp(lg * jnp.maximum(diff, 0.0)), 0.0) for lg in LOG_GAMMA]
    q_decay = [jnp.exp(lg * (row + 1.0)) for lg in LOG_GAMMA]
    k_decay = [jnp.exp(lg * (valid - 1.0 - row)) for lg in LOG_GAMMA]
    s_decay = [float(np.exp(lg * valid)) for lg in LOG_GAMMA]

    chains = [(j, h) for j in range(nb) for h in range(RET_HEADS)]
    qh = [q_ref[j, :, RET_DK * h:RET_DK * (h + 1)] for j, h in chains]
    kh = [k_ref[j, :, RET_DK * h:RET_DK * (h + 1)] for j, h in chains]
    vh = [v_ref[j, :, RET_DV * h:RET_DV * (h + 1)] for j, h in chains]
    scores = [(_dot_nt(qh[i].astype(bf16), kh[i].astype(bf16)) * mask[h]).astype(bf16)
              for i, (j, h) in enumerate(chains)]
    qd = [(qh[i] * q_decay[h]).astype(bf16) for i, (j, h) in enumerate(chains)]
    kd = [(kh[i] * k_decay[h]).astype(bf16) for i, (j, h) in enumerate(chains)]
    s_old = [s_scr[j, h] for j, h in chains]
    for i, (j, h) in enumerate(chains):
        o_scr[j * L:(j + 1) * L, RET_DV * h:RET_DV * (h + 1)] = (
            _dot(scores[i], vh[i]) + _dot(qd[i], s_old[i].astype(bf16)))
    for i, (j, h) in enumerate(chains):
        s_scr[j, h] = s_decay[h] * s_old[i] + _dot_tn(kd[i], vh[i])

    gn = gn_ref[...]
    sg = sg_ref[...].reshape(nb * L, RET_V)
    for h in range(RET_HEADS):
        sl = slice(RET_DV * h, RET_DV * (h + 1))
        o = o_scr[:, sl]
        mu = jnp.mean(o, axis=-1, keepdims=True)
        oc = o - mu
        var = jnp.mean(oc * oc, axis=-1, keepdims=True)
        out = oc * lax.rsqrt(var + RET_GN_EPS) * gn[:, sl] * sg[:, sl]
        o_ref[:, :, sl] = out.reshape(nb, L, RET_DV).astype(o_ref.dtype)

    @pl.when(c == pl.num_programs(1) - 1)
    def _():
        s_out_ref[...] = s_scr[...]


def _retention(q, k, v, sg, s0, gn, *, nb, rows, valid):
    B, T, _ = q.shape
    bcast = s0.shape[0] == 1

    def seq(width):
        return pl.BlockSpec((nb, rows, width), lambda b, c: (b, c, 0))

    state = pl.BlockSpec((nb, RET_HEADS, RET_DK, RET_DV), lambda b, c: (b, 0, 0, 0))
    state_in = pl.BlockSpec((1, RET_HEADS, RET_DK, RET_DV), lambda b, c: (0, 0, 0, 0)) if bcast else state
    return pl.pallas_call(
        functools.partial(_ret_kernel, nb=nb, rows=rows, valid=valid),
        grid=(B // nb, T // rows),
        in_specs=[seq(RET_QK), seq(RET_QK), seq(RET_V), seq(RET_V), state_in, _resident((1, RET_V))],
        out_specs=[seq(RET_V), state],
        out_shape=[jax.ShapeDtypeStruct((B, T, RET_V), bf16),
                   jax.ShapeDtypeStruct((B, RET_HEADS, RET_DK, RET_DV), f32)],
        scratch_shapes=[pltpu.VMEM((nb, RET_HEADS, RET_DK, RET_DV), f32), pltpu.VMEM((nb * rows, RET_V), f32)],
        compiler_params=pltpu.CompilerParams(dimension_semantics=("parallel", "arbitrary"),
                                             vmem_limit_bytes=V7X_VMEM_LIMIT_BYTES),
        name="retention",
    )(q, k, v, sg, s0, gn)


def _rwkv_features(p, p_prev, mu_ref, w0_ref, w2_ref, a0_ref, a2_ref, g2_ref, kk_ref, ka_ref, ones_bd):
    pm = p + (p_prev - p) * mu_ref[...]
    r = pm[:, 0:RWKV_W]
    k = pm[:, RWKV_W:2 * RWKV_W]
    v = pm[:, 2 * RWKV_W:3 * RWKV_W]
    o_w = 3 * RWKV_W
    xw = pm[:, o_w:o_w + DECAY_LORA]
    xa = pm[:, o_w + DECAY_LORA:o_w + DECAY_LORA + AAA_LORA]
    xg = pm[:, o_w + DECAY_LORA + AAA_LORA:SHIFT_W]
    z = w0_ref[...] + _dot(jnp.tanh(xw).astype(bf16), w2_ref[...])
    ld = -float(np.exp(-0.5)) * jax.nn.sigmoid(z)
    a = jax.nn.sigmoid(a0_ref[...] + _dot(xa.astype(bf16), a2_ref[...]))
    g = _dot(jax.nn.sigmoid(xg).astype(bf16), g2_ref[...])
    kk = k * kk_ref[...]
    kk = kk * lax.rsqrt(jnp.maximum(_dot((kk * kk).astype(bf16), ones_bd), 1e-24))
    kp = k * (1.0 + (a - 1.0) * ka_ref[...])
    return r, kp, v, kk, a, ld, g


def _rwkv_output(y, bonus_rkv, g, lng_ref, lnb_ref, ones_bd):
    inv_n = 1.0 / RWKV_HD
    mean = _dot(y.astype(bf16), ones_bd) * inv_n
    yc = y - mean
    var = _dot((yc * yc).astype(bf16), ones_bd) * inv_n
    out = yc * lax.rsqrt(var + RWKV_GN_EPS) * lng_ref[...] + lnb_ref[...]
    rkr, v = bonus_rkv
    bonus = _dot(rkr.astype(bf16), ones_bd) * v
    return (out + bonus) * g


def _rwkv_kernel(p_ref, prev0_ref, h0_ref, mu_ref, w0_ref, w2_ref, a0_ref, a2_ref, g2_ref, kk_ref, ka_ref,
                 rk_ref, lng_ref, lnb_ref, ones_ref, o_ref, h_out_ref, shift_ref, s_scr, prev_scr, y_scr,
                 *, nb, rows, valid):
    c = pl.program_id(1)
    C = rows
    N = RWKV_HD
    R = nb * C
    assert C & (C - 1) == 0
    log2c = C.bit_length() - 1

    @pl.when(c == 0)
    def _():
        for j in range(nb):
            for h in range(RWKV_HEADS):
                s_scr[j, :, N * h:N * (h + 1)] = h0_ref[j if h0_ref.shape[0] > 1 else 0, h]
        prev_scr[...] = jnp.broadcast_to(prev0_ref[...], prev_scr.shape)

    p = p_ref[...].reshape(R, SHIFT_W)
    rowid = lax.broadcasted_iota(jnp.int32, (R, 1), 0)
    step = rowid & (C - 1)
    p_prev = pltpu.roll(p, 1, 0)
    for j in range(nb):
        p_prev = jnp.where(rowid == j * C, prev_scr[j], p_prev)
    for j in range(nb):
        prev_scr[j] = p[j * C + valid - 1:j * C + valid, :]
    ones_bd = ones_ref[...]
    r, kp, v, kk, a, ld, g = _rwkv_features(p, p_prev, mu_ref, w0_ref, w2_ref, a0_ref, a2_ref, g2_ref, kk_ref,
                                            ka_ref, ones_bd)
    if valid < C:
        live = (step < valid).astype(f32)
        ld = ld * live
        kk = kk * live
        kp = kp * live
        v = v * live
    b = kk * a

    cum = ld
    for level in range(log2c):
        reach = 1 << level
        cum = cum + jnp.where(step >= reach, pltpu.roll(cum, reach, 0), 0.0)
    last_rows = [cum[j * C + C - 1:j * C + C, :] for j in range(nb)]
    cum_last = jnp.concatenate([jnp.broadcast_to(x, (C, RWKV_W)) for x in last_rows], axis=0)
    e_in = jnp.exp(cum)
    e_ex = jnp.exp(cum - ld)
    e_neg = jnp.exp(-cum)
    e_end = jnp.exp(cum_last - cum)
    g_end = [jnp.exp(x) for x in last_rows]
    at = -kk * e_ex
    rt = r * e_in
    bt = (b * e_neg).astype(bf16)
    kt = (kp * e_neg).astype(bf16)
    bh = (b * e_end).astype(bf16)
    kh = (kp * e_end).astype(bf16)
    vb = v.astype(bf16)

    G = RWKV_LANE_HEADS
    GW = G * N
    log2n = N.bit_length() - 1
    def head_masks(width, log2_block):
        lane = lax.broadcasted_iota(jnp.int32, (1, width), 1)
        return [((lane >> log2_block) & (G - 1)) == h for h in range(G)]

    head_of_lane = head_masks(GW, log2n)
    head_of_lane2 = head_masks(2 * GW, log2n)
    head_of_col = head_masks(G * C, log2c)
    head_of_col2 = head_masks(2 * G * C, log2c)
    ti = lax.broadcasted_iota(jnp.int32, (C, G * C), 0)
    tj = lax.broadcasted_iota(jnp.int32, (C, G * C), 1) & (C - 1)
    incl = ti >= tj
    strict = ti > tj
    unit = ti == tj
    gi = lax.broadcasted_iota(jnp.int32, (GW, GW), 0)
    gj = lax.broadcasted_iota(jnp.int32, (GW, GW), 1)
    eye_g = gi == gj
    same_head = (gi >> log2n) == (gj >> log2n)

    def bd(a, masks):
        return jnp.concatenate([jnp.where(m, a, jnp.zeros_like(a)) for m in masks], axis=0)

    groups = [(j, q) for j in range(nb) for q in range(RWKV_HEADS // G)]
    n_gr = len(groups)

    def grp(x, j, q):
        return x[j * C:(j + 1) * C, GW * q:GW * (q + 1)]

    ar = [jnp.concatenate([grp(at, j, q), grp(rt, j, q)], axis=0).astype(bf16) for j, q in groups]
    gbk = [_dot_nt(ar[i], jnp.concatenate([bd(grp(bt, j, q), head_of_lane), bd(grp(kt, j, q), head_of_lane)],
                                          axis=0)) for i, (j, q) in enumerate(groups)]
    gb = [x[:, :G * C] for x in gbk]
    gk = [x[:, G * C:] for x in gbk]
    lab = [jnp.where(strict, x[:C], 0.0) for x in gb]
    mrb = [jnp.where(incl, x[C:], 0.0).astype(bf16) for x in gb]
    lmk = [jnp.concatenate([jnp.where(strict, x[:C], 0.0), jnp.where(incl, x[C:], 0.0)], axis=0).astype(bf16)
           for x in gk]
    lmv = [_dot(lmk[i], bd(grp(vb, j, q), head_of_lane)) for i, (j, q) in enumerate(groups)]
    m = [_dot(x.astype(bf16), bd(x.astype(bf16), head_of_col)) for x in lab]
    t_inv = [jnp.where(unit, 1.0, x) for x in lab]
    for level in range(1, log2c):
        mb = [x.astype(bf16) for x in m]
        if level < log2c - 1:
            out = [_dot(mb[i], bd(jnp.concatenate([t_inv[i].astype(bf16), mb[i]], axis=1), head_of_col2))
                   for i in range(n_gr)]
            t_inv = [t_inv[i] + out[i][:, :G * C] for i in range(n_gr)]
            m = [x[:, G * C:] for x in out]
        else:
            t_inv = [t_inv[i] + _dot(mb[i], bd(t_inv[i].astype(bf16), head_of_col)) for i in range(n_gr)]
    wu = [_dot(t_inv[i].astype(bf16),
               bd(jnp.concatenate([grp(at, j, q).astype(bf16), lmv[i][:C].astype(bf16)], axis=1), head_of_lane2)
               ).astype(bf16) for i, (j, q) in enumerate(groups)]
    wa = [x[:, :GW] for x in wu]
    uv = [x[:, GW:] for x in wu]
    pq = [_dot(mrb[i], bd(wu[i], head_of_lane2)) for i in range(n_gr)]
    ry = [grp(rt, j, q) + pq[i][:, :GW] for i, (j, q) in enumerate(groups)]
    y0 = [lmv[i][C:] + pq[i][:, GW:] for i in range(n_gr)]
    gwt = [_dot_tn(wa[i], grp(bh, j, q)) for i, (j, q) in enumerate(groups)]
    dkf = [_dot_tn(jnp.concatenate([uv[i], grp(vb, j, q)], axis=0),
                   jnp.concatenate([grp(bh, j, q), grp(kh, j, q)], axis=0))
           for i, (j, q) in enumerate(groups)]
    for i, (j, q) in enumerate(groups):
        sl = slice(GW * q, GW * (q + 1))
        wg = jnp.where(same_head, gwt[i], 0.0) + jnp.where(eye_g, g_end[j][:, sl], 0.0)
        dk = jnp.where(head_of_lane[0], dkf[i][0:N], 0.0)
        for h in range(1, G):
            dk = dk + jnp.where(head_of_lane[h], dkf[i][N * h:N * (h + 1)], 0.0)
        s_old = s_scr[j, :, sl].astype(bf16)
        y_scr[j * C:(j + 1) * C, sl] = _dot_nt(ry[i].astype(bf16), bd(s_old, head_of_lane)) + y0[i]
        s_scr[j, :, sl] = _dot(s_old, wg.astype(bf16)) + dk

    out = _rwkv_output(y_scr[...], (r * kp * rk_ref[...], v), g, lng_ref, lnb_ref, ones_bd)
    o_ref[...] = out.reshape(nb, C, RWKV_W).astype(o_ref.dtype)

    @pl.when(c == pl.num_programs(1) - 1)
    def _():
        for j in range(nb):
            for h in range(RWKV_HEADS):
                h_out_ref[j, h] = s_scr[j, :, N * h:N * (h + 1)]
        shift_ref[...] = prev_scr[...]


def _rwkv(p, prev0, h0, params, ones_bd, *, nb, rows, valid):
    B, T, _ = p.shape

    def maybe_bcast(arr, tail):
        nd = len(tail) + 1
        if arr.shape[0] == 1:
            return pl.BlockSpec((1,) + tail, lambda b, c: (0,) * nd)
        return pl.BlockSpec((nb,) + tail, lambda b, c: (b,) + (0,) * (nd - 1))

    seq_in = pl.BlockSpec((nb, rows, SHIFT_W), lambda b, c: (b, c, 0))
    seq_out = pl.BlockSpec((nb, rows, RWKV_W), lambda b, c: (b, c, 0))
    st_tail = (RWKV_HEADS, RWKV_HD, RWKV_HD)
    sh_tail = (1, SHIFT_W)
    param_specs = [_resident(x.shape) for x in params]
    return pl.pallas_call(
        functools.partial(_rwkv_kernel, nb=nb, rows=rows, valid=valid),
        grid=(B // nb, T // rows),
        in_specs=[seq_in, maybe_bcast(prev0, sh_tail), maybe_bcast(h0, st_tail)] + param_specs
                 + [_resident(ones_bd.shape)],
        out_specs=[seq_out, pl.BlockSpec((nb,) + st_tail, lambda b, c: (b, 0, 0, 0)),
                   pl.BlockSpec((nb,) + sh_tail, lambda b, c: (b, 0, 0))],
        out_shape=[jax.ShapeDtypeStruct((B, T, RWKV_W), bf16),
                   jax.ShapeDtypeStruct((B,) + st_tail, f32),
                   jax.ShapeDtypeStruct((B,) + sh_tail, f32)],
        scratch_shapes=[pltpu.VMEM((nb, RWKV_HD, RWKV_W), f32), pltpu.VMEM((nb,) + sh_tail, f32),
                        pltpu.VMEM((nb * rows, RWKV_W), f32)],
        compiler_params=pltpu.CompilerParams(dimension_semantics=("parallel", "arbitrary"),
                                             vmem_limit_bytes=V7X_VMEM_LIMIT_BYTES),
        name="rwkv7",
    )(p, prev0, h0, *params, ones_bd)


def _rwkv_step_kernel(p_ref, shift0_ref, s_ref, mu_ref, w0_ref, w2_ref, a0_ref, a2_ref, g2_ref, kk_ref, ka_ref,
                      rk_ref, lng_ref, lnb_ref, ones_ref, o_ref, s_out_ref, shift_ref,
                      feat_scr, y_scr, g_scr, rkr_scr, v_scr, *, steps, batch):
    h = pl.program_id(0)
    T, B, N = steps, batch, RWKV_HD
    key_tiles = N // 8
    f_r, f_w, f_k, f_v, f_a, f_b = range(6)

    @pl.when(h == 0)
    def _():
        p = p_ref[...]
        p_prev = jnp.concatenate([shift0_ref[...], p[:(T - 1) * B]], axis=0)
        r, kp, v, kk, a, ld, g = _rwkv_features(p, p_prev, mu_ref, w0_ref, w2_ref, a0_ref, a2_ref, g2_ref,
                                                kk_ref, ka_ref, ones_ref[...])
        for idx, x in enumerate((r, jnp.exp(ld), kp, v, -kk, kk * a)):
            for t in range(T):
                feat_scr[idx, t] = x[t * B:(t + 1) * B, :].T
        g_scr[...] = g
        rkr_scr[...] = r * kp * rk_ref[...]
        v_scr[...] = v
        shift_ref[...] = p[(T - 1) * B:]

    base = pl.multiple_of(h * N, N)

    def tile(idx, t, kt):
        return feat_scr[idx, t, pl.ds(pl.multiple_of(base + 8 * kt, 8), 8), :]

    def keysum(s, idx, t):
        acc = s[0] * tile(idx, t, 0)
        for kt in range(1, key_tiles):
            acc = acc + s[kt] * tile(idx, t, kt)
        return jnp.sum(acc, axis=0, keepdims=True)

    def value_group(vg, carry):
        v_rows = pl.ds(pl.multiple_of(base + 8 * vg, 8), 8)
        v_tiles = [feat_scr[f_v, t, v_rows, :] for t in range(T)]
        y_rows = [[] for _ in range(T)]
        for i in range(8):
            row0 = pl.multiple_of((8 * vg + i) * N, N)
            s = [s_ref[0, pl.ds(row0 + 8 * kt, 8), :] for kt in range(key_tiles)]
            for t in range(T):
                sa = jnp.broadcast_to(keysum(s, f_a, t), (8, B))
                vv = jnp.broadcast_to(v_tiles[t][i:i + 1, :], (8, B))
                s = [s[kt] * tile(f_w, t, kt) + sa * tile(f_b, t, kt) + vv * tile(f_k, t, kt)
                     for kt in range(key_tiles)]
                y_rows[t].append(keysum(s, f_r, t))
            for kt in range(key_tiles):
                s_out_ref[0, pl.ds(row0 + 8 * kt, 8), :] = s[kt]
        for t in range(T):
            y_scr[t, v_rows, :] = jnp.concatenate(y_rows[t], axis=0)
        return carry

    lax.fori_loop(0, N // 8, value_group, 0)

    @pl.when(h == pl.num_programs(0) - 1)
    def _():
        y = jnp.concatenate([y_scr[t].T for t in range(T)], axis=0)
        out = _rwkv_output(y, (rkr_scr[...], v_scr[...]), g_scr[...], lng_ref, lnb_ref, ones_ref[...])
        o_ref[...] = out.astype(o_ref.dtype)


def _rwkv_step(p, shift0, state, params, ones_bd, *, steps):
    B = state.shape[0]
    rows = steps * B
    n_state = RWKV_HD * RWKV_HD
    s_in = jnp.transpose(state, (1, 2, 3, 0)).reshape(RWKV_HEADS, n_state, B)
    s_spec = pl.BlockSpec((1, n_state, B), lambda h: (h, 0, 0))
    o, s_out, shift = pl.pallas_call(
        functools.partial(_rwkv_step_kernel, steps=steps, batch=B),
        grid=(RWKV_HEADS,),
        in_specs=[_resident((rows, SHIFT_W)), _resident((B, SHIFT_W)), s_spec]
                 + [_resident(x.shape) for x in params] + [_resident(ones_bd.shape)],
        out_specs=[pl.BlockSpec((rows, RWKV_W), lambda h: (0, 0)), s_spec,
                   pl.BlockSpec((B, SHIFT_W), lambda h: (0, 0))],
        out_shape=[jax.ShapeDtypeStruct((rows, RWKV_W), bf16),
                   jax.ShapeDtypeStruct((RWKV_HEADS, n_state, B), f32),
                   jax.ShapeDtypeStruct((B, SHIFT_W), f32)],
        scratch_shapes=[pltpu.VMEM((6, steps, RWKV_W, B), f32), pltpu.VMEM((steps, RWKV_W, B), f32),
                        pltpu.VMEM((rows, RWKV_W), f32), pltpu.VMEM((rows, RWKV_W), f32),
                        pltpu.VMEM((rows, RWKV_W), f32)],
        compiler_params=pltpu.CompilerParams(dimension_semantics=("arbitrary",),
                                             vmem_limit_bytes=V7X_VMEM_LIMIT_BYTES),
        name="rwkv7_step",
    )(p, shift0, s_in, *params, ones_bd)
    s_out = jnp.transpose(s_out.reshape(RWKV_HEADS, RWKV_HD, RWKV_HD, B), (3, 0, 1, 2))
    return o, s_out, shift


def _merge_ffn_kernel(h_ref, oret_ref, orwkv_ref, gate_ref, wr_ref, ww_ref, wo_ref,
                      g_ref, wg_ref, wu_ref, wd_ref, fin_ref, o_ref):
    a = _dot(oret_ref[...], wr_ref[...])
    b = _dot(orwkv_ref[...], ww_ref[...])
    merged = gate_ref[:, :D_MODEL] * a + gate_ref[:, D_MODEL:] * b
    h = h_ref[...] + _dot(merged.astype(bf16), wo_ref[...])
    h = _swiglu_half_step(h, g_ref, _column_chunks(wg_ref), _column_chunks(wu_ref), _row_chunks(wd_ref))
    o_ref[...] = _rms(h, fin_ref[...])


def _merge_ffn(h, o_ret, o_rwkv, gates, w_out_ret, w_out_rwkv, w_out, norm_g, wg, wu, wd, fin_g, *, tm):
    rows = h.shape[0]

    def rowspec(width):
        return pl.BlockSpec((tm, width), lambda i: (i, 0))

    return pl.pallas_call(
        _merge_ffn_kernel,
        grid=(rows // tm,),
        in_specs=[rowspec(D_MODEL), rowspec(RET_V), rowspec(RWKV_W), rowspec(GATE_W),
                  _resident((RET_V, D_MODEL)), _resident((RWKV_W, D_MODEL)), _resident((D_MODEL, D_MODEL))]
                 + [_resident(s) for s in _FFN_WEIGHT_SPECS] + [_resident((1, D_MODEL))],
        out_specs=rowspec(D_MODEL),
        out_shape=jax.ShapeDtypeStruct((rows, D_MODEL), f32),
        compiler_params=pltpu.CompilerParams(dimension_semantics=("parallel",),
                                             vmem_limit_bytes=V7X_VMEM_LIMIT_BYTES),
        name="merge_ffn",
    )(h, o_ret, o_rwkv, gates, w_out_ret, w_out_rwkv, w_out, norm_g, wg, wu, wd, fin_g)


def _rotary_tables(pos):
    half = RET_DK // 2
    inv_freq = ROPE_BASE ** (-jnp.arange(half, dtype=f32) / half)
    ang = pos.astype(f32)[:, None] * inv_freq[None, :]
    cos = jnp.cos(ang)
    sin = jnp.sin(ang)
    cos_t = jnp.tile(jnp.concatenate([cos, cos], axis=1), (1, RET_HEADS))
    sin_t = jnp.tile(jnp.concatenate([-sin, sin], axis=1), (1, RET_HEADS))
    return cos_t, sin_t


def _row_tile(rows, target):
    tm = min(rows, target)
    while rows % tm:
        tm -= 8
    return tm


def kernel(x_prompt, x_sample, state_ret, state_wkv, state_shift, meta_tokens, ffn1_norm, ffn1_w_gate, ffn1_w_up, ffn1_w_down, mix_norm, w_in, ret_gn_g, mu_shift, w0, w2, a0, a2, g2, k_k, k_a, r_k, lnx_g, lnx_b, w_out_ret, w_out_rwkv, w_out, ffn2_norm, ffn2_w_gate, ffn2_w_up, ffn2_w_down, final_norm):
    Bp, Tp, _ = x_prompt.shape
    Bs, Ts, _ = x_sample.shape
    st_dtype = state_ret.dtype

    def row(x):
        return x.reshape(1, -1).astype(f32)

    fin = row(final_norm)
    mix_g = row(mix_norm)
    wr_b, ww_b, wo_b = w_out_ret.astype(bf16), w_out_rwkv.astype(bf16), w_out.astype(bf16)
    rwkv_params = (row(mu_shift), row(w0), w2.astype(bf16), row(a0), a2.astype(bf16), g2.astype(bf16),
                   row(k_k), row(k_a), row(r_k), row(lnx_g), row(lnx_b))
    head_id = jnp.arange(RWKV_W, dtype=jnp.int32) // RWKV_HD
    ones_bd = (head_id[:, None] == head_id[None, :]).astype(bf16)
    gn = row(ret_gn_g)

    n_s = Bs * Ts
    x_small = jnp.concatenate([jnp.swapaxes(x_sample, 0, 1).reshape(n_s, D_MODEL),
                               meta_tokens.astype(x_sample.dtype)], axis=0)
    cos_s, sin_s = _rotary_tables(PAST_LEN + jnp.arange(Ts, dtype=jnp.int32))
    cos_m, sin_m = _rotary_tables(jnp.arange(N_META, dtype=jnp.int32))
    cos_small = jnp.concatenate([jnp.repeat(cos_s, Bs, axis=0), cos_m], axis=0)
    sin_small = jnp.concatenate([jnp.repeat(sin_s, Bs, axis=0), sin_m], axis=0)
    n_small = n_s + N_META
    h_small, *ffn1_b = _ffn_stream(x_small, row(ffn1_norm), ffn1_w_gate, ffn1_w_up, ffn1_w_down)

    n_p = Bp * Tp
    h_p, w_in_b = _ffn(x_prompt.reshape(n_p, D_MODEL), row(ffn1_norm), *ffn1_b, tm=_row_tile(n_p, 512),
                       cast_w_in=w_in)
    q_s, k_s, v_s, sg_s, p_s, gate_s = _inproj(h_small, mix_g, w_in_b, cos_small, sin_small, tm=n_small)

    def meta(x):
        return x[n_s:].reshape(1, N_META, -1)

    zeros_ret = jnp.zeros((1, RET_HEADS, RET_DK, RET_DV), f32)
    _, s_ret_m = _retention(meta(q_s), meta(k_s), meta(v_s), meta(sg_s), zeros_ret, gn,
                            nb=1, rows=N_META, valid=N_META)
    _, h_wkv_m, shift_m = _rwkv(meta(p_s), jnp.zeros((1, 1, SHIFT_W), f32),
                                jnp.zeros((1, RWKV_HEADS, RWKV_HD, RWKV_HD), f32), rwkv_params, ones_bd,
                                nb=1, rows=N_META, valid=N_META)

    pad_t = -(-Ts // 8) * 8

    def samp(x):
        x = jnp.swapaxes(x[:n_s].reshape(Ts, Bs, -1), 0, 1)
        return jnp.pad(x, ((0, 0), (0, pad_t - Ts), (0, 0)))

    o_ret_s, s_ret_s = _retention(samp(q_s), samp(k_s), samp(v_s), samp(sg_s), state_ret.astype(f32), gn,
                                  nb=RET_SAMPLE_SEQS, rows=pad_t, valid=Ts)
    o_ret_s = jnp.swapaxes(o_ret_s[:, :Ts], 0, 1).reshape(n_s, RET_V)
    o_wkv_s, h_wkv_s, shift_s = _rwkv_step(p_s[:n_s], state_shift.astype(f32), state_wkv.astype(f32),
                                           rwkv_params, ones_bd, steps=Ts)

    cos_p, sin_p = _rotary_tables(N_META + jnp.arange(Tp, dtype=jnp.int32))
    q_p, k_p, v_p, sg_p, p_p, gate_p, *ffn2_b = _inproj(h_p, mix_g, w_in_b, cos_p, sin_p, tm=_row_tile(Tp, 512),
                                                        cast_ffn=(ffn2_w_gate, ffn2_w_up, ffn2_w_down))

    def post(h, o_ret, o_rwkv, gates, tm):
        return _merge_ffn(h, o_ret, o_rwkv, gates, wr_b, ww_b, wo_b, row(ffn2_norm), *ffn2_b, fin, tm=tm)

    y_sample = post(h_small[:n_s], o_ret_s, o_wkv_s, gate_s[:n_s], _row_tile(n_s, 512))
    y_sample = jnp.swapaxes(y_sample.reshape(Ts, Bs, D_MODEL), 0, 1)

    def seqs(x):
        return x.reshape(Bp, Tp, -1)

    o_ret_p, s_ret_p = _retention(seqs(q_p), seqs(k_p), seqs(v_p), seqs(sg_p), s_ret_m, gn,
                                  nb=RET_PROMPT_SEQS, rows=RET_CHUNK, valid=RET_CHUNK)
    o_wkv_p, h_wkv_p, shift_p = _rwkv(seqs(p_p), shift_m, h_wkv_m, rwkv_params, ones_bd,
                                      nb=RWKV_PROMPT_SEQS, rows=RWKV_CHUNK, valid=RWKV_CHUNK)
    y_prompt = post(h_p, o_ret_p.reshape(n_p, RET_V), o_wkv_p.reshape(n_p, RWKV_W), gate_p, _row_tile(n_p, 512))

    return (y_prompt.reshape(Bp, Tp, D_MODEL).astype(x_prompt.dtype),
            y_sample.reshape(Bs, Ts, D_MODEL).astype(x_sample.dtype),
            s_ret_p.astype(st_dtype), h_wkv_p.astype(st_dtype),
            shift_p.reshape(Bp, SHIFT_W).astype(st_dtype),
            s_ret_s.astype(st_dtype), h_wkv_s.astype(st_dtype),
            shift_s.reshape(Bs, SHIFT_W).astype(st_dtype))
```

```python
import functools

import numpy as np
import jax
import jax.numpy as jnp
from jax import lax
from jax.experimental import pallas as pl
from jax.experimental.pallas import tpu as pltpu

D_MODEL = 1024
N_META = 16
PAST_LEN = 16384
RET_HEADS = 4
RET_DK = 64
RET_DV = 128
RET_CHUNK = 128
RWKV_HEADS = 8
RWKV_HD = 64
RWKV_W = RWKV_HEADS * RWKV_HD
DECAY_LORA = 64
AAA_LORA = 64
GATE_LORA = 128
D_FF = 2816
ROPE_BASE = 10000.0
NORM_EPS = 1e-6
RET_GN_EPS = 1e-6
RWKV_GN_EPS = 64e-5
RET_QK = RET_HEADS * RET_DK
RET_V = RET_HEADS * RET_DV
SHIFT_W = 3 * RWKV_W + DECAY_LORA + AAA_LORA + GATE_LORA
GATE_W = 2 * D_MODEL
PROJ_W = 2 * RET_QK + 2 * RET_V + SHIFT_W + GATE_W

_C_Q, _C_K, _C_V, _C_G = 0, RET_QK, 2 * RET_QK, 2 * RET_QK + RET_V
_C_P = 2 * RET_QK + 2 * RET_V
_C_GATE = _C_P + SHIFT_W

V7X_VMEM_LIMIT_BYTES = 56 * 1024 * 1024
FF_CHUNK = 256
RWKV_CHUNK = 64
RWKV_LANE_HEADS = 2
HEAD_SUM_LANES = 256
RWKV_PROMPT_SEQS = 8
RET_PROMPT_SEQS = 8
RET_SAMPLE_SEQS = 32
LOG_GAMMA = tuple(float(np.log1p(-2.0 ** (-5.0 - h))) for h in range(RET_HEADS))

f32 = jnp.float32
bf16 = jnp.bfloat16


def _resident(shape):
    zeros = (0,) * len(shape)
    return pl.BlockSpec(shape, lambda *_: zeros, pipeline_mode=pl.Buffered(1))


def _rms(x, g):
    return x * lax.rsqrt(jnp.mean(x * x, axis=-1, keepdims=True) + NORM_EPS) * g


def _dot(a, b):
    return jnp.dot(a, b, preferred_element_type=f32)


def _dot_nt(a, b):
    return lax.dot_general(a, b, (((1,), (1,)), ((), ())), preferred_element_type=f32)


def _dot_tn(a, b):
    return lax.dot_general(a, b, (((0,), (0,)), ((), ())), preferred_element_type=f32)


FF_CHUNKS = D_FF // FF_CHUNK


def _swiglu_half_step(x, g_ref, gate, up, down):
    xn = _rms(x, g_ref[...]).astype(bf16)
    acc = jnp.zeros(x.shape, f32)
    for c in range(FF_CHUNKS):
        gt = _dot(xn, gate(c))
        ut = _dot(xn, up(c))
        act = (gt * jax.nn.sigmoid(gt) * ut).astype(bf16)
        acc = acc + _dot(act, down(c))
    return x + 0.5 * acc


def _column_chunks(w_ref):
    return lambda c: w_ref[:, c * FF_CHUNK:(c + 1) * FF_CHUNK]


def _row_chunks(w_ref):
    return lambda c: w_ref[c * FF_CHUNK:(c + 1) * FF_CHUNK, :]


def _ffn_kernel(x_ref, g_ref, wg_ref, wu_ref, wd_ref, o_ref):
    o_ref[...] = _swiglu_half_step(x_ref[...], g_ref, _column_chunks(wg_ref), _column_chunks(wu_ref),
                                   _row_chunks(wd_ref))


_FFN_WEIGHT_SPECS = ((1, D_MODEL), (D_MODEL, D_FF), (D_MODEL, D_FF), (D_FF, D_MODEL))


def _ffn_chunk_specs(step_of):
    col = pl.BlockSpec((D_MODEL, FF_CHUNK), lambda i: (0, step_of(i)))
    return [col, col, pl.BlockSpec((FF_CHUNK, D_MODEL), lambda i: (step_of(i), 0))]


def _ffn_bf16_shapes():
    return [jax.ShapeDtypeStruct(s, bf16) for s in _FFN_WEIGHT_SPECS[1:]]


def _ffn_stream_kernel(x_ref, g_ref, wg_f, wu_f, wd_f, o_ref, wg_o, wu_o, wd_o, xn_s, acc_s):
    c = pl.program_id(0)

    @pl.when(c == 0)
    def _():
        xn_s[...] = _rms(x_ref[...], g_ref[...]).astype(bf16)
        acc_s[...] = jnp.zeros(acc_s.shape, f32)

    wg = wg_f[...].astype(bf16)
    wu = wu_f[...].astype(bf16)
    wd = wd_f[...].astype(bf16)
    wg_o[...] = wg
    wu_o[...] = wu
    wd_o[...] = wd
    xn = xn_s[...]
    gt = _dot(xn, wg)
    ut = _dot(xn, wu)
    act = (gt * jax.nn.sigmoid(gt) * ut).astype(bf16)
    acc_s[...] = acc_s[...] + _dot(act, wd)

    @pl.when(c == pl.num_programs(0) - 1)
    def _():
        o_ref[...] = x_ref[...] + 0.5 * acc_s[...]


def _ffn_stream(x, norm_g, wg, wu, wd):
    rows = x.shape[0]
    weight_chunks = _ffn_chunk_specs(lambda c: c)
    whole = pl.BlockSpec((rows, D_MODEL), lambda c: (0, 0))
    return pl.pallas_call(
        _ffn_stream_kernel,
        grid=(FF_CHUNKS,),
        in_specs=[_resident((rows, D_MODEL)), _resident((1, D_MODEL))] + weight_chunks,
        out_specs=[whole] + weight_chunks,
        out_shape=[jax.ShapeDtypeStruct((rows, D_MODEL), f32)] + _ffn_bf16_shapes(),
        scratch_shapes=[pltpu.VMEM((rows, D_MODEL), bf16), pltpu.VMEM((rows, D_MODEL), f32)],
        compiler_params=pltpu.CompilerParams(dimension_semantics=("arbitrary",),
                                             vmem_limit_bytes=V7X_VMEM_LIMIT_BYTES),
        name="ffn_stream",
    )(x, norm_g, wg, wu, wd)


def _ffn_side_kernel(x_ref, g_ref, wg_ref, wu_ref, wd_ref, win_f, o_ref, win_o, *, in_chunks):
    _ffn_kernel(x_ref, g_ref, wg_ref, wu_ref, wd_ref, o_ref)

    @pl.when(pl.program_id(0) < in_chunks)
    def _():
        win_o[...] = win_f[...].astype(bf16)


def _ffn(x, norm_g, wg, wu, wd, *, tm, cast_w_in=None):
    rows = x.shape[0]
    tiles = rows // tm
    row = pl.BlockSpec((tm, D_MODEL), lambda i: (i, 0))
    in_specs = [row] + [_resident(s) for s in _FFN_WEIGHT_SPECS]
    if cast_w_in is None:
        return pl.pallas_call(
            _ffn_kernel,
            grid=(tiles,),
            in_specs=in_specs,
            out_specs=row,
            out_shape=jax.ShapeDtypeStruct((rows, D_MODEL), f32),
            compiler_params=pltpu.CompilerParams(dimension_semantics=("parallel",),
                                                 vmem_limit_bytes=V7X_VMEM_LIMIT_BYTES),
            name="ffn",
        )(x, norm_g, wg, wu, wd)
    in_chunks = PROJ_W // FF_CHUNK
    assert in_chunks <= tiles
    win = pl.BlockSpec((D_MODEL, FF_CHUNK), lambda i: (0, jnp.minimum(i, in_chunks - 1)))
    return pl.pallas_call(
        functools.partial(_ffn_side_kernel, in_chunks=in_chunks),
        grid=(tiles,),
        in_specs=in_specs + [win],
        out_specs=[row, win],
        out_shape=[jax.ShapeDtypeStruct((rows, D_MODEL), f32), jax.ShapeDtypeStruct((D_MODEL, PROJ_W), bf16)],
        compiler_params=pltpu.CompilerParams(dimension_semantics=("arbitrary",),
                                             vmem_limit_bytes=V7X_VMEM_LIMIT_BYTES),
        name="ffn",
    )(x, norm_g, wg, wu, wd, cast_w_in)


def _swap_halves(x):
    parts = []
    for j in range(x.shape[1] // 128):
        xs = x[:, 128 * j:128 * (j + 1)]
        fwd = pltpu.roll(xs, 32, 1)
        bwd = pltpu.roll(xs, 96, 1)
        lane = lax.broadcasted_iota(jnp.int32, xs.shape, 1)
        parts.append(jnp.where((lane % RET_DK) < RET_DK // 2, bwd, fwd))
    return jnp.concatenate(parts, axis=1)


def _inproj_cast_kernel(h_ref, g_ref, w_ref, cos_ref, sin_ref, wg_f, wu_f, wd_f,
                        q_ref, k_ref, v_ref, sg_ref, p_ref, gate_ref, wg_o, wu_o, wd_o):
    _inproj_kernel(h_ref, g_ref, w_ref, cos_ref, sin_ref, q_ref, k_ref, v_ref, sg_ref, p_ref, gate_ref)

    @pl.when(pl.program_id(0) < FF_CHUNKS)
    def _():
        for w_f, w_o in ((wg_f, wg_o), (wu_f, wu_o), (wd_f, wd_o)):
            w_o[...] = w_f[...].astype(bf16)


def _inproj_kernel(h_ref, g_ref, w_ref, cos_ref, sin_ref, q_ref, k_ref, v_ref, sg_ref, p_ref, gate_ref):
    un = _rms(h_ref[...], g_ref[...]).astype(bf16)
    gate_ref[...] = jax.nn.sigmoid(_dot(un, w_ref[:, _C_GATE:PROJ_W]))
    gr = _dot(un, w_ref[:, _C_G:_C_P])
    sg_ref[...] = gr * jax.nn.sigmoid(gr)
    cos = cos_ref[...]
    sin = sin_ref[...]
    q = _dot(un, w_ref[:, _C_Q:_C_K])
    q_ref[...] = q * cos + _swap_halves(q) * sin
    k = _dot(un, w_ref[:, _C_K:_C_V])
    k_ref[...] = (k * cos + _swap_halves(k) * sin) * (RET_DK ** -0.5)
    v_ref[...] = _dot(un, w_ref[:, _C_V:_C_G]).astype(v_ref.dtype)
    p_ref[...] = _dot(un, w_ref[:, _C_P:_C_GATE])


def _inproj(h, norm_g, w_in, cos, sin, *, tm, cast_ffn=None):
    rows = h.shape[0]
    tiles = rows // tm
    tab_blocks = cos.shape[0] // tm

    def rowspec(width):
        return pl.BlockSpec((tm, width), lambda i: (i, 0))

    tab = pl.BlockSpec((tm, RET_QK), lambda i: (i % tab_blocks, 0))
    widths = (RET_QK, RET_QK, RET_V, RET_V, SHIFT_W, GATE_W)
    dtypes = (f32, f32, bf16, f32, f32, f32)
    in_specs = [rowspec(D_MODEL), _resident((1, D_MODEL)), _resident((D_MODEL, PROJ_W)), tab, tab]
    out_specs = [rowspec(w) for w in widths]
    out_shape = [jax.ShapeDtypeStruct((rows, w), d) for w, d in zip(widths, dtypes)]
    operands = (h, norm_g, w_in, cos, sin)
    if cast_ffn is not None:
        assert FF_CHUNKS <= tiles
        weight_chunks = _ffn_chunk_specs(lambda i: jnp.minimum(i, FF_CHUNKS - 1))
        in_specs += weight_chunks
        out_specs += weight_chunks
        out_shape += _ffn_bf16_shapes()
        operands += tuple(cast_ffn)
    return pl.pallas_call(
        _inproj_kernel if cast_ffn is None else _inproj_cast_kernel,
        grid=(tiles,),
        in_specs=in_specs,
        out_specs=out_specs,
        out_shape=out_shape,
        compiler_params=pltpu.CompilerParams(dimension_semantics=("arbitrary",),
                                             vmem_limit_bytes=V7X_VMEM_LIMIT_BYTES),
        name="inproj",
    )(*operands)


def _ret_kernel(q_ref, k_ref, v_ref, sg_ref, s0_ref, gn_ref, o_ref, s_out_ref, s_scr, o_scr, *, nb, rows, valid):
    c = pl.program_id(1)
    L = rows

    @pl.when(c == 0)
    def _():
        s_scr[...] = jnp.broadcast_to(s0_ref[...], s_scr.shape)

    ii = lax.broadcasted_iota(jnp.int32, (L, L), 0)
    jj = lax.broadcasted_iota(jnp.int32, (L, L), 1)
    diff = (ii - jj).astype(f32)
    row = lax.broadcasted_iota(jnp.int32, (L, 1), 0).astype(f32)
    mask = [jnp.where(diff >= 0, jnp.exp(lg * jnp.maximum(diff, 0.0)), 0.0) for lg in LOG_GAMMA]
    q_decay = [jnp.exp(lg * (row + 1.0)) for lg in LOG_GAMMA]
    k_decay = [jnp.exp(lg * (valid - 1.0 - row)) for lg in LOG_GAMMA]
    s_decay = [float(np.exp(lg * valid)) for lg in LOG_GAMMA]

    chains = [(j, h) for j in range(nb) for h in range(RET_HEADS)]
    qh = [q_ref[j, :, RET_DK * h:RET_DK * (h + 1)] for j, h in chains]
    kh = [k_ref[j, :, RET_DK * h:RET_DK * (h + 1)] for j, h in chains]
    vh = [v_ref[j, :, RET_DV * h:RET_DV * (h + 1)] for j, h in chains]
    scores = [(_dot_nt(qh[i].astype(bf16), kh[i].astype(bf16)) * mask[h]).astype(bf16)
              for i, (j, h) in enumerate(chains)]
    qd = [(qh[i] * q_decay[h]).astype(bf16) for i, (j, h) in enumerate(chains)]
    kd = [(kh[i] * k_decay[h]).astype(bf16) for i, (j, h) in enumerate(chains)]
    s_old = [s_scr[j, h] for j, h in chains]
    for i, (j, h) in enumerate(chains):
        o_scr[j * L:(j + 1) * L, RET_DV * h:RET_DV * (h + 1)] = (
            _dot(scores[i], vh[i]) + _dot(qd[i], s_old[i].astype(bf16)))
    for i, (j, h) in enumerate(chains):
        s_scr[j, h] = s_decay[h] * s_old[i] + _dot_tn(kd[i], vh[i])

    gn = gn_ref[...]
    sg = sg_ref[...].reshape(nb * L, RET_V)
    for h in range(RET_HEADS):
        sl = slice(RET_DV * h, RET_DV * (h + 1))
        o = o_scr[:, sl]
        mu = jnp.mean(o, axis=-1, keepdims=True)
        oc = o - mu
        var = jnp.mean(oc * oc, axis=-1, keepdims=True)
        out = oc * lax.rsqrt(var + RET_GN_EPS) * gn[:, sl] * sg[:, sl]
        o_ref[:, :, sl] = out.reshape(nb, L, RET_DV).astype(o_ref.dtype)

    @pl.when(c == pl.num_programs(1) - 1)
    def _():
        s_out_ref[...] = s_scr[...]


def _retention(q, k, v, sg, s0, gn, *, nb, rows, valid):
    B, T, _ = q.shape
    bcast = s0.shape[0] == 1

    def seq(width):
        return pl.BlockSpec((nb, rows, width), lambda b, c: (b, c, 0))

    state = pl.BlockSpec((nb, RET_HEADS, RET_DK, RET_DV), lambda b, c: (b, 0, 0, 0))
    state_in = pl.BlockSpec((1, RET_HEADS, RET_DK, RET_DV), lambda b, c: (0, 0, 0, 0)) if bcast else state
    return pl.pallas_call(
        functools.partial(_ret_kernel, nb=nb, rows=rows, valid=valid),
        grid=(B // nb, T // rows),
        in_specs=[seq(RET_QK), seq(RET_QK), seq(RET_V), seq(RET_V), state_in, _resident((1, RET_V))],
        out_specs=[seq(RET_V), state],
        out_shape=[jax.ShapeDtypeStruct((B, T, RET_V), bf16),
                   jax.ShapeDtypeStruct((B, RET_HEADS, RET_DK, RET_DV), f32)],
        scratch_shapes=[pltpu.VMEM((nb, RET_HEADS, RET_DK, RET_DV), f32), pltpu.VMEM((nb * rows, RET_V), f32)],
        compiler_params=pltpu.CompilerParams(dimension_semantics=("parallel", "arbitrary"),
                                             vmem_limit_bytes=V7X_VMEM_LIMIT_BYTES),
        name="retention",
    )(q, k, v, sg, s0, gn)


def _head_sums(x, ones_bd):
    rows = x.shape[0]
    width = ones_bd.shape[0]
    groups = RWKV_W // width
    stacked = jnp.concatenate([x[:, width * i:width * (i + 1)] for i in range(groups)], axis=0)
    z = _dot(stacked.astype(bf16), ones_bd)
    return jnp.concatenate([z[rows * i:rows * (i + 1)] for i in range(groups)], axis=1)


def _rwkv_features(p, p_prev, mu_ref, w0_ref, w2_ref, a0_ref, a2_ref, g2_ref, kk_ref, ka_ref, ones_bd):
    pm = p + (p_prev - p) * mu_ref[...]
    r = pm[:, 0:RWKV_W]
    k = pm[:, RWKV_W:2 * RWKV_W]
    v = pm[:, 2 * RWKV_W:3 * RWKV_W]
    o_w = 3 * RWKV_W
    xw = pm[:, o_w:o_w + DECAY_LORA]
    xa = pm[:, o_w + DECAY_LORA:o_w + DECAY_LORA + AAA_LORA]
    xg = pm[:, o_w + DECAY_LORA + AAA_LORA:SHIFT_W]
    z = w0_ref[...] + _dot(jnp.tanh(xw).astype(bf16), w2_ref[...])
    ld = -float(np.exp(-0.5)) * jax.nn.sigmoid(z)
    a = jax.nn.sigmoid(a0_ref[...] + _dot(xa.astype(bf16), a2_ref[...]))
    g = _dot(jax.nn.sigmoid(xg).astype(bf16), g2_ref[...])
    kk = k * kk_ref[...]
    kk = kk * lax.rsqrt(jnp.maximum(_head_sums(kk * kk, ones_bd), 1e-24))
    kp = k * (1.0 + (a - 1.0) * ka_ref[...])
    return r, kp, v, kk, a, ld, g


def _rwkv_output(y, bonus_rkv, g, lng_ref, lnb_ref, ones_bd):
    inv_n = 1.0 / RWKV_HD
    mean = _head_sums(y, ones_bd) * inv_n
    yc = y - mean
    var = _head_sums(yc * yc, ones_bd) * inv_n
    out = yc * lax.rsqrt(var + RWKV_GN_EPS) * lng_ref[...] + lnb_ref[...]
    rkr, v = bonus_rkv
    bonus = _head_sums(rkr, ones_bd) * v
    return (out + bonus) * g


def _rwkv_kernel(p_ref, prev0_ref, h0_ref, mu_ref, w0_ref, w2_ref, a0_ref, a2_ref, g2_ref, kk_ref, ka_ref,
                 rk_ref, lng_ref, lnb_ref, ones_ref, o_ref, h_out_ref, shift_ref, s_scr, prev_scr, y_scr,
                 *, nb, rows, valid):
    c = pl.program_id(1)
    C = rows
    N = RWKV_HD
    R = nb * C
    assert C & (C - 1) == 0
    log2c = C.bit_length() - 1

    @pl.when(c == 0)
    def _():
        for j in range(nb):
            for h in range(RWKV_HEADS):
                s_scr[j, :, N * h:N * (h + 1)] = h0_ref[j if h0_ref.shape[0] > 1 else 0, h]
        prev_scr[...] = jnp.broadcast_to(prev0_ref[...], prev_scr.shape)

    p = p_ref[...].reshape(R, SHIFT_W)
    rowid = lax.broadcasted_iota(jnp.int32, (R, 1), 0)
    step = rowid & (C - 1)
    p_prev = pltpu.roll(p, 1, 0)
    for j in range(nb):
        p_prev = jnp.where(rowid == j * C, prev_scr[j], p_prev)
    for j in range(nb):
        prev_scr[j] = p[j * C + valid - 1:j * C + valid, :]
    ones_bd = ones_ref[...]
    r, kp, v, kk, a, ld, g = _rwkv_features(p, p_prev, mu_ref, w0_ref, w2_ref, a0_ref, a2_ref, g2_ref, kk_ref,
                                            ka_ref, ones_bd)
    if valid < C:
        live = (step < valid).astype(f32)
        ld = ld * live
        kk = kk * live
        kp = kp * live
        v = v * live
    b = kk * a

    cum = ld
    for level in range(log2c):
        reach = 1 << level
        cum = cum + jnp.where(step >= reach, pltpu.roll(cum, reach, 0), 0.0)
    last_rows = [cum[j * C + C - 1:j * C + C, :] for j in range(nb)]
    cum_last = jnp.concatenate([jnp.broadcast_to(x, (C, RWKV_W)) for x in last_rows], axis=0)
    e_in = jnp.exp(cum)
    e_ex = jnp.exp(cum - ld)
    e_neg = jnp.exp(-cum)
    e_end = jnp.exp(cum_last - cum)
    g_end = [jnp.exp(x) for x in last_rows]
    at = -kk * e_ex
    rt = r * e_in
    bt = (b * e_neg).astype(bf16)
    kt = (kp * e_neg).astype(bf16)
    bh = (b * e_end).astype(bf16)
    kh = (kp * e_end).astype(bf16)
    vb = v.astype(bf16)

    G = RWKV_LANE_HEADS
    GW = G * N
    log2n = N.bit_length() - 1
    def head_masks(width, log2_block):
        lane = lax.broadcasted_iota(jnp.int32, (1, width), 1)
        return [((lane >> log2_block) & (G - 1)) == h for h in range(G)]

    head_of_lane = head_masks(GW, log2n)
    head_of_lane2 = head_masks(2 * GW, log2n)
    head_of_col = head_masks(G * C, log2c)
    head_of_col2 = head_masks(2 * G * C, log2c)
    ti = lax.broadcasted_iota(jnp.int32, (C, G * C), 0)
    tj = lax.broadcasted_iota(jnp.int32, (C, G * C), 1) & (C - 1)
    incl = ti >= tj
    strict = ti > tj
    unit = ti == tj
    gi = lax.broadcasted_iota(jnp.int32, (GW, GW), 0)
    gj = lax.broadcasted_iota(jnp.int32, (GW, GW), 1)
    eye_g = gi == gj
    same_head = (gi >> log2n) == (gj >> log2n)

    def bd(a, masks):
        return jnp.concatenate([jnp.where(m, a, jnp.zeros_like(a)) for m in masks], axis=0)

    groups = [(j, q) for j in range(nb) for q in range(RWKV_HEADS // G)]
    n_gr = len(groups)

    def grp(x, j, q):
        return x[j * C:(j + 1) * C, GW * q:GW * (q + 1)]

    ar = [jnp.concatenate([grp(at, j, q), grp(rt, j, q)], axis=0).astype(bf16) for j, q in groups]
    gbk = [_dot_nt(ar[i], jnp.concatenate([bd(grp(bt, j, q), head_of_lane), bd(grp(kt, j, q), head_of_lane)],
                                          axis=0)) for i, (j, q) in enumerate(groups)]
    gb = [x[:, :G * C] for x in gbk]
    gk = [x[:, G * C:] for x in gbk]
    lab = [jnp.where(strict, x[:C], 0.0) for x in gb]
    mrb = [jnp.where(incl, x[C:], 0.0).astype(bf16) for x in gb]
    lmk = [jnp.concatenate([jnp.where(strict, x[:C], 0.0), jnp.where(incl, x[C:], 0.0)], axis=0).astype(bf16)
           for x in gk]
    lmv = [_dot(lmk[i], bd(grp(vb, j, q), head_of_lane)) for i, (j, q) in enumerate(groups)]
    m = [_dot(x.astype(bf16), bd(x.astype(bf16), head_of_col)) for x in lab]
    t_inv = [jnp.where(unit, 1.0, x) for x in lab]
    for level in range(1, log2c):
        mb = [x.astype(bf16) for x in m]
        if level < log2c - 1:
            out = [_dot(mb[i], bd(jnp.concatenate([t_inv[i].astype(bf16), mb[i]], axis=1), head_of_col2))
                   for i in range(n_gr)]
            t_inv = [t_inv[i] + out[i][:, :G * C] for i in range(n_gr)]
            m = [x[:, G * C:] for x in out]
        else:
            t_inv = [t_inv[i] + _dot(mb[i], bd(t_inv[i].astype(bf16), head_of_col)) for i in range(n_gr)]
    wu = [_dot(t_inv[i].astype(bf16),
               bd(jnp.concatenate([grp(at, j, q).astype(bf16), lmv[i][:C].astype(bf16)], axis=1), head_of_lane2)
               ).astype(bf16) for i, (j, q) in enumerate(groups)]
    wa = [x[:, :GW] for x in wu]
    uv = [x[:, GW:] for x in wu]
    pq = [_dot(mrb[i], bd(wu[i], head_of_lane2)) for i in range(n_gr)]
    ry = [grp(rt, j, q) + pq[i][:, :GW] for i, (j, q) in enumerate(groups)]
    y0 = [lmv[i][C:] + pq[i][:, GW:] for i in range(n_gr)]
    gwt = [_dot_tn(wa[i], grp(bh, j, q)) for i, (j, q) in enumerate(groups)]
    dkf = [_dot_tn(jnp.concatenate([uv[i], grp(vb, j, q)], axis=0),
                   jnp.concatenate([grp(bh, j, q), grp(kh, j, q)], axis=0))
           for i, (j, q) in enumerate(groups)]
    for i, (j, q) in enumerate(groups):
        sl = slice(GW * q, GW * (q + 1))
        wg = jnp.where(same_head, gwt[i], 0.0) + jnp.where(eye_g, g_end[j][:, sl], 0.0)
        dk = jnp.where(head_of_lane[0], dkf[i][0:N], 0.0)
        for h in range(1, G):
            dk = dk + jnp.where(head_of_lane[h], dkf[i][N * h:N * (h + 1)], 0.0)
        s_old = s_scr[j, :, sl].astype(bf16)
        y_scr[j * C:(j + 1) * C, sl] = _dot_nt(ry[i].astype(bf16), bd(s_old, head_of_lane)) + y0[i]
        s_scr[j, :, sl] = _dot(s_old, wg.astype(bf16)) + dk

    out = _rwkv_output(y_scr[...], (r * kp * rk_ref[...], v), g, lng_ref, lnb_ref, ones_bd)
    o_ref[...] = out.reshape(nb, C, RWKV_W).astype(o_ref.dtype)

    @pl.when(c == pl.num_programs(1) - 1)
    def _():
        for j in range(nb):
            for h in range(RWKV_HEADS):
                h_out_ref[j, h] = s_scr[j, :, N * h:N * (h + 1)]
        shift_ref[...] = prev_scr[...]


def _rwkv(p, prev0, h0, params, ones_bd, *, nb, rows, valid):
    B, T, _ = p.shape

    def maybe_bcast(arr, tail):
        nd = len(tail) + 1
        if arr.shape[0] == 1:
            return pl.BlockSpec((1,) + tail, lambda b, c: (0,) * nd)
        return pl.BlockSpec((nb,) + tail, lambda b, c: (b,) + (0,) * (nd - 1))

    seq_in = pl.BlockSpec((nb, rows, SHIFT_W), lambda b, c: (b, c, 0))
    seq_out = pl.BlockSpec((nb, rows, RWKV_W), lambda b, c: (b, c, 0))
    st_tail = (RWKV_HEADS, RWKV_HD, RWKV_HD)
    sh_tail = (1, SHIFT_W)
    param_specs = [_resident(x.shape) for x in params]
    return pl.pallas_call(
        functools.partial(_rwkv_kernel, nb=nb, rows=rows, valid=valid),
        grid=(B // nb, T // rows),
        in_specs=[seq_in, maybe_bcast(prev0, sh_tail), maybe_bcast(h0, st_tail)] + param_specs
                 + [_resident(ones_bd.shape)],
        out_specs=[seq_out, pl.BlockSpec((nb,) + st_tail, lambda b, c: (b, 0, 0, 0)),
                   pl.BlockSpec((nb,) + sh_tail, lambda b, c: (b, 0, 0))],
        out_shape=[jax.ShapeDtypeStruct((B, T, RWKV_W), bf16),
                   jax.ShapeDtypeStruct((B,) + st_tail, f32),
                   jax.ShapeDtypeStruct((B,) + sh_tail, f32)],
        scratch_shapes=[pltpu.VMEM((nb, RWKV_HD, RWKV_W), f32), pltpu.VMEM((nb,) + sh_tail, f32),
                        pltpu.VMEM((nb * rows, RWKV_W), f32)],
        compiler_params=pltpu.CompilerParams(dimension_semantics=("parallel", "arbitrary"),
                                             vmem_limit_bytes=V7X_VMEM_LIMIT_BYTES),
        name="rwkv7",
    )(p, prev0, h0, *params, ones_bd)


def _rwkv_step_kernel(p_ref, shift0_ref, s_ref, mu_ref, w0_ref, w2_ref, a0_ref, a2_ref, g2_ref, kk_ref, ka_ref,
                      rk_ref, lng_ref, lnb_ref, ones_ref, o_ref, s_out_ref, shift_ref,
                      feat_scr, y_scr, g_scr, rkr_scr, v_scr, *, steps, batch):
    h = pl.program_id(0)
    T, B, N = steps, batch, RWKV_HD
    key_tiles = N // 8
    f_r, f_w, f_k, f_v, f_a, f_b = range(6)

    @pl.when(h == 0)
    def _():
        p = p_ref[...]
        p_prev = jnp.concatenate([shift0_ref[...], p[:(T - 1) * B]], axis=0)
        r, kp, v, kk, a, ld, g = _rwkv_features(p, p_prev, mu_ref, w0_ref, w2_ref, a0_ref, a2_ref, g2_ref,
                                                kk_ref, ka_ref, ones_ref[...])
        for idx, x in enumerate((r, jnp.exp(ld), kp, v, -kk, kk * a)):
            for t in range(T):
                feat_scr[idx, t] = x[t * B:(t + 1) * B, :].T
        g_scr[...] = g
        rkr_scr[...] = r * kp * rk_ref[...]
        v_scr[...] = v
        shift_ref[...] = p[(T - 1) * B:]

    base = pl.multiple_of(h * N, N)

    def tile(idx, t, kt):
        return feat_scr[idx, t, pl.ds(pl.multiple_of(base + 8 * kt, 8), 8), :]

    def keysum(s, idx, t):
        acc = s[0] * tile(idx, t, 0)
        for kt in range(1, key_tiles):
            acc = acc + s[kt] * tile(idx, t, kt)
        return jnp.sum(acc, axis=0, keepdims=True)

    def value_group(vg, carry):
        v_rows = pl.ds(pl.multiple_of(base + 8 * vg, 8), 8)
        v_tiles = [feat_scr[f_v, t, v_rows, :] for t in range(T)]
        y_rows = [[] for _ in range(T)]
        for i in range(8):
            row0 = pl.multiple_of((8 * vg + i) * N, N)
            s = [s_ref[0, pl.ds(row0 + 8 * kt, 8), :] for kt in range(key_tiles)]
            for t in range(T):
                sa = jnp.broadcast_to(keysum(s, f_a, t), (8, B))
                vv = jnp.broadcast_to(v_tiles[t][i:i + 1, :], (8, B))
                s = [s[kt] * tile(f_w, t, kt) + sa * tile(f_b, t, kt) + vv * tile(f_k, t, kt)
                     for kt in range(key_tiles)]
                y_rows[t].append(keysum(s, f_r, t))
            for kt in range(key_tiles):
                s_out_ref[0, pl.ds(row0 + 8 * kt, 8), :] = s[kt]
        for t in range(T):
            y_scr[t, v_rows, :] = jnp.concatenate(y_rows[t], axis=0)
        return carry

    lax.fori_loop(0, N // 8, value_group, 0)

    @pl.when(h == pl.num_programs(0) - 1)
    def _():
        y = jnp.concatenate([y_scr[t].T for t in range(T)], axis=0)
        out = _rwkv_output(y, (rkr_scr[...], v_scr[...]), g_scr[...], lng_ref, lnb_ref, ones_ref[...])
        o_ref[...] = out.astype(o_ref.dtype)


def _rwkv_step(p, shift0, state, params, ones_bd, *, steps):
    B = state.shape[0]
    rows = steps * B
    n_state = RWKV_HD * RWKV_HD
    s_in = jnp.transpose(state, (1, 2, 3, 0)).reshape(RWKV_HEADS, n_state, B)
    s_spec = pl.BlockSpec((1, n_state, B), lambda h: (h, 0, 0))
    o, s_out, shift = pl.pallas_call(
        functools.partial(_rwkv_step_kernel, steps=steps, batch=B),
        grid=(RWKV_HEADS,),
        in_specs=[_resident((rows, SHIFT_W)), _resident((B, SHIFT_W)), s_spec]
                 + [_resident(x.shape) for x in params] + [_resident(ones_bd.shape)],
        out_specs=[pl.BlockSpec((rows, RWKV_W), lambda h: (0, 0)), s_spec,
                   pl.BlockSpec((B, SHIFT_W), lambda h: (0, 0))],
        out_shape=[jax.ShapeDtypeStruct((rows, RWKV_W), bf16),
                   jax.ShapeDtypeStruct((RWKV_HEADS, n_state, B), f32),
                   jax.ShapeDtypeStruct((B, SHIFT_W), f32)],
        scratch_shapes=[pltpu.VMEM((6, steps, RWKV_W, B), f32), pltpu.VMEM((steps, RWKV_W, B), f32),
                        pltpu.VMEM((rows, RWKV_W), f32), pltpu.VMEM((rows, RWKV_W), f32),
                        pltpu.VMEM((rows, RWKV_W), f32)],
        compiler_params=pltpu.CompilerParams(dimension_semantics=("arbitrary",),
                                             vmem_limit_bytes=V7X_VMEM_LIMIT_BYTES),
        name="rwkv7_step",
    )(p, shift0, s_in, *params, ones_bd)
    s_out = jnp.transpose(s_out.reshape(RWKV_HEADS, RWKV_HD, RWKV_HD, B), (3, 0, 1, 2))
    return o, s_out, shift


def _merge_ffn_kernel(h_ref, oret_ref, orwkv_ref, gate_ref, wr_ref, ww_ref, wo_ref,
                      g_ref, wg_ref, wu_ref, wd_ref, fin_ref, o_ref):
    a = _dot(oret_ref[...], wr_ref[...])
    b = _dot(orwkv_ref[...], ww_ref[...])
    merged = gate_ref[:, :D_MODEL] * a + gate_ref[:, D_MODEL:] * b
    h = h_ref[...] + _dot(merged.astype(bf16), wo_ref[...])
    h = _swiglu_half_step(h, g_ref, _column_chunks(wg_ref), _column_chunks(wu_ref), _row_chunks(wd_ref))
    o_ref[...] = _rms(h, fin_ref[...])


def _merge_ffn(h, o_ret, o_rwkv, gates, w_out_ret, w_out_rwkv, w_out, norm_g, wg, wu, wd, fin_g, *, tm):
    rows = h.shape[0]

    def rowspec(width):
        return pl.BlockSpec((tm, width), lambda i: (i, 0))

    return pl.pallas_call(
        _merge_ffn_kernel,
        grid=(rows // tm,),
        in_specs=[rowspec(D_MODEL), rowspec(RET_V), rowspec(RWKV_W), rowspec(GATE_W),
                  _resident((RET_V, D_MODEL)), _resident((RWKV_W, D_MODEL)), _resident((D_MODEL, D_MODEL))]
                 + [_resident(s) for s in _FFN_WEIGHT_SPECS] + [_resident((1, D_MODEL))],
        out_specs=rowspec(D_MODEL),
        out_shape=jax.ShapeDtypeStruct((rows, D_MODEL), f32),
        compiler_params=pltpu.CompilerParams(dimension_semantics=("parallel",),
                                             vmem_limit_bytes=V7X_VMEM_LIMIT_BYTES),
        name="merge_ffn",
    )(h, o_ret, o_rwkv, gates, w_out_ret, w_out_rwkv, w_out, norm_g, wg, wu, wd, fin_g)


def _rotary_tables(pos):
    half = RET_DK // 2
    inv_freq = ROPE_BASE ** (-jnp.arange(half, dtype=f32) / half)
    ang = pos.astype(f32)[:, None] * inv_freq[None, :]
    cos = jnp.cos(ang)
    sin = jnp.sin(ang)
    cos_t = jnp.tile(jnp.concatenate([cos, cos], axis=1), (1, RET_HEADS))
    sin_t = jnp.tile(jnp.concatenate([-sin, sin], axis=1), (1, RET_HEADS))
    return cos_t, sin_t


def _row_tile(rows, target):
    tm = min(rows, target)
    while rows % tm:
        tm -= 8
    return tm


def kernel(x_prompt, x_sample, state_ret, state_wkv, state_shift, meta_tokens, ffn1_norm, ffn1_w_gate, ffn1_w_up, ffn1_w_down, mix_norm, w_in, ret_gn_g, mu_shift, w0, w2, a0, a2, g2, k_k, k_a, r_k, lnx_g, lnx_b, w_out_ret, w_out_rwkv, w_out, ffn2_norm, ffn2_w_gate, ffn2_w_up, ffn2_w_down, final_norm):
    Bp, Tp, _ = x_prompt.shape
    Bs, Ts, _ = x_sample.shape
    st_dtype = state_ret.dtype

    def row(x):
        return x.reshape(1, -1).astype(f32)

    fin = row(final_norm)
    mix_g = row(mix_norm)
    wr_b, ww_b, wo_b = w_out_ret.astype(bf16), w_out_rwkv.astype(bf16), w_out.astype(bf16)
    rwkv_params = (row(mu_shift), row(w0), w2.astype(bf16), row(a0), a2.astype(bf16), g2.astype(bf16),
                   row(k_k), row(k_a), row(r_k), row(lnx_g), row(lnx_b))
    head_id = jnp.arange(HEAD_SUM_LANES, dtype=jnp.int32) // RWKV_HD
    ones_bd = (head_id[:, None] == head_id[None, :]).astype(bf16)
    gn = row(ret_gn_g)

    n_s = Bs * Ts
    x_small = jnp.concatenate([jnp.swapaxes(x_sample, 0, 1).reshape(n_s, D_MODEL),
                               meta_tokens.astype(x_sample.dtype)], axis=0)
    cos_s, sin_s = _rotary_tables(PAST_LEN + jnp.arange(Ts, dtype=jnp.int32))
    cos_m, sin_m = _rotary_tables(jnp.arange(N_META, dtype=jnp.int32))
    cos_small = jnp.concatenate([jnp.repeat(cos_s, Bs, axis=0), cos_m], axis=0)
    sin_small = jnp.concatenate([jnp.repeat(sin_s, Bs, axis=0), sin_m], axis=0)
    n_small = n_s + N_META
    h_small, *ffn1_b = _ffn_stream(x_small, row(ffn1_norm), ffn1_w_gate, ffn1_w_up, ffn1_w_down)

    n_p = Bp * Tp
    h_p, w_in_b = _ffn(x_prompt.reshape(n_p, D_MODEL), row(ffn1_norm), *ffn1_b, tm=_row_tile(n_p, 512),
                       cast_w_in=w_in)
    q_s, k_s, v_s, sg_s, p_s, gate_s = _inproj(h_small, mix_g, w_in_b, cos_small, sin_small, tm=n_small)

    def meta(x):
        return x[n_s:].reshape(1, N_META, -1)

    zeros_ret = jnp.zeros((1, RET_HEADS, RET_DK, RET_DV), f32)
    _, s_ret_m = _retention(meta(q_s), meta(k_s), meta(v_s), meta(sg_s), zeros_ret, gn,
                            nb=1, rows=N_META, valid=N_META)
    _, h_wkv_m, shift_m = _rwkv(meta(p_s), jnp.zeros((1, 1, SHIFT_W), f32),
                                jnp.zeros((1, RWKV_HEADS, RWKV_HD, RWKV_HD), f32), rwkv_params, ones_bd,
                                nb=1, rows=N_META, valid=N_META)

    pad_t = -(-Ts // 8) * 8

    def samp(x):
        x = jnp.swapaxes(x[:n_s].reshape(Ts, Bs, -1), 0, 1)
        return jnp.pad(x, ((0, 0), (0, pad_t - Ts), (0, 0)))

    o_ret_s, s_ret_s = _retention(samp(q_s), samp(k_s), samp(v_s), samp(sg_s), state_ret.astype(f32), gn,
                                  nb=RET_SAMPLE_SEQS, rows=pad_t, valid=Ts)
    o_ret_s = jnp.swapaxes(o_ret_s[:, :Ts], 0, 1).reshape(n_s, RET_V)
    o_wkv_s, h_wkv_s, shift_s = _rwkv_step(p_s[:n_s], state_shift.astype(f32), state_wkv.astype(f32),
                                           rwkv_params, ones_bd, steps=Ts)

    cos_p, sin_p = _rotary_tables(N_META + jnp.arange(Tp, dtype=jnp.int32))
    q_p, k_p, v_p, sg_p, p_p, gate_p, *ffn2_b = _inproj(h_p, mix_g, w_in_b, cos_p, sin_p, tm=_row_tile(Tp, 512),
                                                        cast_ffn=(ffn2_w_gate, ffn2_w_up, ffn2_w_down))

    def post(h, o_ret, o_rwkv, gates, tm):
        return _merge_ffn(h, o_ret, o_rwkv, gates, wr_b, ww_b, wo_b, row(ffn2_norm), *ffn2_b, fin, tm=tm)

    y_sample = post(h_small[:n_s], o_ret_s, o_wkv_s, gate_s[:n_s], _row_tile(n_s, 512))
    y_sample = jnp.swapaxes(y_sample.reshape(Ts, Bs, D_MODEL), 0, 1)

    def seqs(x):
        return x.reshape(Bp, Tp, -1)

    o_ret_p, s_ret_p = _retention(seqs(q_p), seqs(k_p), seqs(v_p), seqs(sg_p), s_ret_m, gn,
                                  nb=RET_PROMPT_SEQS, rows=RET_CHUNK, valid=RET_CHUNK)
    o_wkv_p, h_wkv_p, shift_p = _rwkv(seqs(p_p), shift_m, h_wkv_m, rwkv_params, ones_bd,
                                      nb=RWKV_PROMPT_SEQS, rows=RWKV_CHUNK, valid=RWKV_CHUNK)
    y_prompt = post(h_p, o_ret_p.reshape(n_p, RET_V), o_wkv_p.reshape(n_p, RWKV_W), gate_p, _row_tile(n_p, 512))

    return (y_prompt.reshape(Bp, Tp, D_MODEL).astype(x_prompt.dtype),
            y_sample.reshape(Bs, Ts, D_MODEL).astype(x_sample.dtype),
            s_ret_p.astype(st_dtype), h_wkv_p.astype(st_dtype),
            shift_p.reshape(Bp, SHIFT_W).astype(st_dtype),
            s_ret_s.astype(st_dtype), h_wkv_s.astype(st_dtype),
            shift_s.reshape(Bs, SHIFT_W).astype(st_dtype))
```

```python
import functools

import numpy as np
import jax
import jax.numpy as jnp
from jax import lax
from jax.experimental import pallas as pl
from jax.experimental.pallas import tpu as pltpu

D_MODEL = 1024
N_META = 16
PAST_LEN = 16384
RET_HEADS = 4
RET_DK = 64
RET_DV = 128
RET_CHUNK = 128
RWKV_HEADS = 8
RWKV_HD = 64
RWKV_W = RWKV_HEADS * RWKV_HD
DECAY_LORA = 64
AAA_LORA = 64
GATE_LORA = 128
D_FF = 2816
ROPE_BASE = 10000.0
NORM_EPS = 1e-6
RET_GN_EPS = 1e-6
RWKV_GN_EPS = 64e-5
RET_QK = RET_HEADS * RET_DK
RET_V = RET_HEADS * RET_DV
SHIFT_W = 3 * RWKV_W + DECAY_LORA + AAA_LORA + GATE_LORA
GATE_W = 2 * D_MODEL
PROJ_W = 2 * RET_QK + 2 * RET_V + SHIFT_W + GATE_W

_C_Q, _C_K, _C_V, _C_G = 0, RET_QK, 2 * RET_QK, 2 * RET_QK + RET_V
_C_P = 2 * RET_QK + 2 * RET_V
_C_GATE = _C_P + SHIFT_W

V7X_VMEM_LIMIT_BYTES = 56 * 1024 * 1024
FF_CHUNK = 256
RWKV_CHUNK = 64
RWKV_LANE_HEADS = 2
HEAD_SUM_LANES = 256
RWKV_PROMPT_SEQS = 8
RET_PROMPT_SEQS = 8
RET_SAMPLE_SEQS = 32
LOG_GAMMA = tuple(float(np.log1p(-2.0 ** (-5.0 - h))) for h in range(RET_HEADS))

f32 = jnp.float32
bf16 = jnp.bfloat16


def _resident(shape):
    zeros = (0,) * len(shape)
    return pl.BlockSpec(shape, lambda *_: zeros, pipeline_mode=pl.Buffered(1))


def _rms(x, g):
    return x * lax.rsqrt(jnp.mean(x * x, axis=-1, keepdims=True) + NORM_EPS) * g


def _dot(a, b):
    return jnp.dot(a, b, preferred_element_type=f32)


def _dot_nt(a, b):
    return lax.dot_general(a, b, (((1,), (1,)), ((), ())), preferred_element_type=f32)


def _dot_tn(a, b):
    return lax.dot_general(a, b, (((0,), (0,)), ((), ())), preferred_element_type=f32)


FF_CHUNKS = D_FF // FF_CHUNK


def _swiglu_half_step(x, g_ref, gate, up, down):
    xn = _rms(x, g_ref[...]).astype(bf16)
    acc = jnp.zeros(x.shape, f32)
    for c in range(FF_CHUNKS):
        gt = _dot(xn, gate(c))
        ut = _dot(xn, up(c))
        act = (gt * jax.nn.sigmoid(gt) * ut).astype(bf16)
        acc = acc + _dot(act, down(c))
    return x + 0.5 * acc


def _column_chunks(w_ref):
    return lambda c: w_ref[:, c * FF_CHUNK:(c + 1) * FF_CHUNK]


def _row_chunks(w_ref):
    return lambda c: w_ref[c * FF_CHUNK:(c + 1) * FF_CHUNK, :]


def _ffn_kernel(x_ref, g_ref, wg_ref, wu_ref, wd_ref, o_ref):
    o_ref[...] = _swiglu_half_step(x_ref[...], g_ref, _column_chunks(wg_ref), _column_chunks(wu_ref),
                                   _row_chunks(wd_ref))


_FFN_WEIGHT_SPECS = ((1, D_MODEL), (D_MODEL, D_FF), (D_MODEL, D_FF), (D_FF, D_MODEL))


def _ffn_chunk_specs(step_of):
    col = pl.BlockSpec((D_MODEL, FF_CHUNK), lambda i: (0, step_of(i)))
    return [col, col, pl.BlockSpec((FF_CHUNK, D_MODEL), lambda i: (step_of(i), 0))]


def _ffn_bf16_shapes():
    return [jax.ShapeDtypeStruct(s, bf16) for s in _FFN_WEIGHT_SPECS[1:]]


def _ffn_stream_kernel(x_ref, g_ref, wg_f, wu_f, wd_f, o_ref, wg_o, wu_o, wd_o, xn_s, acc_s):
    c = pl.program_id(0)

    @pl.when(c == 0)
    def _():
        xn_s[...] = _rms(x_ref[...], g_ref[...]).astype(bf16)
        acc_s[...] = jnp.zeros(acc_s.shape, f32)

    wg = wg_f[...].astype(bf16)
    wu = wu_f[...].astype(bf16)
    wd = wd_f[...].astype(bf16)
    wg_o[...] = wg
    wu_o[...] = wu
    wd_o[...] = wd
    xn = xn_s[...]
    gt = _dot(xn, wg)
    ut = _dot(xn, wu)
    act = (gt * jax.nn.sigmoid(gt) * ut).astype(bf16)
    acc_s[...] = acc_s[...] + _dot(act, wd)

    @pl.when(c == pl.num_programs(0) - 1)
    def _():
        o_ref[...] = x_ref[...] + 0.5 * acc_s[...]


def _ffn_stream(x, norm_g, wg, wu, wd):
    rows = x.shape[0]
    weight_chunks = _ffn_chunk_specs(lambda c: c)
    whole = pl.BlockSpec((rows, D_MODEL), lambda c: (0, 0))
    return pl.pallas_call(
        _ffn_stream_kernel,
        grid=(FF_CHUNKS,),
        in_specs=[_resident((rows, D_MODEL)), _resident((1, D_MODEL))] + weight_chunks,
        out_specs=[whole] + weight_chunks,
        out_shape=[jax.ShapeDtypeStruct((rows, D_MODEL), f32)] + _ffn_bf16_shapes(),
        scratch_shapes=[pltpu.VMEM((rows, D_MODEL), bf16), pltpu.VMEM((rows, D_MODEL), f32)],
        compiler_params=pltpu.CompilerParams(dimension_semantics=("arbitrary",),
                                             vmem_limit_bytes=V7X_VMEM_LIMIT_BYTES),
        name="ffn_stream",
    )(x, norm_g, wg, wu, wd)


def _ffn_side_kernel(x_ref, g_ref, wg_ref, wu_ref, wd_ref, win_f, o_ref, win_o, *, in_chunks):
    _ffn_kernel(x_ref, g_ref, wg_ref, wu_ref, wd_ref, o_ref)

    @pl.when(pl.program_id(0) < in_chunks)
    def _():
        win_o[...] = win_f[...].astype(bf16)


def _ffn(x, norm_g, wg, wu, wd, cast_w_in, *, tm):
    rows = x.shape[0]
    tiles = rows // tm
    row = pl.BlockSpec((tm, D_MODEL), lambda i: (i, 0))
    in_specs = [row] + [_resident(s) for s in _FFN_WEIGHT_SPECS]
    in_chunks = PROJ_W // FF_CHUNK
    assert in_chunks <= tiles
    win = pl.BlockSpec((D_MODEL, FF_CHUNK), lambda i: (0, jnp.minimum(i, in_chunks - 1)))
    return pl.pallas_call(
        functools.partial(_ffn_side_kernel, in_chunks=in_chunks),
        grid=(tiles,),
        in_specs=in_specs + [win],
        out_specs=[row, win],
        out_shape=[jax.ShapeDtypeStruct((rows, D_MODEL), f32), jax.ShapeDtypeStruct((D_MODEL, PROJ_W), bf16)],
        compiler_params=pltpu.CompilerParams(dimension_semantics=("arbitrary",),
                                             vmem_limit_bytes=V7X_VMEM_LIMIT_BYTES),
        name="ffn",
    )(x, norm_g, wg, wu, wd, cast_w_in)


def _swap_halves(x):
    parts = []
    for j in range(x.shape[1] // 128):
        xs = x[:, 128 * j:128 * (j + 1)]
        fwd = pltpu.roll(xs, 32, 1)
        bwd = pltpu.roll(xs, 96, 1)
        lane = lax.broadcasted_iota(jnp.int32, xs.shape, 1)
        parts.append(jnp.where((lane % RET_DK) < RET_DK // 2, bwd, fwd))
    return jnp.concatenate(parts, axis=1)


def _inproj_cast_kernel(h_ref, g_ref, w_ref, cos_ref, sin_ref, wg_f, wu_f, wd_f,
                        q_ref, k_ref, v_ref, sg_ref, p_ref, gate_ref, wg_o, wu_o, wd_o):
    _inproj_kernel(h_ref, g_ref, w_ref, cos_ref, sin_ref, q_ref, k_ref, v_ref, sg_ref, p_ref, gate_ref)

    @pl.when(pl.program_id(0) < FF_CHUNKS)
    def _():
        for w_f, w_o in ((wg_f, wg_o), (wu_f, wu_o), (wd_f, wd_o)):
            w_o[...] = w_f[...].astype(bf16)


def _inproj_kernel(h_ref, g_ref, w_ref, cos_ref, sin_ref, q_ref, k_ref, v_ref, sg_ref, p_ref, gate_ref):
    un = _rms(h_ref[...], g_ref[...]).astype(bf16)
    gate_ref[...] = jax.nn.sigmoid(_dot(un, w_ref[:, _C_GATE:PROJ_W]))
    gr = _dot(un, w_ref[:, _C_G:_C_P])
    sg_ref[...] = gr * jax.nn.sigmoid(gr)
    cos = cos_ref[...]
    sin = sin_ref[...]
    q = _dot(un, w_ref[:, _C_Q:_C_K])
    q_ref[...] = q * cos + _swap_halves(q) * sin
    k = _dot(un, w_ref[:, _C_K:_C_V])
    k_ref[...] = (k * cos + _swap_halves(k) * sin) * (RET_DK ** -0.5)
    v_ref[...] = _dot(un, w_ref[:, _C_V:_C_G]).astype(v_ref.dtype)
    p_ref[...] = _dot(un, w_ref[:, _C_P:_C_GATE])


def _inproj(h, norm_g, w_in, cos, sin, *, tm, cast_ffn=None):
    rows = h.shape[0]
    tiles = rows // tm
    tab_blocks = cos.shape[0] // tm

    def rowspec(width):
        return pl.BlockSpec((tm, width), lambda i: (i, 0))

    tab = pl.BlockSpec((tm, RET_QK), lambda i: (i % tab_blocks, 0))
    widths = (RET_QK, RET_QK, RET_V, RET_V, SHIFT_W, GATE_W)
    dtypes = (f32, f32, bf16, f32, f32, f32)
    in_specs = [rowspec(D_MODEL), _resident((1, D_MODEL)), _resident((D_MODEL, PROJ_W)), tab, tab]
    out_specs = [rowspec(w) for w in widths]
    out_shape = [jax.ShapeDtypeStruct((rows, w), d) for w, d in zip(widths, dtypes)]
    operands = (h, norm_g, w_in, cos, sin)
    if cast_ffn is not None:
        assert FF_CHUNKS <= tiles
        weight_chunks = _ffn_chunk_specs(lambda i: jnp.minimum(i, FF_CHUNKS - 1))
        in_specs += weight_chunks
        out_specs += weight_chunks
        out_shape += _ffn_bf16_shapes()
        operands += tuple(cast_ffn)
    return pl.pallas_call(
        _inproj_kernel if cast_ffn is None else _inproj_cast_kernel,
        grid=(tiles,),
        in_specs=in_specs,
        out_specs=out_specs,
        out_shape=out_shape,
        compiler_params=pltpu.CompilerParams(dimension_semantics=("arbitrary",),
                                             vmem_limit_bytes=V7X_VMEM_LIMIT_BYTES),
        name="inproj",
    )(*operands)


def _ret_kernel(q_ref, k_ref, v_ref, sg_ref, s0_ref, gn_ref, o_ref, s_out_ref, s_scr, o_scr, *, nb, rows, valid):
    c = pl.program_id(1)
    L = rows

    @pl.when(c == 0)
    def _():
        s_scr[...] = jnp.broadcast_to(s0_ref[...], s_scr.shape)

    ii = lax.broadcasted_iota(jnp.int32, (L, L), 0)
    jj = lax.broadcasted_iota(jnp.int32, (L, L), 1)
    diff = (ii - jj).astype(f32)
    row = lax.broadcasted_iota(jnp.int32, (L, 1), 0).astype(f32)
    mask = [jnp.where(diff >= 0, jnp.exp(lg * jnp.maximum(diff, 0.0)), 0.0) for lg in LOG_GAMMA]
    q_decay = [jnp.exp(lg * (row + 1.0)) for lg in LOG_GAMMA]
    k_decay = [jnp.exp(lg * (valid - 1.0 - row)) for lg in LOG_GAMMA]
    s_decay = [float(np.exp(lg * valid)) for lg in LOG_GAMMA]

    chains = [(j, h) for j in range(nb) for h in range(RET_HEADS)]
    qh = [q_ref[j, :, RET_DK * h:RET_DK * (h + 1)] for j, h in chains]
    kh = [k_ref[j, :, RET_DK * h:RET_DK * (h + 1)] for j, h in chains]
    vh = [v_ref[j, :, RET_DV * h:RET_DV * (h + 1)] for j, h in chains]
    scores = [(_dot_nt(qh[i].astype(bf16), kh[i].astype(bf16)) * mask[h]).astype(bf16)
              for i, (j, h) in enumerate(chains)]
    qd = [(qh[i] * q_decay[h]).astype(bf16) for i, (j, h) in enumerate(chains)]
    kd = [(kh[i] * k_decay[h]).astype(bf16) for i, (j, h) in enumerate(chains)]
    s_old = [s_scr[j, h] for j, h in chains]
    for i, (j, h) in enumerate(chains):
        o_scr[j * L:(j + 1) * L, RET_DV * h:RET_DV * (h + 1)] = (
            _dot(scores[i], vh[i]) + _dot(qd[i], s_old[i].astype(bf16)))
    for i, (j, h) in enumerate(chains):
        s_scr[j, h] = s_decay[h] * s_old[i] + _dot_tn(kd[i], vh[i])

    gn = gn_ref[...]
    sg = sg_ref[...].reshape(nb * L, RET_V)
    for h in range(RET_HEADS):
        sl = slice(RET_DV * h, RET_DV * (h + 1))
        o = o_scr[:, sl]
        mu = jnp.mean(o, axis=-1, keepdims=True)
        oc = o - mu
        var = jnp.mean(oc * oc, axis=-1, keepdims=True)
        out = oc * lax.rsqrt(var + RET_GN_EPS) * gn[:, sl] * sg[:, sl]
        o_ref[:, :, sl] = out.reshape(nb, L, RET_DV).astype(o_ref.dtype)

    @pl.when(c == pl.num_programs(1) - 1)
    def _():
        s_out_ref[...] = s_scr[...]


def _retention(q, k, v, sg, s0, gn, *, nb, rows, valid):
    B, T, _ = q.shape
    bcast = s0.shape[0] == 1

    def seq(width):
        return pl.BlockSpec((nb, rows, width), lambda b, c: (b, c, 0))

    state = pl.BlockSpec((nb, RET_HEADS, RET_DK, RET_DV), lambda b, c: (b, 0, 0, 0))
    state_in = pl.BlockSpec((1, RET_HEADS, RET_DK, RET_DV), lambda b, c: (0, 0, 0, 0)) if bcast else state
    return pl.pallas_call(
        functools.partial(_ret_kernel, nb=nb, rows=rows, valid=valid),
        grid=(B // nb, T // rows),
        in_specs=[seq(RET_QK), seq(RET_QK), seq(RET_V), seq(RET_V), state_in, _resident((1, RET_V))],
        out_specs=[seq(RET_V), state],
        out_shape=[jax.ShapeDtypeStruct((B, T, RET_V), bf16),
                   jax.ShapeDtypeStruct((B, RET_HEADS, RET_DK, RET_DV), f32)],
        scratch_shapes=[pltpu.VMEM((nb, RET_HEADS, RET_DK, RET_DV), f32), pltpu.VMEM((nb * rows, RET_V), f32)],
        compiler_params=pltpu.CompilerParams(dimension_semantics=("parallel", "arbitrary"),
                                             vmem_limit_bytes=V7X_VMEM_LIMIT_BYTES),
        name="retention",
    )(q, k, v, sg, s0, gn)


def _head_sums(x, ones_bd):
    rows = x.shape[0]
    width = ones_bd.shape[0]
    groups = RWKV_W // width
    stacked = jnp.concatenate([x[:, width * i:width * (i + 1)] for i in range(groups)], axis=0)
    z = _dot(stacked.astype(bf16), ones_bd)
    return jnp.concatenate([z[rows * i:rows * (i + 1)] for i in range(groups)], axis=1)


def _rwkv_features(p, p_prev, mu_ref, w0_ref, w2_ref, a0_ref, a2_ref, g2_ref, kk_ref, ka_ref, ones_bd):
    pm = p + (p_prev - p) * mu_ref[...]
    r = pm[:, 0:RWKV_W]
    k = pm[:, RWKV_W:2 * RWKV_W]
    v = pm[:, 2 * RWKV_W:3 * RWKV_W]
    o_w = 3 * RWKV_W
    xw = pm[:, o_w:o_w + DECAY_LORA]
    xa = pm[:, o_w + DECAY_LORA:o_w + DECAY_LORA + AAA_LORA]
    xg = pm[:, o_w + DECAY_LORA + AAA_LORA:SHIFT_W]
    z = w0_ref[...] + _dot(jnp.tanh(xw).astype(bf16), w2_ref[...])
    ld = -float(np.exp(-0.5)) * jax.nn.sigmoid(z)
    a = jax.nn.sigmoid(a0_ref[...] + _dot(xa.astype(bf16), a2_ref[...]))
    g = _dot(jax.nn.sigmoid(xg).astype(bf16), g2_ref[...])
    kk = k * kk_ref[...]
    kk = kk * lax.rsqrt(jnp.maximum(_head_sums(kk * kk, ones_bd), 1e-24))
    kp = k * (1.0 + (a - 1.0) * ka_ref[...])
    return r, kp, v, kk, a, ld, g


def _rwkv_output(y, bonus_rkv, g, lng_ref, lnb_ref, ones_bd):
    inv_n = 1.0 / RWKV_HD
    mean = _head_sums(y, ones_bd) * inv_n
    yc = y - mean
    var = _head_sums(yc * yc, ones_bd) * inv_n
    out = yc * lax.rsqrt(var + RWKV_GN_EPS) * lng_ref[...] + lnb_ref[...]
    rkr, v = bonus_rkv
    bonus = _head_sums(rkr, ones_bd) * v
    return (out + bonus) * g


def _rwkv_kernel(p_ref, prev0_ref, h0_ref, mu_ref, w0_ref, w2_ref, a0_ref, a2_ref, g2_ref, kk_ref, ka_ref,
                 rk_ref, lng_ref, lnb_ref, ones_ref, o_ref, h_out_ref, shift_ref, s_scr, prev_scr, y_scr,
                 *, nb, rows, valid):
    c = pl.program_id(1)
    C = rows
    N = RWKV_HD
    R = nb * C
    assert C & (C - 1) == 0
    log2c = C.bit_length() - 1

    @pl.when(c == 0)
    def _():
        for j in range(nb):
            for h in range(RWKV_HEADS):
                s_scr[j, :, N * h:N * (h + 1)] = h0_ref[j if h0_ref.shape[0] > 1 else 0, h]
        prev_scr[...] = jnp.broadcast_to(prev0_ref[...], prev_scr.shape)

    p = p_ref[...].reshape(R, SHIFT_W)
    rowid = lax.broadcasted_iota(jnp.int32, (R, 1), 0)
    step = rowid & (C - 1)
    p_prev = pltpu.roll(p, 1, 0)
    for j in range(nb):
        p_prev = jnp.where(rowid == j * C, prev_scr[j], p_prev)
    for j in range(nb):
        prev_scr[j] = p[j * C + valid - 1:j * C + valid, :]
    ones_bd = ones_ref[...]
    r, kp, v, kk, a, ld, g = _rwkv_features(p, p_prev, mu_ref, w0_ref, w2_ref, a0_ref, a2_ref, g2_ref, kk_ref,
                                            ka_ref, ones_bd)
    if valid < C:
        live = (step < valid).astype(f32)
        ld = ld * live
        kk = kk * live
        kp = kp * live
        v = v * live
    b = kk * a

    cum = ld
    for level in range(log2c):
        reach = 1 << level
        cum = cum + jnp.where(step >= reach, pltpu.roll(cum, reach, 0), 0.0)
    last_rows = [cum[j * C + C - 1:j * C + C, :] for j in range(nb)]
    cum_last = jnp.concatenate([jnp.broadcast_to(x, (C, RWKV_W)) for x in last_rows], axis=0)
    e_in = jnp.exp(cum)
    e_ex = jnp.exp(cum - ld)
    e_neg = jnp.exp(-cum)
    e_end = jnp.exp(cum_last - cum)
    g_end = [jnp.exp(x) for x in last_rows]
    at = -kk * e_ex
    rt = r * e_in
    bt = (b * e_neg).astype(bf16)
    kt = (kp * e_neg).astype(bf16)
    bh = (b * e_end).astype(bf16)
    kh = (kp * e_end).astype(bf16)
    vb = v.astype(bf16)

    G = RWKV_LANE_HEADS
    GW = G * N
    log2n = N.bit_length() - 1
    def head_masks(width, log2_block):
        lane = lax.broadcasted_iota(jnp.int32, (1, width), 1)
        return [((lane >> log2_block) & (G - 1)) == h for h in range(G)]

    head_of_lane = head_masks(GW, log2n)
    head_of_lane2 = head_masks(2 * GW, log2n)
    head_of_col = head_masks(G * C, log2c)
    head_of_col2 = head_masks(2 * G * C, log2c)
    ti = lax.broadcasted_iota(jnp.int32, (C, G * C), 0)
    tj = lax.broadcasted_iota(jnp.int32, (C, G * C), 1) & (C - 1)
    incl = ti >= tj
    strict = ti > tj
    unit = ti == tj
    gi = lax.broadcasted_iota(jnp.int32, (GW, GW), 0)
    gj = lax.broadcasted_iota(jnp.int32, (GW, GW), 1)
    eye_g = gi == gj
    same_head = (gi >> log2n) == (gj >> log2n)

    def bd(a, masks):
        return jnp.concatenate([jnp.where(m, a, jnp.zeros_like(a)) for m in masks], axis=0)

    groups = [(j, q) for j in range(nb) for q in range(RWKV_HEADS // G)]
    n_gr = len(groups)

    def grp(x, j, q):
        return x[j * C:(j + 1) * C, GW * q:GW * (q + 1)]

    ar = [jnp.concatenate([grp(at, j, q), grp(rt, j, q)], axis=0).astype(bf16) for j, q in groups]
    gbk = [_dot_nt(ar[i], jnp.concatenate([bd(grp(bt, j, q), head_of_lane), bd(grp(kt, j, q), head_of_lane)],
                                          axis=0)) for i, (j, q) in enumerate(groups)]
    gb = [x[:, :G * C] for x in gbk]
    gk = [x[:, G * C:] for x in gbk]
    lab = [jnp.where(strict, x[:C], 0.0) for x in gb]
    mrb = [jnp.where(incl, x[C:], 0.0).astype(bf16) for x in gb]
    lmk = [jnp.concatenate([jnp.where(strict, x[:C], 0.0), jnp.where(incl, x[C:], 0.0)], axis=0).astype(bf16)
           for x in gk]
    lmv = [_dot(lmk[i], bd(grp(vb, j, q), head_of_lane)) for i, (j, q) in enumerate(groups)]
    m = [_dot(x.astype(bf16), bd(x.astype(bf16), head_of_col)) for x in lab]
    t_inv = [jnp.where(unit, 1.0, x) for x in lab]
    for level in range(1, log2c):
        mb = [x.astype(bf16) for x in m]
        if level < log2c - 1:
            out = [_dot(mb[i], bd(jnp.concatenate([t_inv[i].astype(bf16), mb[i]], axis=1), head_of_col2))
                   for i in range(n_gr)]
            t_inv = [t_inv[i] + out[i][:, :G * C] for i in range(n_gr)]
            m = [x[:, G * C:] for x in out]
        else:
            t_inv = [t_inv[i] + _dot(mb[i], bd(t_inv[i].astype(bf16), head_of_col)) for i in range(n_gr)]
    wu = [_dot(t_inv[i].astype(bf16),
               bd(jnp.concatenate([grp(at, j, q).astype(bf16), lmv[i][:C].astype(bf16)], axis=1), head_of_lane2)
               ).astype(bf16) for i, (j, q) in enumerate(groups)]
    wa = [x[:, :GW] for x in wu]
    uv = [x[:, GW:] for x in wu]
    pq = [_dot(mrb[i], bd(wu[i], head_of_lane2)) for i in range(n_gr)]
    ry = [grp(rt, j, q) + pq[i][:, :GW] for i, (j, q) in enumerate(groups)]
    y0 = [lmv[i][C:] + pq[i][:, GW:] for i in range(n_gr)]
    gwt = [_dot_tn(wa[i], grp(bh, j, q)) for i, (j, q) in enumerate(groups)]
    dkf = [_dot_tn(jnp.concatenate([uv[i], grp(vb, j, q)], axis=0),
                   jnp.concatenate([grp(bh, j, q), grp(kh, j, q)], axis=0))
           for i, (j, q) in enumerate(groups)]
    for i, (j, q) in enumerate(groups):
        sl = slice(GW * q, GW * (q + 1))
        wg = jnp.where(same_head, gwt[i], 0.0) + jnp.where(eye_g, g_end[j][:, sl], 0.0)
        dk = jnp.where(head_of_lane[0], dkf[i][0:N], 0.0)
        for h in range(1, G):
            dk = dk + jnp.where(head_of_lane[h], dkf[i][N * h:N * (h + 1)], 0.0)
        s_old = s_scr[j, :, sl].astype(bf16)
        y_scr[j * C:(j + 1) * C, sl] = _dot_nt(ry[i].astype(bf16), bd(s_old, head_of_lane)) + y0[i]
        s_scr[j, :, sl] = _dot(s_old, wg.astype(bf16)) + dk

    out = _rwkv_output(y_scr[...], (r * kp * rk_ref[...], v), g, lng_ref, lnb_ref, ones_bd)
    o_ref[...] = out.reshape(nb, C, RWKV_W).astype(o_ref.dtype)

    @pl.when(c == pl.num_programs(1) - 1)
    def _():
        for j in range(nb):
            for h in range(RWKV_HEADS):
                h_out_ref[j, h] = s_scr[j, :, N * h:N * (h + 1)]
        shift_ref[...] = prev_scr[...]


def _rwkv(p, prev0, h0, params, ones_bd, *, nb, rows, valid):
    B, T, _ = p.shape

    def maybe_bcast(arr, tail):
        nd = len(tail) + 1
        if arr.shape[0] == 1:
            return pl.BlockSpec((1,) + tail, lambda b, c: (0,) * nd)
        return pl.BlockSpec((nb,) + tail, lambda b, c: (b,) + (0,) * (nd - 1))

    seq_in = pl.BlockSpec((nb, rows, SHIFT_W), lambda b, c: (b, c, 0))
    seq_out = pl.BlockSpec((nb, rows, RWKV_W), lambda b, c: (b, c, 0))
    st_tail = (RWKV_HEADS, RWKV_HD, RWKV_HD)
    sh_tail = (1, SHIFT_W)
    param_specs = [_resident(x.shape) for x in params]
    return pl.pallas_call(
        functools.partial(_rwkv_kernel, nb=nb, rows=rows, valid=valid),
        grid=(B // nb, T // rows),
        in_specs=[seq_in, maybe_bcast(prev0, sh_tail), maybe_bcast(h0, st_tail)] + param_specs
                 + [_resident(ones_bd.shape)],
        out_specs=[seq_out, pl.BlockSpec((nb,) + st_tail, lambda b, c: (b, 0, 0, 0)),
                   pl.BlockSpec((nb,) + sh_tail, lambda b, c: (b, 0, 0))],
        out_shape=[jax.ShapeDtypeStruct((B, T, RWKV_W), bf16),
                   jax.ShapeDtypeStruct((B,) + st_tail, f32),
                   jax.ShapeDtypeStruct((B,) + sh_tail, f32)],
        scratch_shapes=[pltpu.VMEM((nb, RWKV_HD, RWKV_W), f32), pltpu.VMEM((nb,) + sh_tail, f32),
                        pltpu.VMEM((nb * rows, RWKV_W), f32)],
        compiler_params=pltpu.CompilerParams(dimension_semantics=("parallel", "arbitrary"),
                                             vmem_limit_bytes=V7X_VMEM_LIMIT_BYTES),
        name="rwkv7",
    )(p, prev0, h0, *params, ones_bd)


def _rwkv_step_kernel(p_ref, shift0_ref, s_ref, mu_ref, w0_ref, w2_ref, a0_ref, a2_ref, g2_ref, kk_ref, ka_ref,
                      rk_ref, lng_ref, lnb_ref, ones_ref, o_ref, s_out_ref, shift_ref,
                      feat_scr, y_scr, g_scr, rkr_scr, v_scr, *, steps, batch):
    h = pl.program_id(0)
    T, B, N = steps, batch, RWKV_HD
    key_tiles = N // 8
    f_r, f_w, f_k, f_v, f_a, f_b = range(6)

    @pl.when(h == 0)
    def _():
        p = p_ref[...]
        p_prev = jnp.concatenate([shift0_ref[...], p[:(T - 1) * B]], axis=0)
        r, kp, v, kk, a, ld, g = _rwkv_features(p, p_prev, mu_ref, w0_ref, w2_ref, a0_ref, a2_ref, g2_ref,
                                                kk_ref, ka_ref, ones_ref[...])
        for idx, x in enumerate((r, jnp.exp(ld), kp, v, -kk, kk * a)):
            for t in range(T):
                feat_scr[idx, t] = x[t * B:(t + 1) * B, :].T
        g_scr[...] = g
        rkr_scr[...] = r * kp * rk_ref[...]
        v_scr[...] = v
        shift_ref[...] = p[(T - 1) * B:]

    base = pl.multiple_of(h * N, N)

    def tile(idx, t, kt):
        return feat_scr[idx, t, pl.ds(pl.multiple_of(base + 8 * kt, 8), 8), :]

    def keysum(s, idx, t):
        acc = s[0] * tile(idx, t, 0)
        for kt in range(1, key_tiles):
            acc = acc + s[kt] * tile(idx, t, kt)
        return jnp.sum(acc, axis=0, keepdims=True)

    def value_group(vg, carry):
        v_rows = pl.ds(pl.multiple_of(base + 8 * vg, 8), 8)
        v_tiles = [feat_scr[f_v, t, v_rows, :] for t in range(T)]
        y_rows = [[] for _ in range(T)]
        for i in range(8):
            row0 = pl.multiple_of((8 * vg + i) * N, N)
            s = [s_ref[0, pl.ds(row0 + 8 * kt, 8), :] for kt in range(key_tiles)]
            for t in range(T):
                sa = jnp.broadcast_to(keysum(s, f_a, t), (8, B))
                vv = jnp.broadcast_to(v_tiles[t][i:i + 1, :], (8, B))
                s = [s[kt] * tile(f_w, t, kt) + sa * tile(f_b, t, kt) + vv * tile(f_k, t, kt)
                     for kt in range(key_tiles)]
                y_rows[t].append(keysum(s, f_r, t))
            for kt in range(key_tiles):
                s_out_ref[0, pl.ds(row0 + 8 * kt, 8), :] = s[kt]
        for t in range(T):
            y_scr[t, v_rows, :] = jnp.concatenate(y_rows[t], axis=0)
        return carry

    lax.fori_loop(0, N // 8, value_group, 0)

    @pl.when(h == pl.num_programs(0) - 1)
    def _():
        y = jnp.concatenate([y_scr[t].T for t in range(T)], axis=0)
        out = _rwkv_output(y, (rkr_scr[...], v_scr[...]), g_scr[...], lng_ref, lnb_ref, ones_ref[...])
        o_ref[...] = out.astype(o_ref.dtype)


def _rwkv_step(p, shift0, state, params, ones_bd, *, steps):
    B = state.shape[0]
    rows = steps * B
    n_state = RWKV_HD * RWKV_HD
    s_in = jnp.transpose(state, (1, 2, 3, 0)).reshape(RWKV_HEADS, n_state, B)
    s_spec = pl.BlockSpec((1, n_state, B), lambda h: (h, 0, 0))
    o, s_out, shift = pl.pallas_call(
        functools.partial(_rwkv_step_kernel, steps=steps, batch=B),
        grid=(RWKV_HEADS,),
        in_specs=[_resident((rows, SHIFT_W)), _resident((B, SHIFT_W)), s_spec]
                 + [_resident(x.shape) for x in params] + [_resident(ones_bd.shape)],
        out_specs=[pl.BlockSpec((rows, RWKV_W), lambda h: (0, 0)), s_spec,
                   pl.BlockSpec((B, SHIFT_W), lambda h: (0, 0))],
        out_shape=[jax.ShapeDtypeStruct((rows, RWKV_W), bf16),
                   jax.ShapeDtypeStruct((RWKV_HEADS, n_state, B), f32),
                   jax.ShapeDtypeStruct((B, SHIFT_W), f32)],
        scratch_shapes=[pltpu.VMEM((6, steps, RWKV_W, B), f32), pltpu.VMEM((steps, RWKV_W, B), f32),
                        pltpu.VMEM((rows, RWKV_W), f32), pltpu.VMEM((rows, RWKV_W), f32),
                        pltpu.VMEM((rows, RWKV_W), f32)],
        compiler_params=pltpu.CompilerParams(dimension_semantics=("arbitrary",),
                                             vmem_limit_bytes=V7X_VMEM_LIMIT_BYTES),
        name="rwkv7_step",
    )(p, shift0, s_in, *params, ones_bd)
    s_out = jnp.transpose(s_out.reshape(RWKV_HEADS, RWKV_HD, RWKV_HD, B), (3, 0, 1, 2))
    return o, s_out, shift


def _merge_ffn_kernel(h_ref, oret_ref, orwkv_ref, gate_ref, wr_ref, ww_ref, wo_ref,
                      g_ref, wg_ref, wu_ref, wd_ref, fin_ref, o_ref):
    a = _dot(oret_ref[...], wr_ref[...])
    b = _dot(orwkv_ref[...], ww_ref[...])
    merged = gate_ref[:, :D_MODEL] * a + gate_ref[:, D_MODEL:] * b
    h = h_ref[...] + _dot(merged.astype(bf16), wo_ref[...])
    h = _swiglu_half_step(h, g_ref, _column_chunks(wg_ref), _column_chunks(wu_ref), _row_chunks(wd_ref))
    o_ref[...] = _rms(h, fin_ref[...])


def _merge_ffn(h, o_ret, o_rwkv, gates, w_out_ret, w_out_rwkv, w_out, norm_g, wg, wu, wd, fin_g, *, tm):
    rows = o_ret.shape[0]
    assert rows % tm == 0 and o_rwkv.shape[0] == rows and h.shape[0] >= rows and gates.shape[0] >= rows

    def rowspec(width):
        return pl.BlockSpec((tm, width), lambda i: (i, 0))

    return pl.pallas_call(
        _merge_ffn_kernel,
        grid=(rows // tm,),
        in_specs=[rowspec(D_MODEL), rowspec(RET_V), rowspec(RWKV_W), rowspec(GATE_W),
                  _resident((RET_V, D_MODEL)), _resident((RWKV_W, D_MODEL)), _resident((D_MODEL, D_MODEL))]
                 + [_resident(s) for s in _FFN_WEIGHT_SPECS] + [_resident((1, D_MODEL))],
        out_specs=rowspec(D_MODEL),
        out_shape=jax.ShapeDtypeStruct((rows, D_MODEL), f32),
        compiler_params=pltpu.CompilerParams(dimension_semantics=("parallel",),
                                             vmem_limit_bytes=V7X_VMEM_LIMIT_BYTES),
        name="merge_ffn",
    )(h, o_ret, o_rwkv, gates, w_out_ret, w_out_rwkv, w_out, norm_g, wg, wu, wd, fin_g)


def _rotary_tables(pos):
    half = RET_DK // 2
    inv_freq = ROPE_BASE ** (-jnp.arange(half, dtype=f32) / half)
    ang = pos.astype(f32)[:, None] * inv_freq[None, :]
    cos = jnp.cos(ang)
    sin = jnp.sin(ang)
    cos_t = jnp.tile(jnp.concatenate([cos, cos], axis=1), (1, RET_HEADS))
    sin_t = jnp.tile(jnp.concatenate([-sin, sin], axis=1), (1, RET_HEADS))
    return cos_t, sin_t


def _row_tile(rows, target):
    tm = min(rows, target)
    while rows % tm:
        tm -= 8
    return tm


def kernel(x_prompt, x_sample, state_ret, state_wkv, state_shift, meta_tokens, ffn1_norm, ffn1_w_gate, ffn1_w_up, ffn1_w_down, mix_norm, w_in, ret_gn_g, mu_shift, w0, w2, a0, a2, g2, k_k, k_a, r_k, lnx_g, lnx_b, w_out_ret, w_out_rwkv, w_out, ffn2_norm, ffn2_w_gate, ffn2_w_up, ffn2_w_down, final_norm):
    Bp, Tp, _ = x_prompt.shape
    Bs, Ts, _ = x_sample.shape
    st_dtype = state_ret.dtype

    def row(x):
        return x.reshape(1, -1).astype(f32)

    fin = row(final_norm)
    mix_g = row(mix_norm)
    wr_b, ww_b, wo_b = w_out_ret.astype(bf16), w_out_rwkv.astype(bf16), w_out.astype(bf16)
    rwkv_params = (row(mu_shift), row(w0), w2.astype(bf16), row(a0), a2.astype(bf16), g2.astype(bf16),
                   row(k_k), row(k_a), row(r_k), row(lnx_g), row(lnx_b))
    head_id = jnp.arange(HEAD_SUM_LANES, dtype=jnp.int32) // RWKV_HD
    ones_bd = (head_id[:, None] == head_id[None, :]).astype(bf16)
    gn = row(ret_gn_g)

    n_s = Bs * Ts
    x_small = jnp.concatenate([jnp.swapaxes(x_sample, 0, 1).reshape(n_s, D_MODEL),
                               meta_tokens.astype(x_sample.dtype)], axis=0)
    cos_s, sin_s = _rotary_tables(PAST_LEN + jnp.arange(Ts, dtype=jnp.int32))
    cos_m, sin_m = _rotary_tables(jnp.arange(N_META, dtype=jnp.int32))
    cos_small = jnp.concatenate([jnp.repeat(cos_s, Bs, axis=0), cos_m], axis=0)
    sin_small = jnp.concatenate([jnp.repeat(sin_s, Bs, axis=0), sin_m], axis=0)
    n_small = n_s + N_META
    h_small, *ffn1_b = _ffn_stream(x_small, row(ffn1_norm), ffn1_w_gate, ffn1_w_up, ffn1_w_down)

    n_p = Bp * Tp
    h_p, w_in_b = _ffn(x_prompt.reshape(n_p, D_MODEL), row(ffn1_norm), *ffn1_b, w_in, tm=_row_tile(n_p, 512))
    q_s, k_s, v_s, sg_s, p_s, gate_s = _inproj(h_small, mix_g, w_in_b, cos_small, sin_small, tm=n_small)

    def meta(x):
        return x[n_s:].reshape(1, N_META, -1)

    zeros_ret = jnp.zeros((1, RET_HEADS, RET_DK, RET_DV), f32)
    _, s_ret_m = _retention(meta(q_s), meta(k_s), meta(v_s), meta(sg_s), zeros_ret, gn,
                            nb=1, rows=N_META, valid=N_META)
    _, h_wkv_m, shift_m = _rwkv(meta(p_s), jnp.zeros((1, 1, SHIFT_W), f32),
                                jnp.zeros((1, RWKV_HEADS, RWKV_HD, RWKV_HD), f32), rwkv_params, ones_bd,
                                nb=1, rows=N_META, valid=N_META)

    pad_t = -(-Ts // 8) * 8

    def samp(x):
        x = jnp.swapaxes(x[:n_s].reshape(Ts, Bs, -1), 0, 1)
        return jnp.pad(x, ((0, 0), (0, pad_t - Ts), (0, 0)))

    o_ret_s, s_ret_s = _retention(samp(q_s), samp(k_s), samp(v_s), samp(sg_s), state_ret.astype(f32), gn,
                                  nb=RET_SAMPLE_SEQS, rows=pad_t, valid=Ts)
    o_ret_s = jnp.swapaxes(o_ret_s[:, :Ts], 0, 1).reshape(n_s, RET_V)
    o_wkv_s, h_wkv_s, shift_s = _rwkv_step(p_s, state_shift.astype(f32), state_wkv.astype(f32),
                                           rwkv_params, ones_bd, steps=Ts)

    cos_p, sin_p = _rotary_tables(N_META + jnp.arange(Tp, dtype=jnp.int32))
    q_p, k_p, v_p, sg_p, p_p, gate_p, *ffn2_b = _inproj(h_p, mix_g, w_in_b, cos_p, sin_p, tm=_row_tile(Tp, 512),
                                                        cast_ffn=(ffn2_w_gate, ffn2_w_up, ffn2_w_down))

    def post(h, o_ret, o_rwkv, gates, tm):
        return _merge_ffn(h, o_ret, o_rwkv, gates, wr_b, ww_b, wo_b, row(ffn2_norm), *ffn2_b, fin, tm=tm)

    y_sample = post(h_small, o_ret_s, o_wkv_s, gate_s, _row_tile(n_s, 512))
    y_sample = jnp.swapaxes(y_sample.reshape(Ts, Bs, D_MODEL), 0, 1)

    def seqs(x):
        return x.reshape(Bp, Tp, -1)

    o_ret_p, s_ret_p = _retention(seqs(q_p), seqs(k_p), seqs(v_p), seqs(sg_p), s_ret_m, gn,
                                  nb=RET_PROMPT_SEQS, rows=RET_CHUNK, valid=RET_CHUNK)
    o_wkv_p, h_wkv_p, shift_p = _rwkv(seqs(p_p), shift_m, h_wkv_m, rwkv_params, ones_bd,
                                      nb=RWKV_PROMPT_SEQS, rows=RWKV_CHUNK, valid=RWKV_CHUNK)
    y_prompt = post(h_p, o_ret_p.reshape(n_p, RET_V), o_wkv_p.reshape(n_p, RWKV_W), gate_p, _row_tile(n_p, 512))

    return (y_prompt.reshape(Bp, Tp, D_MODEL).astype(x_prompt.dtype),
            y_sample.reshape(Bs, Ts, D_MODEL).astype(x_sample.dtype),
            s_ret_p.astype(st_dtype), h_wkv_p.astype(st_dtype),
            shift_p.reshape(Bp, SHIFT_W).astype(st_dtype),
            s_ret_s.astype(st_dtype), h_wkv_s.astype(st_dtype),
            shift_s.reshape(Bs, SHIFT_W).astype(st_dtype))
```

```python
import functools

import numpy as np
import jax
import jax.numpy as jnp
from jax import lax
from jax.experimental import pallas as pl
from jax.experimental.pallas import tpu as pltpu

D_MODEL = 1024
N_META = 16
PAST_LEN = 16384
RET_HEADS = 4
RET_DK = 64
RET_DV = 128
RET_CHUNK = 128
RWKV_HEADS = 8
RWKV_HD = 64
RWKV_W = RWKV_HEADS * RWKV_HD
DECAY_LORA = 64
AAA_LORA = 64
GATE_LORA = 128
D_FF = 2816
ROPE_BASE = 10000.0
NORM_EPS = 1e-6
RET_GN_EPS = 1e-6
RWKV_GN_EPS = 64e-5
RET_QK = RET_HEADS * RET_DK
RET_V = RET_HEADS * RET_DV
SHIFT_W = 3 * RWKV_W + DECAY_LORA + AAA_LORA + GATE_LORA
GATE_W = 2 * D_MODEL
PROJ_W = 2 * RET_QK + 2 * RET_V + SHIFT_W + GATE_W

_C_Q, _C_K, _C_V, _C_G = 0, RET_QK, 2 * RET_QK, 2 * RET_QK + RET_V
_C_P = 2 * RET_QK + 2 * RET_V
_C_GATE = _C_P + SHIFT_W

V7X_VMEM_LIMIT_BYTES = 56 * 1024 * 1024
FF_CHUNK = 256
FFN_STREAM_SLOTS = 3
RWKV_CHUNK = 64
RWKV_LANE_HEADS = 2
HEAD_SUM_LANES = 256
RWKV_PROMPT_SEQS = 8
RET_PROMPT_SEQS = 8
RET_SAMPLE_SEQS = 32
LOG_GAMMA = tuple(float(np.log1p(-2.0 ** (-5.0 - h))) for h in range(RET_HEADS))

f32 = jnp.float32
bf16 = jnp.bfloat16


def _resident(shape):
    zeros = (0,) * len(shape)
    return pl.BlockSpec(shape, lambda *_: zeros, pipeline_mode=pl.Buffered(1))


def _rms(x, g):
    return x * lax.rsqrt(jnp.mean(x * x, axis=-1, keepdims=True) + NORM_EPS) * g


def _dot(a, b):
    return jnp.dot(a, b, preferred_element_type=f32)


def _dot_nt(a, b):
    return lax.dot_general(a, b, (((1,), (1,)), ((), ())), preferred_element_type=f32)


def _dot_tn(a, b):
    return lax.dot_general(a, b, (((0,), (0,)), ((), ())), preferred_element_type=f32)


FF_CHUNKS = D_FF // FF_CHUNK


def _swiglu_half_step(x, g_ref, gate, up, down):
    xn = _rms(x, g_ref[...]).astype(bf16)
    acc = jnp.zeros(x.shape, f32)
    for c in range(FF_CHUNKS):
        gt = _dot(xn, gate(c))
        ut = _dot(xn, up(c))
        act = (gt * jax.nn.sigmoid(gt) * ut).astype(bf16)
        acc = acc + _dot(act, down(c))
    return x + 0.5 * acc


def _column_chunks(w_ref):
    return lambda c: w_ref[:, c * FF_CHUNK:(c + 1) * FF_CHUNK]


def _row_chunks(w_ref):
    return lambda c: w_ref[c * FF_CHUNK:(c + 1) * FF_CHUNK, :]


def _ffn_kernel(x_ref, g_ref, wg_ref, wu_ref, wd_ref, o_ref):
    o_ref[...] = _swiglu_half_step(x_ref[...], g_ref, _column_chunks(wg_ref), _column_chunks(wu_ref),
                                   _row_chunks(wd_ref))


_FFN_WEIGHT_SPECS = ((1, D_MODEL), (D_MODEL, D_FF), (D_MODEL, D_FF), (D_FF, D_MODEL))


def _ffn_chunk_specs(step_of):
    col = pl.BlockSpec((D_MODEL, FF_CHUNK), lambda i: (0, step_of(i)))
    return [col, col, pl.BlockSpec((FF_CHUNK, D_MODEL), lambda i: (step_of(i), 0))]


def _ffn_bf16_shapes():
    return [jax.ShapeDtypeStruct(s, bf16) for s in _FFN_WEIGHT_SPECS[1:]]


def _ffn_stream_kernel(x_ref, g_ref, wg_hbm, wu_hbm, wd_hbm, o_ref, wg_o, wu_o, wd_o, xn_s, acc_s,
                       wg_buf, wu_buf, wd_buf, sems):
    c = pl.program_id(0)
    n = pl.num_programs(0)

    def chunk_copies(chunk, slot):
        cols = pl.ds(pl.multiple_of(chunk * FF_CHUNK, FF_CHUNK), FF_CHUNK)
        return (pltpu.make_async_copy(wg_hbm.at[:, cols], wg_buf.at[slot], sems.at[0, slot]),
                pltpu.make_async_copy(wu_hbm.at[:, cols], wu_buf.at[slot], sems.at[1, slot]),
                pltpu.make_async_copy(wd_hbm.at[cols, :], wd_buf.at[slot], sems.at[2, slot]))

    @pl.when(c == 0)
    def _():
        for first in range(FFN_STREAM_SLOTS):
            for copy in chunk_copies(first, first):
                copy.start()
        xn_s[...] = _rms(x_ref[...], g_ref[...]).astype(bf16)
        acc_s[...] = jnp.zeros(acc_s.shape, f32)

    slot = c % FFN_STREAM_SLOTS
    for copy in chunk_copies(c, slot):
        copy.wait()
    wg = wg_buf[slot].astype(bf16)
    wu = wu_buf[slot].astype(bf16)
    wd = wd_buf[slot].astype(bf16)
    wg_o[...] = wg
    wu_o[...] = wu
    wd_o[...] = wd
    xn = xn_s[...]
    gt = _dot(xn, wg)
    ut = _dot(xn, wu)
    act = (gt * jax.nn.sigmoid(gt) * ut).astype(bf16)
    acc_s[...] = acc_s[...] + _dot(act, wd)

    @pl.when(c + FFN_STREAM_SLOTS < n)
    def _():
        for copy in chunk_copies(c + FFN_STREAM_SLOTS, slot):
            copy.start()

    @pl.when(c == n - 1)
    def _():
        o_ref[...] = x_ref[...] + 0.5 * acc_s[...]


def _ffn_stream(x, norm_g, wg, wu, wd):
    rows = x.shape[0]
    assert FF_CHUNKS >= FFN_STREAM_SLOTS
    weight_chunks = _ffn_chunk_specs(lambda c: c)
    whole = pl.BlockSpec((rows, D_MODEL), lambda c: (0, 0))
    hbm = pl.BlockSpec(memory_space=pl.ANY)
    return pl.pallas_call(
        _ffn_stream_kernel,
        grid=(FF_CHUNKS,),
        in_specs=[_resident((rows, D_MODEL)), _resident((1, D_MODEL)), hbm, hbm, hbm],
        out_specs=[whole] + weight_chunks,
        out_shape=[jax.ShapeDtypeStruct((rows, D_MODEL), f32)] + _ffn_bf16_shapes(),
        scratch_shapes=[pltpu.VMEM((rows, D_MODEL), bf16), pltpu.VMEM((rows, D_MODEL), f32),
                        pltpu.VMEM((FFN_STREAM_SLOTS, D_MODEL, FF_CHUNK), f32),
                        pltpu.VMEM((FFN_STREAM_SLOTS, D_MODEL, FF_CHUNK), f32),
                        pltpu.VMEM((FFN_STREAM_SLOTS, FF_CHUNK, D_MODEL), f32),
                        pltpu.SemaphoreType.DMA((3, FFN_STREAM_SLOTS))],
        compiler_params=pltpu.CompilerParams(dimension_semantics=("arbitrary",),
                                             vmem_limit_bytes=V7X_VMEM_LIMIT_BYTES),
        name="ffn_stream",
    )(x, norm_g, wg, wu, wd)


def _ffn_side_kernel(x_ref, g_ref, wg_ref, wu_ref, wd_ref, win_f, o_ref, win_o, *, in_chunks):
    _ffn_kernel(x_ref, g_ref, wg_ref, wu_ref, wd_ref, o_ref)

    @pl.when(pl.program_id(0) < in_chunks)
    def _():
        win_o[...] = win_f[...].astype(bf16)


def _ffn(x, norm_g, wg, wu, wd, cast_w_in, *, tm):
    rows = x.shape[0]
    tiles = rows // tm
    row = pl.BlockSpec((tm, D_MODEL), lambda i: (i, 0))
    in_specs = [row] + [_resident(s) for s in _FFN_WEIGHT_SPECS]
    in_chunks = PROJ_W // FF_CHUNK
    assert in_chunks <= tiles
    win = pl.BlockSpec((D_MODEL, FF_CHUNK), lambda i: (0, jnp.minimum(i, in_chunks - 1)))
    return pl.pallas_call(
        functools.partial(_ffn_side_kernel, in_chunks=in_chunks),
        grid=(tiles,),
        in_specs=in_specs + [win],
        out_specs=[row, win],
        out_shape=[jax.ShapeDtypeStruct((rows, D_MODEL), f32), jax.ShapeDtypeStruct((D_MODEL, PROJ_W), bf16)],
        compiler_params=pltpu.CompilerParams(dimension_semantics=("arbitrary",),
                                             vmem_limit_bytes=V7X_VMEM_LIMIT_BYTES),
        name="ffn",
    )(x, norm_g, wg, wu, wd, cast_w_in)


def _swap_halves(x):
    parts = []
    for j in range(x.shape[1] // 128):
        xs = x[:, 128 * j:128 * (j + 1)]
        fwd = pltpu.roll(xs, 32, 1)
        bwd = pltpu.roll(xs, 96, 1)
        lane = lax.broadcasted_iota(jnp.int32, xs.shape, 1)
        parts.append(jnp.where((lane % RET_DK) < RET_DK // 2, bwd, fwd))
    return jnp.concatenate(parts, axis=1)


def _inproj_cast_kernel(h_ref, g_ref, w_ref, cos_ref, sin_ref, wg_f, wu_f, wd_f,
                        q_ref, k_ref, v_ref, sg_ref, p_ref, gate_ref, wg_o, wu_o, wd_o):
    _inproj_kernel(h_ref, g_ref, w_ref, cos_ref, sin_ref, q_ref, k_ref, v_ref, sg_ref, p_ref, gate_ref)

    @pl.when(pl.program_id(0) < FF_CHUNKS)
    def _():
        for w_f, w_o in ((wg_f, wg_o), (wu_f, wu_o), (wd_f, wd_o)):
            w_o[...] = w_f[...].astype(bf16)


def _inproj_kernel(h_ref, g_ref, w_ref, cos_ref, sin_ref, q_ref, k_ref, v_ref, sg_ref, p_ref, gate_ref):
    un = _rms(h_ref[...], g_ref[...]).astype(bf16)
    gate_ref[...] = jax.nn.sigmoid(_dot(un, w_ref[:, _C_GATE:PROJ_W]))
    gr = _dot(un, w_ref[:, _C_G:_C_P])
    sg_ref[...] = gr * jax.nn.sigmoid(gr)
    cos = cos_ref[...]
    sin = sin_ref[...]
    q = _dot(un, w_ref[:, _C_Q:_C_K])
    q_ref[...] = q * cos + _swap_halves(q) * sin
    k = _dot(un, w_ref[:, _C_K:_C_V])
    k_ref[...] = (k * cos + _swap_halves(k) * sin) * (RET_DK ** -0.5)
    v_ref[...] = _dot(un, w_ref[:, _C_V:_C_G]).astype(v_ref.dtype)
    p_ref[...] = _dot(un, w_ref[:, _C_P:_C_GATE])


def _inproj(h, norm_g, w_in, cos, sin, *, tm, cast_ffn=None):
    rows = h.shape[0]
    tiles = rows // tm
    tab_blocks = cos.shape[0] // tm

    def rowspec(width):
        return pl.BlockSpec((tm, width), lambda i: (i, 0))

    tab = pl.BlockSpec((tm, RET_QK), lambda i: (i % tab_blocks, 0))
    widths = (RET_QK, RET_QK, RET_V, RET_V, SHIFT_W, GATE_W)
    dtypes = (f32, f32, bf16, f32, f32, f32)
    in_specs = [rowspec(D_MODEL), _resident((1, D_MODEL)), _resident((D_MODEL, PROJ_W)), tab, tab]
    out_specs = [rowspec(w) for w in widths]
    out_shape = [jax.ShapeDtypeStruct((rows, w), d) for w, d in zip(widths, dtypes)]
    operands = (h, norm_g, w_in, cos, sin)
    if cast_ffn is not None:
        assert FF_CHUNKS <= tiles
        weight_chunks = _ffn_chunk_specs(lambda i: jnp.minimum(i, FF_CHUNKS - 1))
        in_specs += weight_chunks
        out_specs += weight_chunks
        out_shape += _ffn_bf16_shapes()
        operands += tuple(cast_ffn)
    return pl.pallas_call(
        _inproj_kernel if cast_ffn is None else _inproj_cast_kernel,
        grid=(tiles,),
        in_specs=in_specs,
        out_specs=out_specs,
        out_shape=out_shape,
        compiler_params=pltpu.CompilerParams(dimension_semantics=("arbitrary",),
                                             vmem_limit_bytes=V7X_VMEM_LIMIT_BYTES),
        name="inproj",
    )(*operands)


def _ret_kernel(q_ref, k_ref, v_ref, sg_ref, s0_ref, gn_ref, o_ref, s_out_ref, s_scr, o_scr, *, nb, rows, valid):
    c = pl.program_id(1)
    L = rows

    @pl.when(c == 0)
    def _():
        s_scr[...] = jnp.broadcast_to(s0_ref[...], s_scr.shape)

    ii = lax.broadcasted_iota(jnp.int32, (L, L), 0)
    jj = lax.broadcasted_iota(jnp.int32, (L, L), 1)
    diff = (ii - jj).astype(f32)
    row = lax.broadcasted_iota(jnp.int32, (L, 1), 0).astype(f32)
    mask = [jnp.where(diff >= 0, jnp.exp(lg * jnp.maximum(diff, 0.0)), 0.0) for lg in LOG_GAMMA]
    q_decay = [jnp.exp(lg * (row + 1.0)) for lg in LOG_GAMMA]
    k_decay = [jnp.exp(lg * (valid - 1.0 - row)) for lg in LOG_GAMMA]
    s_decay = [float(np.exp(lg * valid)) for lg in LOG_GAMMA]

    chains = [(j, h) for j in range(nb) for h in range(RET_HEADS)]
    qh = [q_ref[j, :, RET_DK * h:RET_DK * (h + 1)] for j, h in chains]
    kh = [k_ref[j, :, RET_DK * h:RET_DK * (h + 1)] for j, h in chains]
    vh = [v_ref[j, :, RET_DV * h:RET_DV * (h + 1)] for j, h in chains]
    scores = [(_dot_nt(qh[i].astype(bf16), kh[i].astype(bf16)) * mask[h]).astype(bf16)
              for i, (j, h) in enumerate(chains)]
    qd = [(qh[i] * q_decay[h]).astype(bf16) for i, (j, h) in enumerate(chains)]
    kd = [(kh[i] * k_decay[h]).astype(bf16) for i, (j, h) in enumerate(chains)]
    s_old = [s_scr[j, h] for j, h in chains]
    for i, (j, h) in enumerate(chains):
        o_scr[j * L:(j + 1) * L, RET_DV * h:RET_DV * (h + 1)] = (
            _dot(scores[i], vh[i]) + _dot(qd[i], s_old[i].astype(bf16)))
    for i, (j, h) in enumerate(chains):
        s_scr[j, h] = s_decay[h] * s_old[i] + _dot_tn(kd[i], vh[i])

    gn = gn_ref[...]
    sg = sg_ref[...].reshape(nb * L, RET_V)
    for h in range(RET_HEADS):
        sl = slice(RET_DV * h, RET_DV * (h + 1))
        o = o_scr[:, sl]
        mu = jnp.mean(o, axis=-1, keepdims=True)
        oc = o - mu
        var = jnp.mean(oc * oc, axis=-1, keepdims=True)
        out = oc * lax.rsqrt(var + RET_GN_EPS) * gn[:, sl] * sg[:, sl]
        o_ref[:, :, sl] = out.reshape(nb, L, RET_DV).astype(o_ref.dtype)

    @pl.when(c == pl.num_programs(1) - 1)
    def _():
        s_out_ref[...] = s_scr[...]


def _retention(q, k, v, sg, s0, gn, *, nb, rows, valid):
    B, T, _ = q.shape
    bcast = s0.shape[0] == 1

    def seq(width):
        return pl.BlockSpec((nb, rows, width), lambda b, c: (b, c, 0))

    state = pl.BlockSpec((nb, RET_HEADS, RET_DK, RET_DV), lambda b, c: (b, 0, 0, 0))
    state_in = pl.BlockSpec((1, RET_HEADS, RET_DK, RET_DV), lambda b, c: (0, 0, 0, 0)) if bcast else state
    return pl.pallas_call(
        functools.partial(_ret_kernel, nb=nb, rows=rows, valid=valid),
        grid=(B // nb, T // rows),
        in_specs=[seq(RET_QK), seq(RET_QK), seq(RET_V), seq(RET_V), state_in, _resident((1, RET_V))],
        out_specs=[seq(RET_V), state],
        out_shape=[jax.ShapeDtypeStruct((B, T, RET_V), bf16),
                   jax.ShapeDtypeStruct((B, RET_HEADS, RET_DK, RET_DV), f32)],
        scratch_shapes=[pltpu.VMEM((nb, RET_HEADS, RET_DK, RET_DV), f32), pltpu.VMEM((nb * rows, RET_V), f32)],
        compiler_params=pltpu.CompilerParams(dimension_semantics=("parallel", "arbitrary"),
                                             vmem_limit_bytes=V7X_VMEM_LIMIT_BYTES),
        name="retention",
    )(q, k, v, sg, s0, gn)


def _head_sums(x, ones_bd):
    rows = x.shape[0]
    width = ones_bd.shape[0]
    groups = RWKV_W // width
    stacked = jnp.concatenate([x[:, width * i:width * (i + 1)] for i in range(groups)], axis=0)
    z = _dot(stacked.astype(bf16), ones_bd)
    return jnp.concatenate([z[rows * i:rows * (i + 1)] for i in range(groups)], axis=1)


def _rwkv_features(p, p_prev, mu_ref, w0_ref, w2_ref, a0_ref, a2_ref, g2_ref, kk_ref, ka_ref, ones_bd):
    pm = p + (p_prev - p) * mu_ref[...]
    r = pm[:, 0:RWKV_W]
    k = pm[:, RWKV_W:2 * RWKV_W]
    v = pm[:, 2 * RWKV_W:3 * RWKV_W]
    o_w = 3 * RWKV_W
    xw = pm[:, o_w:o_w + DECAY_LORA]
    xa = pm[:, o_w + DECAY_LORA:o_w + DECAY_LORA + AAA_LORA]
    xg = pm[:, o_w + DECAY_LORA + AAA_LORA:SHIFT_W]
    z = w0_ref[...] + _dot(jnp.tanh(xw).astype(bf16), w2_ref[...])
    ld = -float(np.exp(-0.5)) * jax.nn.sigmoid(z)
    a = jax.nn.sigmoid(a0_ref[...] + _dot(xa.astype(bf16), a2_ref[...]))
    g = _dot(jax.nn.sigmoid(xg).astype(bf16), g2_ref[...])
    kk = k * kk_ref[...]
    kk = kk * lax.rsqrt(jnp.maximum(_head_sums(kk * kk, ones_bd), 1e-24))
    kp = k * (1.0 + (a - 1.0) * ka_ref[...])
    return r, kp, v, kk, a, ld, g


def _rwkv_output(y, bonus_rkv, g, lng_ref, lnb_ref, ones_bd):
    inv_n = 1.0 / RWKV_HD
    mean = _head_sums(y, ones_bd) * inv_n
    yc = y - mean
    var = _head_sums(yc * yc, ones_bd) * inv_n
    out = yc * lax.rsqrt(var + RWKV_GN_EPS) * lng_ref[...] + lnb_ref[...]
    rkr, v = bonus_rkv
    bonus = _head_sums(rkr, ones_bd) * v
    return (out + bonus) * g


def _rwkv_kernel(p_ref, prev0_ref, h0_ref, mu_ref, w0_ref, w2_ref, a0_ref, a2_ref, g2_ref, kk_ref, ka_ref,
                 rk_ref, lng_ref, lnb_ref, ones_ref, o_ref, h_out_ref, shift_ref, s_scr, prev_scr, y_scr,
                 *, nb, rows, valid):
    c = pl.program_id(1)
    C = rows
    N = RWKV_HD
    R = nb * C
    assert C & (C - 1) == 0
    log2c = C.bit_length() - 1

    @pl.when(c == 0)
    def _():
        for j in range(nb):
            for h in range(RWKV_HEADS):
                s_scr[j, :, N * h:N * (h + 1)] = h0_ref[j if h0_ref.shape[0] > 1 else 0, h]
        prev_scr[...] = jnp.broadcast_to(prev0_ref[...], prev_scr.shape)

    p = p_ref[...].reshape(R, SHIFT_W)
    rowid = lax.broadcasted_iota(jnp.int32, (R, 1), 0)
    step = rowid & (C - 1)
    p_prev = pltpu.roll(p, 1, 0)
    for j in range(nb):
        p_prev = jnp.where(rowid == j * C, prev_scr[j], p_prev)
    for j in range(nb):
        prev_scr[j] = p[j * C + valid - 1:j * C + valid, :]
    ones_bd = ones_ref[...]
    r, kp, v, kk, a, ld, g = _rwkv_features(p, p_prev, mu_ref, w0_ref, w2_ref, a0_ref, a2_ref, g2_ref, kk_ref,
                                            ka_ref, ones_bd)
    if valid < C:
        live = (step < valid).astype(f32)
        ld = ld * live
        kk = kk * live
        kp = kp * live
        v = v * live
    b = kk * a

    cum = ld
    for level in range(log2c):
        reach = 1 << level
        cum = cum + jnp.where(step >= reach, pltpu.roll(cum, reach, 0), 0.0)
    last_rows = [cum[j * C + C - 1:j * C + C, :] for j in range(nb)]
    cum_last = jnp.concatenate([jnp.broadcast_to(x, (C, RWKV_W)) for x in last_rows], axis=0)
    e_in = jnp.exp(cum)
    e_ex = jnp.exp(cum - ld)
    e_neg = jnp.exp(-cum)
    e_end = jnp.exp(cum_last - cum)
    g_end = [jnp.exp(x) for x in last_rows]
    at = -kk * e_ex
    rt = r * e_in
    bt = (b * e_neg).astype(bf16)
    kt = (kp * e_neg).astype(bf16)
    bh = (b * e_end).astype(bf16)
    kh = (kp * e_end).astype(bf16)
    vb = v.astype(bf16)

    G = RWKV_LANE_HEADS
    GW = G * N
    log2n = N.bit_length() - 1
    def head_masks(width, log2_block):
        lane = lax.broadcasted_iota(jnp.int32, (1, width), 1)
        return [((lane >> log2_block) & (G - 1)) == h for h in range(G)]

    head_of_lane = head_masks(GW, log2n)
    head_of_lane2 = head_masks(2 * GW, log2n)
    head_of_col = head_masks(G * C, log2c)
    head_of_col2 = head_masks(2 * G * C, log2c)
    ti = lax.broadcasted_iota(jnp.int32, (C, G * C), 0)
    tj = lax.broadcasted_iota(jnp.int32, (C, G * C), 1) & (C - 1)
    incl = ti >= tj
    strict = ti > tj
    unit = ti == tj
    gi = lax.broadcasted_iota(jnp.int32, (GW, GW), 0)
    gj = lax.broadcasted_iota(jnp.int32, (GW, GW), 1)
    eye_g = gi == gj
    same_head = (gi >> log2n) == (gj >> log2n)

    def bd(a, masks):
        return jnp.concatenate([jnp.where(m, a, jnp.zeros_like(a)) for m in masks], axis=0)

    groups = [(j, q) for j in range(nb) for q in range(RWKV_HEADS // G)]
    n_gr = len(groups)

    def grp(x, j, q):
        return x[j * C:(j + 1) * C, GW * q:GW * (q + 1)]

    ar = [jnp.concatenate([grp(at, j, q), grp(rt, j, q)], axis=0).astype(bf16) for j, q in groups]
    gbk = [_dot_nt(ar[i], jnp.concatenate([bd(grp(bt, j, q), head_of_lane), bd(grp(kt, j, q), head_of_lane)],
                                          axis=0)) for i, (j, q) in enumerate(groups)]
    gb = [x[:, :G * C] for x in gbk]
    gk = [x[:, G * C:] for x in gbk]
    lab = [jnp.where(strict, x[:C], 0.0) for x in gb]
    mrb = [jnp.where(incl, x[C:], 0.0).astype(bf16) for x in gb]
    lmk = [jnp.concatenate([jnp.where(strict, x[:C], 0.0), jnp.where(incl, x[C:], 0.0)], axis=0).astype(bf16)
           for x in gk]
    lmv = [_dot(lmk[i], bd(grp(vb, j, q), head_of_lane)) for i, (j, q) in enumerate(groups)]
    m = [_dot(x.astype(bf16), bd(x.astype(bf16), head_of_col)) for x in lab]
    t_inv = [jnp.where(unit, 1.0, x) for x in lab]
    for level in range(1, log2c):
        mb = [x.astype(bf16) for x in m]
        if level < log2c - 1:
            out = [_dot(mb[i], bd(jnp.concatenate([t_inv[i].astype(bf16), mb[i]], axis=1), head_of_col2))
                   for i in range(n_gr)]
            t_inv = [t_inv[i] + out[i][:, :G * C] for i in range(n_gr)]
            m = [x[:, G * C:] for x in out]
        else:
            t_inv = [t_inv[i] + _dot(mb[i], bd(t_inv[i].astype(bf16), head_of_col)) for i in range(n_gr)]
    wu = [_dot(t_inv[i].astype(bf16),
               bd(jnp.concatenate([grp(at, j, q).astype(bf16), lmv[i][:C].astype(bf16)], axis=1), head_of_lane2)
               ).astype(bf16) for i, (j, q) in enumerate(groups)]
    wa = [x[:, :GW] for x in wu]
    uv = [x[:, GW:] for x in wu]
    pq = [_dot(mrb[i], bd(wu[i], head_of_lane2)) for i in range(n_gr)]
    ry = [grp(rt, j, q) + pq[i][:, :GW] for i, (j, q) in enumerate(groups)]
    y0 = [lmv[i][C:] + pq[i][:, GW:] for i in range(n_gr)]
    gwt = [_dot_tn(wa[i], grp(bh, j, q)) for i, (j, q) in enumerate(groups)]
    dkf = [_dot_tn(jnp.concatenate([uv[i], grp(vb, j, q)], axis=0),
                   jnp.concatenate([grp(bh, j, q), grp(kh, j, q)], axis=0))
           for i, (j, q) in enumerate(groups)]
    for i, (j, q) in enumerate(groups):
        sl = slice(GW * q, GW * (q + 1))
        wg = jnp.where(same_head, gwt[i], 0.0) + jnp.where(eye_g, g_end[j][:, sl], 0.0)
        dk = jnp.where(head_of_lane[0], dkf[i][0:N], 0.0)
        for h in range(1, G):
            dk = dk + jnp.where(head_of_lane[h], dkf[i][N * h:N * (h + 1)], 0.0)
        s_old = s_scr[j, :, sl].astype(bf16)
        y_scr[j * C:(j + 1) * C, sl] = _dot_nt(ry[i].astype(bf16), bd(s_old, head_of_lane)) + y0[i]
        s_scr[j, :, sl] = _dot(s_old, wg.astype(bf16)) + dk

    out = _rwkv_output(y_scr[...], (r * kp * rk_ref[...], v), g, lng_ref, lnb_ref, ones_bd)
    o_ref[...] = out.reshape(nb, C, RWKV_W).astype(o_ref.dtype)

    @pl.when(c == pl.num_programs(1) - 1)
    def _():
        for j in range(nb):
            for h in range(RWKV_HEADS):
                h_out_ref[j, h] = s_scr[j, :, N * h:N * (h + 1)]
        shift_ref[...] = prev_scr[...]


def _rwkv(p, prev0, h0, params, ones_bd, *, nb, rows, valid):
    B, T, _ = p.shape

    def maybe_bcast(arr, tail):
        nd = len(tail) + 1
        if arr.shape[0] == 1:
            return pl.BlockSpec((1,) + tail, lambda b, c: (0,) * nd)
        return pl.BlockSpec((nb,) + tail, lambda b, c: (b,) + (0,) * (nd - 1))

    seq_in = pl.BlockSpec((nb, rows, SHIFT_W), lambda b, c: (b, c, 0))
    seq_out = pl.BlockSpec((nb, rows, RWKV_W), lambda b, c: (b, c, 0))
    st_tail = (RWKV_HEADS, RWKV_HD, RWKV_HD)
    sh_tail = (1, SHIFT_W)
    param_specs = [_resident(x.shape) for x in params]
    return pl.pallas_call(
        functools.partial(_rwkv_kernel, nb=nb, rows=rows, valid=valid),
        grid=(B // nb, T // rows),
        in_specs=[seq_in, maybe_bcast(prev0, sh_tail), maybe_bcast(h0, st_tail)] + param_specs
                 + [_resident(ones_bd.shape)],
        out_specs=[seq_out, pl.BlockSpec((nb,) + st_tail, lambda b, c: (b, 0, 0, 0)),
                   pl.BlockSpec((nb,) + sh_tail, lambda b, c: (b, 0, 0))],
        out_shape=[jax.ShapeDtypeStruct((B, T, RWKV_W), bf16),
                   jax.ShapeDtypeStruct((B,) + st_tail, f32),
                   jax.ShapeDtypeStruct((B,) + sh_tail, f32)],
        scratch_shapes=[pltpu.VMEM((nb, RWKV_HD, RWKV_W), f32), pltpu.VMEM((nb,) + sh_tail, f32),
                        pltpu.VMEM((nb * rows, RWKV_W), f32)],
        compiler_params=pltpu.CompilerParams(dimension_semantics=("parallel", "arbitrary"),
                                             vmem_limit_bytes=V7X_VMEM_LIMIT_BYTES),
        name="rwkv7",
    )(p, prev0, h0, *params, ones_bd)


def _rwkv_step_kernel(p_ref, shift0_ref, s_ref, mu_ref, w0_ref, w2_ref, a0_ref, a2_ref, g2_ref, kk_ref, ka_ref,
                      rk_ref, lng_ref, lnb_ref, ones_ref, o_ref, s_out_ref, shift_ref,
                      feat_scr, y_scr, g_scr, rkr_scr, v_scr, *, steps, batch):
    h = pl.program_id(0)
    T, B, N = steps, batch, RWKV_HD
    key_tiles = N // 8
    f_r, f_w, f_k, f_v, f_a, f_b = range(6)

    @pl.when(h == 0)
    def _():
        p = p_ref[...]
        p_prev = jnp.concatenate([shift0_ref[...], p[:(T - 1) * B]], axis=0)
        r, kp, v, kk, a, ld, g = _rwkv_features(p, p_prev, mu_ref, w0_ref, w2_ref, a0_ref, a2_ref, g2_ref,
                                                kk_ref, ka_ref, ones_ref[...])
        for idx, x in enumerate((r, jnp.exp(ld), kp, v, -kk, kk * a)):
            for t in range(T):
                feat_scr[idx, t] = x[t * B:(t + 1) * B, :].T
        g_scr[...] = g
        rkr_scr[...] = r * kp * rk_ref[...]
        v_scr[...] = v
        shift_ref[...] = p[(T - 1) * B:]

    base = pl.multiple_of(h * N, N)

    def tile(idx, t, kt):
        return feat_scr[idx, t, pl.ds(pl.multiple_of(base + 8 * kt, 8), 8), :]

    def keysum(s, idx, t):
        acc = s[0] * tile(idx, t, 0)
        for kt in range(1, key_tiles):
            acc = acc + s[kt] * tile(idx, t, kt)
        return jnp.sum(acc, axis=0, keepdims=True)

    def value_group(vg, carry):
        v_rows = pl.ds(pl.multiple_of(base + 8 * vg, 8), 8)
        v_tiles = [feat_scr[f_v, t, v_rows, :] for t in range(T)]
        y_rows = [[] for _ in range(T)]
        for i in range(8):
            row0 = pl.multiple_of((8 * vg + i) * N, N)
            s = [s_ref[0, pl.ds(row0 + 8 * kt, 8), :] for kt in range(key_tiles)]
            for t in range(T):
                sa = jnp.broadcast_to(keysum(s, f_a, t), (8, B))
                vv = jnp.broadcast_to(v_tiles[t][i:i + 1, :], (8, B))
                s = [s[kt] * tile(f_w, t, kt) + sa * tile(f_b, t, kt) + vv * tile(f_k, t, kt)
                     for kt in range(key_tiles)]
                y_rows[t].append(keysum(s, f_r, t))
            for kt in range(key_tiles):
                s_out_ref[0, pl.ds(row0 + 8 * kt, 8), :] = s[kt]
        for t in range(T):
            y_scr[t, v_rows, :] = jnp.concatenate(y_rows[t], axis=0)
        return carry

    lax.fori_loop(0, N // 8, value_group, 0)

    @pl.when(h == pl.num_programs(0) - 1)
    def _():
        y = jnp.concatenate([y_scr[t].T for t in range(T)], axis=0)
        out = _rwkv_output(y, (rkr_scr[...], v_scr[...]), g_scr[...], lng_ref, lnb_ref, ones_ref[...])
        o_ref[...] = out.astype(o_ref.dtype)


def _rwkv_step(p, shift0, state, params, ones_bd, *, steps):
    B = state.shape[0]
    rows = steps * B
    n_state = RWKV_HD * RWKV_HD
    s_in = jnp.transpose(state, (1, 2, 3, 0)).reshape(RWKV_HEADS, n_state, B)
    s_spec = pl.BlockSpec((1, n_state, B), lambda h: (h, 0, 0))
    o, s_out, shift = pl.pallas_call(
        functools.partial(_rwkv_step_kernel, steps=steps, batch=B),
        grid=(RWKV_HEADS,),
        in_specs=[_resident((rows, SHIFT_W)), _resident((B, SHIFT_W)), s_spec]
                 + [_resident(x.shape) for x in params] + [_resident(ones_bd.shape)],
        out_specs=[pl.BlockSpec((rows, RWKV_W), lambda h: (0, 0)), s_spec,
                   pl.BlockSpec((B, SHIFT_W), lambda h: (0, 0))],
        out_shape=[jax.ShapeDtypeStruct((rows, RWKV_W), bf16),
                   jax.ShapeDtypeStruct((RWKV_HEADS, n_state, B), f32),
                   jax.ShapeDtypeStruct((B, SHIFT_W), f32)],
        scratch_shapes=[pltpu.VMEM((6, steps, RWKV_W, B), f32), pltpu.VMEM((steps, RWKV_W, B), f32),
                        pltpu.VMEM((rows, RWKV_W), f32), pltpu.VMEM((rows, RWKV_W), f32),
                        pltpu.VMEM((rows, RWKV_W), f32)],
        compiler_params=pltpu.CompilerParams(dimension_semantics=("arbitrary",),
                                             vmem_limit_bytes=V7X_VMEM_LIMIT_BYTES),
        name="rwkv7_step",
    )(p, shift0, s_in, *params, ones_bd)
    s_out = jnp.transpose(s_out.reshape(RWKV_HEADS, RWKV_HD, RWKV_HD, B), (3, 0, 1, 2))
    return o, s_out, shift


def _merge_ffn_kernel(h_ref, oret_ref, orwkv_ref, gate_ref, wr_ref, ww_ref, wo_ref,
                      g_ref, wg_ref, wu_ref, wd_ref, fin_ref, o_ref):
    a = _dot(oret_ref[...], wr_ref[...])
    b = _dot(orwkv_ref[...], ww_ref[...])
    merged = gate_ref[:, :D_MODEL] * a + gate_ref[:, D_MODEL:] * b
    h = h_ref[...] + _dot(merged.astype(bf16), wo_ref[...])
    h = _swiglu_half_step(h, g_ref, _column_chunks(wg_ref), _column_chunks(wu_ref), _row_chunks(wd_ref))
    o_ref[...] = _rms(h, fin_ref[...])


def _merge_ffn(h, o_ret, o_rwkv, gates, w_out_ret, w_out_rwkv, w_out, norm_g, wg, wu, wd, fin_g, *, tm):
    rows = o_ret.shape[0]
    assert rows % tm == 0 and o_rwkv.shape[0] == rows and h.shape[0] >= rows and gates.shape[0] >= rows

    def rowspec(width):
        return pl.BlockSpec((tm, width), lambda i: (i, 0))

    return pl.pallas_call(
        _merge_ffn_kernel,
        grid=(rows // tm,),
        in_specs=[rowspec(D_MODEL), rowspec(RET_V), rowspec(RWKV_W), rowspec(GATE_W),
                  _resident((RET_V, D_MODEL)), _resident((RWKV_W, D_MODEL)), _resident((D_MODEL, D_MODEL))]
                 + [_resident(s) for s in _FFN_WEIGHT_SPECS] + [_resident((1, D_MODEL))],
        out_specs=rowspec(D_MODEL),
        out_shape=jax.ShapeDtypeStruct((rows, D_MODEL), f32),
        compiler_params=pltpu.CompilerParams(dimension_semantics=("parallel",),
                                             vmem_limit_bytes=V7X_VMEM_LIMIT_BYTES),
        name="merge_ffn",
    )(h, o_ret, o_rwkv, gates, w_out_ret, w_out_rwkv, w_out, norm_g, wg, wu, wd, fin_g)


def _rotary_tables(pos):
    half = RET_DK // 2
    inv_freq = ROPE_BASE ** (-jnp.arange(half, dtype=f32) / half)
    ang = pos.astype(f32)[:, None] * inv_freq[None, :]
    cos = jnp.cos(ang)
    sin = jnp.sin(ang)
    cos_t = jnp.tile(jnp.concatenate([cos, cos], axis=1), (1, RET_HEADS))
    sin_t = jnp.tile(jnp.concatenate([-sin, sin], axis=1), (1, RET_HEADS))
    return cos_t, sin_t


def _row_tile(rows, target):
    tm = min(rows, target)
    while rows % tm:
        tm -= 8
    return tm


def kernel(x_prompt, x_sample, state_ret, state_wkv, state_shift, meta_tokens, ffn1_norm, ffn1_w_gate, ffn1_w_up, ffn1_w_down, mix_norm, w_in, ret_gn_g, mu_shift, w0, w2, a0, a2, g2, k_k, k_a, r_k, lnx_g, lnx_b, w_out_ret, w_out_rwkv, w_out, ffn2_norm, ffn2_w_gate, ffn2_w_up, ffn2_w_down, final_norm):
    Bp, Tp, _ = x_prompt.shape
    Bs, Ts, _ = x_sample.shape
    st_dtype = state_ret.dtype

    def row(x):
        return x.reshape(1, -1).astype(f32)

    fin = row(final_norm)
    mix_g = row(mix_norm)
    wr_b, ww_b, wo_b = w_out_ret.astype(bf16), w_out_rwkv.astype(bf16), w_out.astype(bf16)
    rwkv_params = (row(mu_shift), row(w0), w2.astype(bf16), row(a0), a2.astype(bf16), g2.astype(bf16),
                   row(k_k), row(k_a), row(r_k), row(lnx_g), row(lnx_b))
    head_id = jnp.arange(HEAD_SUM_LANES, dtype=jnp.int32) // RWKV_HD
    ones_bd = (head_id[:, None] == head_id[None, :]).astype(bf16)
    gn = row(ret_gn_g)

    n_s = Bs * Ts
    x_small = jnp.concatenate([jnp.swapaxes(x_sample, 0, 1).reshape(n_s, D_MODEL),
                               meta_tokens.astype(x_sample.dtype)], axis=0)
    cos_s, sin_s = _rotary_tables(PAST_LEN + jnp.arange(Ts, dtype=jnp.int32))
    cos_m, sin_m = _rotary_tables(jnp.arange(N_META, dtype=jnp.int32))
    cos_small = jnp.concatenate([jnp.repeat(cos_s, Bs, axis=0), cos_m], axis=0)
    sin_small = jnp.concatenate([jnp.repeat(sin_s, Bs, axis=0), sin_m], axis=0)
    n_small = n_s + N_META
    h_small, *ffn1_b = _ffn_stream(x_small, row(ffn1_norm), ffn1_w_gate, ffn1_w_up, ffn1_w_down)

    n_p = Bp * Tp
    h_p, w_in_b = _ffn(x_prompt.reshape(n_p, D_MODEL), row(ffn1_norm), *ffn1_b, w_in, tm=_row_tile(n_p, 512))
    q_s, k_s, v_s, sg_s, p_s, gate_s = _inproj(h_small, mix_g, w_in_b, cos_small, sin_small, tm=n_small)

    def meta(x):
        return x[n_s:].reshape(1, N_META, -1)

    zeros_ret = jnp.zeros((1, RET_HEADS, RET_DK, RET_DV), f32)
    _, s_ret_m = _retention(meta(q_s), meta(k_s), meta(v_s), meta(sg_s), zeros_ret, gn,
                            nb=1, rows=N_META, valid=N_META)
    _, h_wkv_m, shift_m = _rwkv(meta(p_s), jnp.zeros((1, 1, SHIFT_W), f32),
                                jnp.zeros((1, RWKV_HEADS, RWKV_HD, RWKV_HD), f32), rwkv_params, ones_bd,
                                nb=1, rows=N_META, valid=N_META)

    pad_t = -(-Ts // 8) * 8

    def samp(x):
        x = jnp.swapaxes(x[:n_s].reshape(Ts, Bs, -1), 0, 1)
        return jnp.pad(x, ((0, 0), (0, pad_t - Ts), (0, 0)))

    o_ret_s, s_ret_s = _retention(samp(q_s), samp(k_s), samp(v_s), samp(sg_s), state_ret.astype(f32), gn,
                                  nb=RET_SAMPLE_SEQS, rows=pad_t, valid=Ts)
    o_ret_s = jnp.swapaxes(o_ret_s[:, :Ts], 0, 1).reshape(n_s, RET_V)
    o_wkv_s, h_wkv_s, shift_s = _rwkv_step(p_s, state_shift.astype(f32), state_wkv.astype(f32),
                                           rwkv_params, ones_bd, steps=Ts)

    cos_p, sin_p = _rotary_tables(N_META + jnp.arange(Tp, dtype=jnp.int32))
    q_p, k_p, v_p, sg_p, p_p, gate_p, *ffn2_b = _inproj(h_p, mix_g, w_in_b, cos_p, sin_p, tm=_row_tile(Tp, 512),
                                                        cast_ffn=(ffn2_w_gate, ffn2_w_up, ffn2_w_down))

    def post(h, o_ret, o_rwkv, gates, tm):
        return _merge_ffn(h, o_ret, o_rwkv, gates, wr_b, ww_b, wo_b, row(ffn2_norm), *ffn2_b, fin, tm=tm)

    y_sample = post(h_small, o_ret_s, o_wkv_s, gate_s, _row_tile(n_s, 512))
    y_sample = jnp.swapaxes(y_sample.reshape(Ts, Bs, D_MODEL), 0, 1)

    def seqs(x):
        return x.reshape(Bp, Tp, -1)

    o_ret_p, s_ret_p = _retention(seqs(q_p), seqs(k_p), seqs(v_p), seqs(sg_p), s_ret_m, gn,
                                  nb=RET_PROMPT_SEQS, rows=RET_CHUNK, valid=RET_CHUNK)
    o_wkv_p, h_wkv_p, shift_p = _rwkv(seqs(p_p), shift_m, h_wkv_m, rwkv_params, ones_bd,
                                      nb=RWKV_PROMPT_SEQS, rows=RWKV_CHUNK, valid=RWKV_CHUNK)
    y_prompt = post(h_p, o_ret_p.reshape(n_p, RET_V), o_wkv_p.reshape(n_p, RWKV_W), gate_p, _row_tile(n_p, 512))

    return (y_prompt.reshape(Bp, Tp, D_MODEL).astype(x_prompt.dtype),
            y_sample.reshape(Bs, Ts, D_MODEL).astype(x_sample.dtype),
            s_ret_p.astype(st_dtype), h_wkv_p.astype(st_dtype),
            shift_p.reshape(Bp, SHIFT_W).astype(st_dtype),
            s_ret_s.astype(st_dtype), h_wkv_s.astype(st_dtype),
            shift_s.reshape(Bs, SHIFT_W).astype(st_dtype))
```

```python
import functools

import numpy as np
import jax
import jax.numpy as jnp
from jax import lax
from jax.experimental import pallas as pl
from jax.experimental.pallas import tpu as pltpu

D_MODEL = 1024
N_META = 16
PAST_LEN = 16384
RET_HEADS = 4
RET_DK = 64
RET_DV = 128
RET_CHUNK = 128
RWKV_HEADS = 8
RWKV_HD = 64
RWKV_W = RWKV_HEADS * RWKV_HD
DECAY_LORA = 64
AAA_LORA = 64
GATE_LORA = 128
D_FF = 2816
ROPE_BASE = 10000.0
NORM_EPS = 1e-6
RET_GN_EPS = 1e-6
RWKV_GN_EPS = 64e-5
RET_QK = RET_HEADS * RET_DK
RET_V = RET_HEADS * RET_DV
SHIFT_W = 3 * RWKV_W + DECAY_LORA + AAA_LORA + GATE_LORA
GATE_W = 2 * D_MODEL
PROJ_W = 2 * RET_QK + 2 * RET_V + SHIFT_W + GATE_W

_C_Q, _C_K, _C_V, _C_G = 0, RET_QK, 2 * RET_QK, 2 * RET_QK + RET_V
_C_P = 2 * RET_QK + 2 * RET_V
_C_GATE = _C_P + SHIFT_W

V7X_VMEM_LIMIT_BYTES = 56 * 1024 * 1024
FF_CHUNK = 256
FFN_STREAM_SLOTS = 3
RWKV_CHUNK = 64
RWKV_LANE_HEADS = 2
HEAD_SUM_LANES = 256
RWKV_PROMPT_SEQS = 8
RET_PROMPT_SEQS = 8
RET_SAMPLE_SEQS = 32
LOG_GAMMA = tuple(float(np.log1p(-2.0 ** (-5.0 - h))) for h in range(RET_HEADS))

f32 = jnp.float32
bf16 = jnp.bfloat16


def _resident(shape):
    zeros = (0,) * len(shape)
    return pl.BlockSpec(shape, lambda *_: zeros, pipeline_mode=pl.Buffered(1))


def _rms(x, g):
    return x * lax.rsqrt(jnp.mean(x * x, axis=-1, keepdims=True) + NORM_EPS) * g


def _dot(a, b):
    return jnp.dot(a, b, preferred_element_type=f32)


def _dot_nt(a, b):
    return lax.dot_general(a, b, (((1,), (1,)), ((), ())), preferred_element_type=f32)


def _dot_tn(a, b):
    return lax.dot_general(a, b, (((0,), (0,)), ((), ())), preferred_element_type=f32)


FF_CHUNKS = D_FF // FF_CHUNK


def _swiglu_half_step(x, g_ref, gate, up, down):
    xn = _rms(x, g_ref[...]).astype(bf16)
    acc = jnp.zeros(x.shape, f32)
    for c in range(FF_CHUNKS):
        gt = _dot(xn, gate(c))
        ut = _dot(xn, up(c))
        act = (gt * jax.nn.sigmoid(gt) * ut).astype(bf16)
        acc = acc + _dot(act, down(c))
    return x + 0.5 * acc


def _column_chunks(w_ref):
    return lambda c: w_ref[:, c * FF_CHUNK:(c + 1) * FF_CHUNK]


def _row_chunks(w_ref):
    return lambda c: w_ref[c * FF_CHUNK:(c + 1) * FF_CHUNK, :]


def _ffn_kernel(x_ref, g_ref, wg_ref, wu_ref, wd_ref, o_ref):
    o_ref[...] = _swiglu_half_step(x_ref[...], g_ref, _column_chunks(wg_ref), _column_chunks(wu_ref),
                                   _row_chunks(wd_ref))


_FFN_WEIGHT_SPECS = ((1, D_MODEL), (D_MODEL, D_FF), (D_MODEL, D_FF), (D_FF, D_MODEL))


def _ffn_chunk_specs(step_of):
    col = pl.BlockSpec((D_MODEL, FF_CHUNK), lambda i: (0, step_of(i)))
    return [col, col, pl.BlockSpec((FF_CHUNK, D_MODEL), lambda i: (step_of(i), 0))]


def _ffn_bf16_shapes():
    return [jax.ShapeDtypeStruct(s, bf16) for s in _FFN_WEIGHT_SPECS[1:]]


def _ffn_stream_kernel(x_ref, g_ref, wg_hbm, wu_hbm, wd_hbm, o_ref, wg_o, wu_o, wd_o, xn_s, acc_s,
                       wg_buf, wu_buf, wd_buf, sems):
    c = pl.program_id(0)
    n = pl.num_programs(0)

    def chunk_copies(chunk, slot):
        cols = pl.ds(pl.multiple_of(chunk * FF_CHUNK, FF_CHUNK), FF_CHUNK)
        return (pltpu.make_async_copy(wg_hbm.at[:, cols], wg_buf.at[slot], sems.at[0, slot]),
                pltpu.make_async_copy(wu_hbm.at[:, cols], wu_buf.at[slot], sems.at[1, slot]),
                pltpu.make_async_copy(wd_hbm.at[cols, :], wd_buf.at[slot], sems.at[2, slot]))

    @pl.when(c == 0)
    def _():
        for first in range(FFN_STREAM_SLOTS):
            for which, copy in enumerate(chunk_copies(first, first)):
                copy.start(priority=which % 2)
        xn_s[...] = _rms(x_ref[...], g_ref[...]).astype(bf16)
        acc_s[...] = jnp.zeros(acc_s.shape, f32)

    slot = c % FFN_STREAM_SLOTS
    for copy in chunk_copies(c, slot):
        copy.wait()
    wg = wg_buf[slot].astype(bf16)
    wu = wu_buf[slot].astype(bf16)
    wd = wd_buf[slot].astype(bf16)
    wg_o[...] = wg
    wu_o[...] = wu
    wd_o[...] = wd
    xn = xn_s[...]
    gt = _dot(xn, wg)
    ut = _dot(xn, wu)
    act = (gt * jax.nn.sigmoid(gt) * ut).astype(bf16)
    acc_s[...] = acc_s[...] + _dot(act, wd)

    @pl.when(c + FFN_STREAM_SLOTS < n)
    def _():
        for which, copy in enumerate(chunk_copies(c + FFN_STREAM_SLOTS, slot)):
            copy.start(priority=which % 2)

    @pl.when(c == n - 1)
    def _():
        o_ref[...] = x_ref[...] + 0.5 * acc_s[...]


def _ffn_stream(x, norm_g, wg, wu, wd):
    rows = x.shape[0]
    assert FF_CHUNKS >= FFN_STREAM_SLOTS
    weight_chunks = _ffn_chunk_specs(lambda c: c)
    whole = pl.BlockSpec((rows, D_MODEL), lambda c: (0, 0))
    hbm = pl.BlockSpec(memory_space=pl.ANY)
    return pl.pallas_call(
        _ffn_stream_kernel,
        grid=(FF_CHUNKS,),
        in_specs=[_resident((rows, D_MODEL)), _resident((1, D_MODEL)), hbm, hbm, hbm],
        out_specs=[whole] + weight_chunks,
        out_shape=[jax.ShapeDtypeStruct((rows, D_MODEL), f32)] + _ffn_bf16_shapes(),
        scratch_shapes=[pltpu.VMEM((rows, D_MODEL), bf16), pltpu.VMEM((rows, D_MODEL), f32),
                        pltpu.VMEM((FFN_STREAM_SLOTS, D_MODEL, FF_CHUNK), f32),
                        pltpu.VMEM((FFN_STREAM_SLOTS, D_MODEL, FF_CHUNK), f32),
                        pltpu.VMEM((FFN_STREAM_SLOTS, FF_CHUNK, D_MODEL), f32),
                        pltpu.SemaphoreType.DMA((3, FFN_STREAM_SLOTS))],
        compiler_params=pltpu.CompilerParams(dimension_semantics=("arbitrary",),
                                             vmem_limit_bytes=V7X_VMEM_LIMIT_BYTES),
        name="ffn_stream",
    )(x, norm_g, wg, wu, wd)


def _ffn_side_kernel(x_ref, g_ref, wg_ref, wu_ref, wd_ref, win_f, o_ref, win_o, *, in_chunks):
    _ffn_kernel(x_ref, g_ref, wg_ref, wu_ref, wd_ref, o_ref)

    @pl.when(pl.program_id(0) < in_chunks)
    def _():
        win_o[...] = win_f[...].astype(bf16)


def _ffn(x, norm_g, wg, wu, wd, cast_w_in, *, tm):
    rows = x.shape[0]
    tiles = rows // tm
    row = pl.BlockSpec((tm, D_MODEL), lambda i: (i, 0))
    in_specs = [row] + [_resident(s) for s in _FFN_WEIGHT_SPECS]
    in_chunks = PROJ_W // FF_CHUNK
    assert in_chunks <= tiles
    win = pl.BlockSpec((D_MODEL, FF_CHUNK), lambda i: (0, jnp.minimum(i, in_chunks - 1)))
    return pl.pallas_call(
        functools.partial(_ffn_side_kernel, in_chunks=in_chunks),
        grid=(tiles,),
        in_specs=in_specs + [win],
        out_specs=[row, win],
        out_shape=[jax.ShapeDtypeStruct((rows, D_MODEL), f32), jax.ShapeDtypeStruct((D_MODEL, PROJ_W), bf16)],
        compiler_params=pltpu.CompilerParams(dimension_semantics=("arbitrary",),
                                             vmem_limit_bytes=V7X_VMEM_LIMIT_BYTES),
        name="ffn",
    )(x, norm_g, wg, wu, wd, cast_w_in)


def _swap_halves(x):
    parts = []
    for j in range(x.shape[1] // 128):
        xs = x[:, 128 * j:128 * (j + 1)]
        fwd = pltpu.roll(xs, 32, 1)
        bwd = pltpu.roll(xs, 96, 1)
        lane = lax.broadcasted_iota(jnp.int32, xs.shape, 1)
        parts.append(jnp.where((lane % RET_DK) < RET_DK // 2, bwd, fwd))
    return jnp.concatenate(parts, axis=1)


def _inproj_cast_kernel(h_ref, g_ref, w_ref, cos_ref, sin_ref, wg_f, wu_f, wd_f,
                        q_ref, k_ref, v_ref, sg_ref, p_ref, gate_ref, wg_o, wu_o, wd_o):
    _inproj_kernel(h_ref, g_ref, w_ref, cos_ref, sin_ref, q_ref, k_ref, v_ref, sg_ref, p_ref, gate_ref)

    @pl.when(pl.program_id(0) < FF_CHUNKS)
    def _():
        for w_f, w_o in ((wg_f, wg_o), (wu_f, wu_o), (wd_f, wd_o)):
            w_o[...] = w_f[...].astype(bf16)


def _inproj_kernel(h_ref, g_ref, w_ref, cos_ref, sin_ref, q_ref, k_ref, v_ref, sg_ref, p_ref, gate_ref):
    un = _rms(h_ref[...], g_ref[...]).astype(bf16)
    gate_ref[...] = jax.nn.sigmoid(_dot(un, w_ref[:, _C_GATE:PROJ_W]))
    gr = _dot(un, w_ref[:, _C_G:_C_P])
    sg_ref[...] = gr * jax.nn.sigmoid(gr)
    cos = cos_ref[...]
    sin = sin_ref[...]
    q = _dot(un, w_ref[:, _C_Q:_C_K])
    q_ref[...] = q * cos + _swap_halves(q) * sin
    k = _dot(un, w_ref[:, _C_K:_C_V])
    k_ref[...] = (k * cos + _swap_halves(k) * sin) * (RET_DK ** -0.5)
    v_ref[...] = _dot(un, w_ref[:, _C_V:_C_G]).astype(v_ref.dtype)
    p_ref[...] = _dot(un, w_ref[:, _C_P:_C_GATE])


def _inproj(h, norm_g, w_in, cos, sin, *, tm, cast_ffn=None):
    rows = h.shape[0]
    tiles = rows // tm
    tab_blocks = cos.shape[0] // tm

    def rowspec(width):
        return pl.BlockSpec((tm, width), lambda i: (i, 0))

    tab = pl.BlockSpec((tm, RET_QK), lambda i: (i % tab_blocks, 0))
    widths = (RET_QK, RET_QK, RET_V, RET_V, SHIFT_W, GATE_W)
    dtypes = (f32, f32, bf16, f32, f32, f32)
    in_specs = [rowspec(D_MODEL), _resident((1, D_MODEL)), _resident((D_MODEL, PROJ_W)), tab, tab]
    out_specs = [rowspec(w) for w in widths]
    out_shape = [jax.ShapeDtypeStruct((rows, w), d) for w, d in zip(widths, dtypes)]
    operands = (h, norm_g, w_in, cos, sin)
    if cast_ffn is not None:
        assert FF_CHUNKS <= tiles
        weight_chunks = _ffn_chunk_specs(lambda i: jnp.minimum(i, FF_CHUNKS - 1))
        in_specs += weight_chunks
        out_specs += weight_chunks
        out_shape += _ffn_bf16_shapes()
        operands += tuple(cast_ffn)
    return pl.pallas_call(
        _inproj_kernel if cast_ffn is None else _inproj_cast_kernel,
        grid=(tiles,),
        in_specs=in_specs,
        out_specs=out_specs,
        out_shape=out_shape,
        compiler_params=pltpu.CompilerParams(dimension_semantics=("arbitrary",),
                                             vmem_limit_bytes=V7X_VMEM_LIMIT_BYTES),
        name="inproj",
    )(*operands)


def _ret_kernel(q_ref, k_ref, v_ref, sg_ref, s0_ref, gn_ref, o_ref, s_out_ref, s_scr, o_scr, *, nb, rows, valid):
    c = pl.program_id(1)
    L = rows

    @pl.when(c == 0)
    def _():
        s_scr[...] = jnp.broadcast_to(s0_ref[...], s_scr.shape)

    ii = lax.broadcasted_iota(jnp.int32, (L, L), 0)
    jj = lax.broadcasted_iota(jnp.int32, (L, L), 1)
    diff = (ii - jj).astype(f32)
    row = lax.broadcasted_iota(jnp.int32, (L, 1), 0).astype(f32)
    mask = [jnp.where(diff >= 0, jnp.exp(lg * jnp.maximum(diff, 0.0)), 0.0) for lg in LOG_GAMMA]
    q_decay = [jnp.exp(lg * (row + 1.0)) for lg in LOG_GAMMA]
    k_decay = [jnp.exp(lg * (valid - 1.0 - row)) for lg in LOG_GAMMA]
    s_decay = [float(np.exp(lg * valid)) for lg in LOG_GAMMA]

    chains = [(j, h) for j in range(nb) for h in range(RET_HEADS)]
    qh = [q_ref[j, :, RET_DK * h:RET_DK * (h + 1)] for j, h in chains]
    kh = [k_ref[j, :, RET_DK * h:RET_DK * (h + 1)] for j, h in chains]
    vh = [v_ref[j, :, RET_DV * h:RET_DV * (h + 1)] for j, h in chains]
    scores = [(_dot_nt(qh[i].astype(bf16), kh[i].astype(bf16)) * mask[h]).astype(bf16)
              for i, (j, h) in enumerate(chains)]
    qd = [(qh[i] * q_decay[h]).astype(bf16) for i, (j, h) in enumerate(chains)]
    kd = [(kh[i] * k_decay[h]).astype(bf16) for i, (j, h) in enumerate(chains)]
    s_old = [s_scr[j, h] for j, h in chains]
    for i, (j, h) in enumerate(chains):
        o_scr[j * L:(j + 1) * L, RET_DV * h:RET_DV * (h + 1)] = (
            _dot(scores[i], vh[i]) + _dot(qd[i], s_old[i].astype(bf16)))
    for i, (j, h) in enumerate(chains):
        s_scr[j, h] = s_decay[h] * s_old[i] + _dot_tn(kd[i], vh[i])

    gn = gn_ref[...]
    sg = sg_ref[...].reshape(nb * L, RET_V)
    for h in range(RET_HEADS):
        sl = slice(RET_DV * h, RET_DV * (h + 1))
        o = o_scr[:, sl]
        mu = jnp.mean(o, axis=-1, keepdims=True)
        oc = o - mu
        var = jnp.mean(oc * oc, axis=-1, keepdims=True)
        out = oc * lax.rsqrt(var + RET_GN_EPS) * gn[:, sl] * sg[:, sl]
        o_ref[:, :, sl] = out.reshape(nb, L, RET_DV).astype(o_ref.dtype)

    @pl.when(c == pl.num_programs(1) - 1)
    def _():
        s_out_ref[...] = s_scr[...]


def _retention(q, k, v, sg, s0, gn, *, nb, rows, valid):
    B, T, _ = q.shape
    bcast = s0.shape[0] == 1

    def seq(width):
        return pl.BlockSpec((nb, rows, width), lambda b, c: (b, c, 0))

    state = pl.BlockSpec((nb, RET_HEADS, RET_DK, RET_DV), lambda b, c: (b, 0, 0, 0))
    state_in = pl.BlockSpec((1, RET_HEADS, RET_DK, RET_DV), lambda b, c: (0, 0, 0, 0)) if bcast else state
    return pl.pallas_call(
        functools.partial(_ret_kernel, nb=nb, rows=rows, valid=valid),
        grid=(B // nb, T // rows),
        in_specs=[seq(RET_QK), seq(RET_QK), seq(RET_V), seq(RET_V), state_in, _resident((1, RET_V))],
        out_specs=[seq(RET_V), state],
        out_shape=[jax.ShapeDtypeStruct((B, T, RET_V), bf16),
                   jax.ShapeDtypeStruct((B, RET_HEADS, RET_DK, RET_DV), f32)],
        scratch_shapes=[pltpu.VMEM((nb, RET_HEADS, RET_DK, RET_DV), f32), pltpu.VMEM((nb * rows, RET_V), f32)],
        compiler_params=pltpu.CompilerParams(dimension_semantics=("parallel", "arbitrary"),
                                             vmem_limit_bytes=V7X_VMEM_LIMIT_BYTES),
        name="retention",
    )(q, k, v, sg, s0, gn)


def _head_sums(x, ones_bd):
    rows = x.shape[0]
    width = ones_bd.shape[0]
    groups = RWKV_W // width
    stacked = jnp.concatenate([x[:, width * i:width * (i + 1)] for i in range(groups)], axis=0)
    z = _dot(stacked.astype(bf16), ones_bd)
    return jnp.concatenate([z[rows * i:rows * (i + 1)] for i in range(groups)], axis=1)


def _rwkv_features(p, p_prev, mu_ref, w0_ref, w2_ref, a0_ref, a2_ref, g2_ref, kk_ref, ka_ref, ones_bd):
    pm = p + (p_prev - p) * mu_ref[...]
    r = pm[:, 0:RWKV_W]
    k = pm[:, RWKV_W:2 * RWKV_W]
    v = pm[:, 2 * RWKV_W:3 * RWKV_W]
    o_w = 3 * RWKV_W
    xw = pm[:, o_w:o_w + DECAY_LORA]
    xa = pm[:, o_w + DECAY_LORA:o_w + DECAY_LORA + AAA_LORA]
    xg = pm[:, o_w + DECAY_LORA + AAA_LORA:SHIFT_W]
    z = w0_ref[...] + _dot(jnp.tanh(xw).astype(bf16), w2_ref[...])
    ld = -float(np.exp(-0.5)) * jax.nn.sigmoid(z)
    a = jax.nn.sigmoid(a0_ref[...] + _dot(xa.astype(bf16), a2_ref[...]))
    g = _dot(jax.nn.sigmoid(xg).astype(bf16), g2_ref[...])
    kk = k * kk_ref[...]
    kk = kk * lax.rsqrt(jnp.maximum(_head_sums(kk * kk, ones_bd), 1e-24))
    kp = k * (1.0 + (a - 1.0) * ka_ref[...])
    return r, kp, v, kk, a, ld, g


def _rwkv_output(y, bonus_rkv, g, lng_ref, lnb_ref, ones_bd):
    inv_n = 1.0 / RWKV_HD
    mean = _head_sums(y, ones_bd) * inv_n
    yc = y - mean
    var = _head_sums(yc * yc, ones_bd) * inv_n
    out = yc * lax.rsqrt(var + RWKV_GN_EPS) * lng_ref[...] + lnb_ref[...]
    rkr, v = bonus_rkv
    bonus = _head_sums(rkr, ones_bd) * v
    return (out + bonus) * g


def _rwkv_kernel(p_ref, prev0_ref, h0_ref, mu_ref, w0_ref, w2_ref, a0_ref, a2_ref, g2_ref, kk_ref, ka_ref,
                 rk_ref, lng_ref, lnb_ref, ones_ref, o_ref, h_out_ref, shift_ref, s_scr, prev_scr, y_scr,
                 *, nb, rows, valid):
    c = pl.program_id(1)
    C = rows
    N = RWKV_HD
    R = nb * C
    assert C & (C - 1) == 0
    log2c = C.bit_length() - 1

    @pl.when(c == 0)
    def _():
        for j in range(nb):
            for h in range(RWKV_HEADS):
                s_scr[j, :, N * h:N * (h + 1)] = h0_ref[j if h0_ref.shape[0] > 1 else 0, h]
        prev_scr[...] = jnp.broadcast_to(prev0_ref[...], prev_scr.shape)

    p = p_ref[...].reshape(R, SHIFT_W)
    rowid = lax.broadcasted_iota(jnp.int32, (R, 1), 0)
    step = rowid & (C - 1)
    p_prev = pltpu.roll(p, 1, 0)
    for j in range(nb):
        p_prev = jnp.where(rowid == j * C, prev_scr[j], p_prev)
    for j in range(nb):
        prev_scr[j] = p[j * C + valid - 1:j * C + valid, :]
    ones_bd = ones_ref[...]
    r, kp, v, kk, a, ld, g = _rwkv_features(p, p_prev, mu_ref, w0_ref, w2_ref, a0_ref, a2_ref, g2_ref, kk_ref,
                                            ka_ref, ones_bd)
    if valid < C:
        live = (step < valid).astype(f32)
        ld = ld * live
        kk = kk * live
        kp = kp * live
        v = v * live
    b = kk * a

    cum = ld
    for level in range(log2c):
        reach = 1 << level
        cum = cum + jnp.where(step >= reach, pltpu.roll(cum, reach, 0), 0.0)
    last_rows = [cum[j * C + C - 1:j * C + C, :] for j in range(nb)]
    cum_last = jnp.concatenate([jnp.broadcast_to(x, (C, RWKV_W)) for x in last_rows], axis=0)
    e_in = jnp.exp(cum)
    e_ex = jnp.exp(cum - ld)
    e_neg = jnp.exp(-cum)
    e_end = jnp.exp(cum_last - cum)
    g_end = [jnp.exp(x) for x in last_rows]
    at = -kk * e_ex
    rt = r * e_in
    bt = (b * e_neg).astype(bf16)
    kt = (kp * e_neg).astype(bf16)
    bh = (b * e_end).astype(bf16)
    kh = (kp * e_end).astype(bf16)
    vb = v.astype(bf16)

    G = RWKV_LANE_HEADS
    GW = G * N
    log2n = N.bit_length() - 1
    def head_masks(width, log2_block):
        lane = lax.broadcasted_iota(jnp.int32, (1, width), 1)
        return [((lane >> log2_block) & (G - 1)) == h for h in range(G)]

    head_of_lane = head_masks(GW, log2n)
    head_of_lane2 = head_masks(2 * GW, log2n)
    head_of_col = head_masks(G * C, log2c)
    head_of_col2 = head_masks(2 * G * C, log2c)
    ti = lax.broadcasted_iota(jnp.int32, (C, G * C), 0)
    tj = lax.broadcasted_iota(jnp.int32, (C, G * C), 1) & (C - 1)
    incl = ti >= tj
    strict = ti > tj
    unit = ti == tj
    gi = lax.broadcasted_iota(jnp.int32, (GW, GW), 0)
    gj = lax.broadcasted_iota(jnp.int32, (GW, GW), 1)
    eye_g = gi == gj
    same_head = (gi >> log2n) == (gj >> log2n)

    def bd(a, masks):
        return jnp.concatenate([jnp.where(m, a, jnp.zeros_like(a)) for m in masks], axis=0)

    groups = [(j, q) for j in range(nb) for q in range(RWKV_HEADS // G)]
    n_gr = len(groups)

    def grp(x, j, q):
        return x[j * C:(j + 1) * C, GW * q:GW * (q + 1)]

    ar = [jnp.concatenate([grp(at, j, q), grp(rt, j, q)], axis=0).astype(bf16) for j, q in groups]
    gbk = [_dot_nt(ar[i], jnp.concatenate([bd(grp(bt, j, q), head_of_lane), bd(grp(kt, j, q), head_of_lane)],
                                          axis=0)) for i, (j, q) in enumerate(groups)]
    gb = [x[:, :G * C] for x in gbk]
    gk = [x[:, G * C:] for x in gbk]
    lab = [jnp.where(strict, x[:C], 0.0) for x in gb]
    mrb = [jnp.where(incl, x[C:], 0.0).astype(bf16) for x in gb]
    lmk = [jnp.concatenate([jnp.where(strict, x[:C], 0.0), jnp.where(incl, x[C:], 0.0)], axis=0).astype(bf16)
           for x in gk]
    lmv = [_dot(lmk[i], bd(grp(vb, j, q), head_of_lane)) for i, (j, q) in enumerate(groups)]
    m = [_dot(x.astype(bf16), bd(x.astype(bf16), head_of_col)) for x in lab]
    t_inv = [jnp.where(unit, 1.0, x) for x in lab]
    for level in range(1, log2c):
        mb = [x.astype(bf16) for x in m]
        if level < log2c - 1:
            out = [_dot(mb[i], bd(jnp.concatenate([t_inv[i].astype(bf16), mb[i]], axis=1), head_of_col2))
                   for i in range(n_gr)]
            t_inv = [t_inv[i] + out[i][:, :G * C] for i in range(n_gr)]
            m = [x[:, G * C:] for x in out]
        else:
            t_inv = [t_inv[i] + _dot(mb[i], bd(t_inv[i].astype(bf16), head_of_col)) for i in range(n_gr)]
    wu = [_dot(t_inv[i].astype(bf16),
               bd(jnp.concatenate([grp(at, j, q).astype(bf16), lmv[i][:C].astype(bf16)], axis=1), head_of_lane2)
               ).astype(bf16) for i, (j, q) in enumerate(groups)]
    wa = [x[:, :GW] for x in wu]
    uv = [x[:, GW:] for x in wu]
    pq = [_dot(mrb[i], bd(wu[i], head_of_lane2)) for i in range(n_gr)]
    ry = [grp(rt, j, q) + pq[i][:, :GW] for i, (j, q) in enumerate(groups)]
    y0 = [lmv[i][C:] + pq[i][:, GW:] for i in range(n_gr)]
    gwt = [_dot_tn(wa[i], grp(bh, j, q)) for i, (j, q) in enumerate(groups)]
    dkf = [_dot_tn(jnp.concatenate([uv[i], grp(vb, j, q)], axis=0),
                   jnp.concatenate([grp(bh, j, q), grp(kh, j, q)], axis=0))
           for i, (j, q) in enumerate(groups)]
    for i, (j, q) in enumerate(groups):
        sl = slice(GW * q, GW * (q + 1))
        wg = jnp.where(same_head, gwt[i], 0.0) + jnp.where(eye_g, g_end[j][:, sl], 0.0)
        dk = jnp.where(head_of_lane[0], dkf[i][0:N], 0.0)
        for h in range(1, G):
            dk = dk + jnp.where(head_of_lane[h], dkf[i][N * h:N * (h + 1)], 0.0)
        s_old = s_scr[j, :, sl].astype(bf16)
        y_scr[j * C:(j + 1) * C, sl] = _dot_nt(ry[i].astype(bf16), bd(s_old, head_of_lane)) + y0[i]
        s_scr[j, :, sl] = _dot(s_old, wg.astype(bf16)) + dk

    out = _rwkv_output(y_scr[...], (r * kp * rk_ref[...], v), g, lng_ref, lnb_ref, ones_bd)
    o_ref[...] = out.reshape(nb, C, RWKV_W).astype(o_ref.dtype)

    @pl.when(c == pl.num_programs(1) - 1)
    def _():
        for j in range(nb):
            for h in range(RWKV_HEADS):
                h_out_ref[j, h] = s_scr[j, :, N * h:N * (h + 1)]
        shift_ref[...] = prev_scr[...]


def _rwkv(p, prev0, h0, params, ones_bd, *, nb, rows, valid):
    B, T, _ = p.shape

    def maybe_bcast(arr, tail):
        nd = len(tail) + 1
        if arr.shape[0] == 1:
            return pl.BlockSpec((1,) + tail, lambda b, c: (0,) * nd)
        return pl.BlockSpec((nb,) + tail, lambda b, c: (b,) + (0,) * (nd - 1))

    seq_in = pl.BlockSpec((nb, rows, SHIFT_W), lambda b, c: (b, c, 0))
    seq_out = pl.BlockSpec((nb, rows, RWKV_W), lambda b, c: (b, c, 0))
    st_tail = (RWKV_HEADS, RWKV_HD, RWKV_HD)
    sh_tail = (1, SHIFT_W)
    param_specs = [_resident(x.shape) for x in params]
    return pl.pallas_call(
        functools.partial(_rwkv_kernel, nb=nb, rows=rows, valid=valid),
        grid=(B // nb, T // rows),
        in_specs=[seq_in, maybe_bcast(prev0, sh_tail), maybe_bcast(h0, st_tail)] + param_specs
                 + [_resident(ones_bd.shape)],
        out_specs=[seq_out, pl.BlockSpec((nb,) + st_tail, lambda b, c: (b, 0, 0, 0)),
                   pl.BlockSpec((nb,) + sh_tail, lambda b, c: (b, 0, 0))],
        out_shape=[jax.ShapeDtypeStruct((B, T, RWKV_W), bf16),
                   jax.ShapeDtypeStruct((B,) + st_tail, f32),
                   jax.ShapeDtypeStruct((B,) + sh_tail, f32)],
        scratch_shapes=[pltpu.VMEM((nb, RWKV_HD, RWKV_W), f32), pltpu.VMEM((nb,) + sh_tail, f32),
                        pltpu.VMEM((nb * rows, RWKV_W), f32)],
        compiler_params=pltpu.CompilerParams(dimension_semantics=("parallel", "arbitrary"),
                                             vmem_limit_bytes=V7X_VMEM_LIMIT_BYTES),
        name="rwkv7",
    )(p, prev0, h0, *params, ones_bd)


def _rwkv_step_kernel(p_ref, shift0_ref, s_ref, mu_ref, w0_ref, w2_ref, a0_ref, a2_ref, g2_ref, kk_ref, ka_ref,
                      rk_ref, lng_ref, lnb_ref, ones_ref, o_ref, s_out_ref, shift_ref,
                      feat_scr, y_scr, g_scr, rkr_scr, v_scr, *, steps, batch):
    h = pl.program_id(0)
    T, B, N = steps, batch, RWKV_HD
    key_tiles = N // 8
    f_r, f_w, f_k, f_v, f_a, f_b = range(6)

    @pl.when(h == 0)
    def _():
        p = p_ref[...]
        p_prev = jnp.concatenate([shift0_ref[...], p[:(T - 1) * B]], axis=0)
        r, kp, v, kk, a, ld, g = _rwkv_features(p, p_prev, mu_ref, w0_ref, w2_ref, a0_ref, a2_ref, g2_ref,
                                                kk_ref, ka_ref, ones_ref[...])
        for idx, x in enumerate((r, jnp.exp(ld), kp, v, -kk, kk * a)):
            for t in range(T):
                feat_scr[idx, t] = x[t * B:(t + 1) * B, :].T
        g_scr[...] = g
        rkr_scr[...] = r * kp * rk_ref[...]
        v_scr[...] = v
        shift_ref[...] = p[(T - 1) * B:]

    base = pl.multiple_of(h * N, N)

    def tile(idx, t, kt):
        return feat_scr[idx, t, pl.ds(pl.multiple_of(base + 8 * kt, 8), 8), :]

    def keysum(s, idx, t):
        acc = s[0] * tile(idx, t, 0)
        for kt in range(1, key_tiles):
            acc = acc + s[kt] * tile(idx, t, kt)
        return jnp.sum(acc, axis=0, keepdims=True)

    def value_group(vg, carry):
        v_rows = pl.ds(pl.multiple_of(base + 8 * vg, 8), 8)
        v_tiles = [feat_scr[f_v, t, v_rows, :] for t in range(T)]
        y_rows = [[] for _ in range(T)]
        for i in range(8):
            row0 = pl.multiple_of((8 * vg + i) * N, N)
            s = [s_ref[0, pl.ds(row0 + 8 * kt, 8), :] for kt in range(key_tiles)]
            for t in range(T):
                sa = jnp.broadcast_to(keysum(s, f_a, t), (8, B))
                vv = jnp.broadcast_to(v_tiles[t][i:i + 1, :], (8, B))
                s = [s[kt] * tile(f_w, t, kt) + sa * tile(f_b, t, kt) + vv * tile(f_k, t, kt)
                     for kt in range(key_tiles)]
                y_rows[t].append(keysum(s, f_r, t))
            for kt in range(key_tiles):
                s_out_ref[0, pl.ds(row0 + 8 * kt, 8), :] = s[kt]
        for t in range(T):
            y_scr[t, v_rows, :] = jnp.concatenate(y_rows[t], axis=0)
        return carry

    lax.fori_loop(0, N // 8, value_group, 0)

    @pl.when(h == pl.num_programs(0) - 1)
    def _():
        y = jnp.concatenate([y_scr[t].T for t in range(T)], axis=0)
        out = _rwkv_output(y, (rkr_scr[...], v_scr[...]), g_scr[...], lng_ref, lnb_ref, ones_ref[...])
        o_ref[...] = out.astype(o_ref.dtype)


def _rwkv_step(p, shift0, state, params, ones_bd, *, steps):
    B = state.shape[0]
    rows = steps * B
    n_state = RWKV_HD * RWKV_HD
    s_in = jnp.transpose(state, (1, 2, 3, 0)).reshape(RWKV_HEADS, n_state, B)
    s_spec = pl.BlockSpec((1, n_state, B), lambda h: (h, 0, 0))
    o, s_out, shift = pl.pallas_call(
        functools.partial(_rwkv_step_kernel, steps=steps, batch=B),
        grid=(RWKV_HEADS,),
        in_specs=[_resident((rows, SHIFT_W)), _resident((B, SHIFT_W)), s_spec]
                 + [_resident(x.shape) for x in params] + [_resident(ones_bd.shape)],
        out_specs=[pl.BlockSpec((rows, RWKV_W), lambda h: (0, 0)), s_spec,
                   pl.BlockSpec((B, SHIFT_W), lambda h: (0, 0))],
        out_shape=[jax.ShapeDtypeStruct((rows, RWKV_W), bf16),
                   jax.ShapeDtypeStruct((RWKV_HEADS, n_state, B), f32),
                   jax.ShapeDtypeStruct((B, SHIFT_W), f32)],
        scratch_shapes=[pltpu.VMEM((6, steps, RWKV_W, B), f32), pltpu.VMEM((steps, RWKV_W, B), f32),
                        pltpu.VMEM((rows, RWKV_W), f32), pltpu.VMEM((rows, RWKV_W), f32),
                        pltpu.VMEM((rows, RWKV_W), f32)],
        compiler_params=pltpu.CompilerParams(dimension_semantics=("arbitrary",),
                                             vmem_limit_bytes=V7X_VMEM_LIMIT_BYTES),
        name="rwkv7_step",
    )(p, shift0, s_in, *params, ones_bd)
    s_out = jnp.transpose(s_out.reshape(RWKV_HEADS, RWKV_HD, RWKV_HD, B), (3, 0, 1, 2))
    return o, s_out, shift


def _merge_ffn_kernel(h_ref, oret_ref, orwkv_ref, gate_ref, wr_ref, ww_ref, wo_ref,
                      g_ref, wg_ref, wu_ref, wd_ref, fin_ref, o_ref):
    a = _dot(oret_ref[...], wr_ref[...])
    b = _dot(orwkv_ref[...], ww_ref[...])
    merged = gate_ref[:, :D_MODEL] * a + gate_ref[:, D_MODEL:] * b
    h = h_ref[...] + _dot(merged.astype(bf16), wo_ref[...])
    h = _swiglu_half_step(h, g_ref, _column_chunks(wg_ref), _column_chunks(wu_ref), _row_chunks(wd_ref))
    o_ref[...] = _rms(h, fin_ref[...])


def _merge_ffn(h, o_ret, o_rwkv, gates, w_out_ret, w_out_rwkv, w_out, norm_g, wg, wu, wd, fin_g, *, tm):
    rows = o_ret.shape[0]
    assert rows % tm == 0 and o_rwkv.shape[0] == rows and h.shape[0] >= rows and gates.shape[0] >= rows

    def rowspec(width):
        return pl.BlockSpec((tm, width), lambda i: (i, 0))

    return pl.pallas_call(
        _merge_ffn_kernel,
        grid=(rows // tm,),
        in_specs=[rowspec(D_MODEL), rowspec(RET_V), rowspec(RWKV_W), rowspec(GATE_W),
                  _resident((RET_V, D_MODEL)), _resident((RWKV_W, D_MODEL)), _resident((D_MODEL, D_MODEL))]
                 + [_resident(s) for s in _FFN_WEIGHT_SPECS] + [_resident((1, D_MODEL))],
        out_specs=rowspec(D_MODEL),
        out_shape=jax.ShapeDtypeStruct((rows, D_MODEL), f32),
        compiler_params=pltpu.CompilerParams(dimension_semantics=("parallel",),
                                             vmem_limit_bytes=V7X_VMEM_LIMIT_BYTES),
        name="merge_ffn",
    )(h, o_ret, o_rwkv, gates, w_out_ret, w_out_rwkv, w_out, norm_g, wg, wu, wd, fin_g)


def _rotary_tables(pos):
    half = RET_DK // 2
    inv_freq = ROPE_BASE ** (-jnp.arange(half, dtype=f32) / half)
    ang = pos.astype(f32)[:, None] * inv_freq[None, :]
    cos = jnp.cos(ang)
    sin = jnp.sin(ang)
    cos_t = jnp.tile(jnp.concatenate([cos, cos], axis=1), (1, RET_HEADS))
    sin_t = jnp.tile(jnp.concatenate([-sin, sin], axis=1), (1, RET_HEADS))
    return cos_t, sin_t


def _row_tile(rows, target):
    tm = min(rows, target)
    while rows % tm:
        tm -= 8
    return tm


def kernel(x_prompt, x_sample, state_ret, state_wkv, state_shift, meta_tokens, ffn1_norm, ffn1_w_gate, ffn1_w_up, ffn1_w_down, mix_norm, w_in, ret_gn_g, mu_shift, w0, w2, a0, a2, g2, k_k, k_a, r_k, lnx_g, lnx_b, w_out_ret, w_out_rwkv, w_out, ffn2_norm, ffn2_w_gate, ffn2_w_up, ffn2_w_down, final_norm):
    Bp, Tp, _ = x_prompt.shape
    Bs, Ts, _ = x_sample.shape
    st_dtype = state_ret.dtype

    def row(x):
        return x.reshape(1, -1).astype(f32)

    fin = row(final_norm)
    mix_g = row(mix_norm)
    wr_b, ww_b, wo_b = w_out_ret.astype(bf16), w_out_rwkv.astype(bf16), w_out.astype(bf16)
    rwkv_params = (row(mu_shift), row(w0), w2.astype(bf16), row(a0), a2.astype(bf16), g2.astype(bf16),
                   row(k_k), row(k_a), row(r_k), row(lnx_g), row(lnx_b))
    head_id = jnp.arange(HEAD_SUM_LANES, dtype=jnp.int32) // RWKV_HD
    ones_bd = (head_id[:, None] == head_id[None, :]).astype(bf16)
    gn = row(ret_gn_g)

    n_s = Bs * Ts
    x_small = jnp.concatenate([jnp.swapaxes(x_sample, 0, 1).reshape(n_s, D_MODEL),
                               meta_tokens.astype(x_sample.dtype)], axis=0)
    cos_s, sin_s = _rotary_tables(PAST_LEN + jnp.arange(Ts, dtype=jnp.int32))
    cos_m, sin_m = _rotary_tables(jnp.arange(N_META, dtype=jnp.int32))
    cos_small = jnp.concatenate([jnp.repeat(cos_s, Bs, axis=0), cos_m], axis=0)
    sin_small = jnp.concatenate([jnp.repeat(sin_s, Bs, axis=0), sin_m], axis=0)
    n_small = n_s + N_META
    h_small, *ffn1_b = _ffn_stream(x_small, row(ffn1_norm), ffn1_w_gate, ffn1_w_up, ffn1_w_down)

    n_p = Bp * Tp
    h_p, w_in_b = _ffn(x_prompt.reshape(n_p, D_MODEL), row(ffn1_norm), *ffn1_b, w_in, tm=_row_tile(n_p, 512))
    q_s, k_s, v_s, sg_s, p_s, gate_s = _inproj(h_small, mix_g, w_in_b, cos_small, sin_small, tm=n_small)

    def meta(x):
        return x[n_s:].reshape(1, N_META, -1)

    zeros_ret = jnp.zeros((1, RET_HEADS, RET_DK, RET_DV), f32)
    _, s_ret_m = _retention(meta(q_s), meta(k_s), meta(v_s), meta(sg_s), zeros_ret, gn,
                            nb=1, rows=N_META, valid=N_META)
    _, h_wkv_m, shift_m = _rwkv(meta(p_s), jnp.zeros((1, 1, SHIFT_W), f32),
                                jnp.zeros((1, RWKV_HEADS, RWKV_HD, RWKV_HD), f32), rwkv_params, ones_bd,
                                nb=1, rows=N_META, valid=N_META)

    pad_t = -(-Ts // 8) * 8

    def samp(x):
        x = jnp.swapaxes(x[:n_s].reshape(Ts, Bs, -1), 0, 1)
        return jnp.pad(x, ((0, 0), (0, pad_t - Ts), (0, 0)))

    o_ret_s, s_ret_s = _retention(samp(q_s), samp(k_s), samp(v_s), samp(sg_s), state_ret.astype(f32), gn,
                                  nb=RET_SAMPLE_SEQS, rows=pad_t, valid=Ts)
    o_ret_s = jnp.swapaxes(o_ret_s[:, :Ts], 0, 1).reshape(n_s, RET_V)
    o_wkv_s, h_wkv_s, shift_s = _rwkv_step(p_s, state_shift.astype(f32), state_wkv.astype(f32),
                                           rwkv_params, ones_bd, steps=Ts)

    cos_p, sin_p = _rotary_tables(N_META + jnp.arange(Tp, dtype=jnp.int32))
    q_p, k_p, v_p, sg_p, p_p, gate_p, *ffn2_b = _inproj(h_p, mix_g, w_in_b, cos_p, sin_p, tm=_row_tile(Tp, 512),
                                                        cast_ffn=(ffn2_w_gate, ffn2_w_up, ffn2_w_down))

    def post(h, o_ret, o_rwkv, gates, tm):
        return _merge_ffn(h, o_ret, o_rwkv, gates, wr_b, ww_b, wo_b, row(ffn2_norm), *ffn2_b, fin, tm=tm)

    y_sample = post(h_small, o_ret_s, o_wkv_s, gate_s, _row_tile(n_s, 512))
    y_sample = jnp.swapaxes(y_sample.reshape(Ts, Bs, D_MODEL), 0, 1)

    def seqs(x):
        return x.reshape(Bp, Tp, -1)

    o_ret_p, s_ret_p = _retention(seqs(q_p), seqs(k_p), seqs(v_p), seqs(sg_p), s_ret_m, gn,
                                  nb=RET_PROMPT_SEQS, rows=RET_CHUNK, valid=RET_CHUNK)
    o_wkv_p, h_wkv_p, shift_p = _rwkv(seqs(p_p), shift_m, h_wkv_m, rwkv_params, ones_bd,
                                      nb=RWKV_PROMPT_SEQS, rows=RWKV_CHUNK, valid=RWKV_CHUNK)
    y_prompt = post(h_p, o_ret_p.reshape(n_p, RET_V), o_wkv_p.reshape(n_p, RWKV_W), gate_p, _row_tile(n_p, 512))

    return (y_prompt.reshape(Bp, Tp, D_MODEL).astype(x_prompt.dtype),
            y_sample.reshape(Bs, Ts, D_MODEL).astype(x_sample.dtype),
            s_ret_p.astype(st_dtype), h_wkv_p.astype(st_dtype),
            shift_p.reshape(Bp, SHIFT_W).astype(st_dtype),
            s_ret_s.astype(st_dtype), h_wkv_s.astype(st_dtype),
            shift_s.reshape(Bs, SHIFT_W).astype(st_dtype))
```
